```python
import math
import jax, jax.numpy as jnp
from jax import lax
import numpy as np

D_MODEL = 2048
BATCH = 2
SEQ = 8192
DEPTH = 4
DEC_BATCH = 8
DEC_SEQ = 32
PAST_LEN = 4096

CHUNK = 64
QBLOCK = 128
LN_EPS = 1e-5
D_MIX = D_MODEL
DH_A = 64
H_A = D_MIX // 4 // DH_A
W_A = H_A * DH_A
DK_B = 64
DV_B = 64
H_B = D_MIX // 4 // DV_B
W_B = H_B * DV_B
W_C = D_MIX - W_A - W_B
P_C = 64
H_C = W_C // P_C
N_C = 128
G_C = 2
HG_C = H_C // G_C
CONV_W = 4
CONV_DIM = W_C + 2 * G_C * N_C
SPLIT_SIZES = (W_A, W_A, W_A, H_A, H_B * DK_B, H_B * DK_B, W_B, W_B, W_C, CONV_DIM, H_C)
P_IN = sum(SPLIT_SIZES)
ROPE_BASE = 10000.0
N_MEM = 256
H_M = 4
DH_M = 128
W_M = H_M * DH_M
PEER_NK = 128
PEER_N = PEER_NK * PEER_NK
PEER_HEADS = 8
PEER_DK = 256
PEER_TOPK = 16
PEER_BLOCK = 128
ALPHA = (2 * DEPTH) ** 0.25
BETA = (8 * DEPTH) ** -0.25
FOX_BIAS_INIT = 3.0

kernel_name = 'hybrid_stream_encoder_step'

f32 = jnp.float32


def layer_norm(x, g, b):
    xf = x.astype(f32)
    mu = xf.mean(-1, keepdims=True)
    var = jnp.square(xf - mu).mean(-1, keepdims=True)
    return ((xf - mu) * lax.rsqrt(var + LN_EPS) * g + b).astype(x.dtype)


def rms_norm(x, w):
    xf = x.astype(f32)
    return xf * lax.rsqrt(jnp.mean(xf * xf, -1, keepdims=True) + LN_EPS) * w


def split_proj(h):
    idx = np.cumsum(SPLIT_SIZES)[:-1].tolist()
    return jnp.split(h, idx, axis=-1)


def rotary(x, pos):
    half = x.shape[-1] // 2
    inv = ROPE_BASE ** (-jnp.arange(half, dtype=f32) / half)
    ang = pos.astype(f32)[:, None] * inv[None, :]
    cos = jnp.cos(ang)[None, :, None, :]
    sin = jnp.sin(ang)[None, :, None, :]
    x1, x2 = x[..., :half], x[..., half:]
    return jnp.concatenate([x1 * cos - x2 * sin, x2 * cos + x1 * sin], -1)


def fox_prompt(q, k, v, logf):
    B, S, H, D = q.shape
    c = jnp.cumsum(logf, axis=1)
    c_keys = c.transpose(0, 2, 1)
    nb = S // QBLOCK
    qb = q.reshape(B, nb, QBLOCK, H, D).swapaxes(0, 1)
    cb = c.reshape(B, nb, QBLOCK, H).swapaxes(0, 1)
    pos_k = jnp.arange(S)
    scale = DH_A ** -0.5

    def one(args):
        i, qi, ci = args
        s = jnp.einsum('bqhd,bkhd->bhqk', qi, k).astype(f32) * scale
        s = s + ci.transpose(0, 2, 1)[..., None] - c_keys[:, :, None, :]
        pos_q = i * QBLOCK + jnp.arange(QBLOCK)
        s = jnp.where(pos_k[None, :] <= pos_q[:, None], s, -jnp.inf)
        p = jax.nn.softmax(s, axis=-1)
        return jnp.einsum('bhqk,bkhd->bqhd', p.astype(v.dtype), v)

    out = lax.map(one, (jnp.arange(nb), qb, cb))
    return out.swapaxes(0, 1).reshape(B, S, H, D)


def fox_sample(q, k, v, logf, ck, cv, clogf):
    P = ck.shape[1]
    L = q.shape[1]
    k_all = jnp.concatenate([ck, k], 1)
    v_all = jnp.concatenate([cv, v], 1)
    c = jnp.cumsum(jnp.concatenate([clogf.astype(f32), logf], 1), axis=1)
    cq = c[:, P:]
    s = jnp.einsum('bqhd,bkhd->bhqk', q, k_all).astype(f32) * (DH_A ** -0.5)
    s = s + cq.transpose(0, 2, 1)[..., None] - c.transpose(0, 2, 1)[:, :, None, :]
    mask = jnp.arange(P + L)[None, :] <= (P + jnp.arange(L))[:, None]
    s = jnp.where(mask, s, -jnp.inf)
    p = jax.nn.softmax(s, axis=-1)
    return jnp.einsum('bhqk,bkhd->bqhd', p.astype(v_all.dtype), v_all)


def ret_log_gamma():
    return jnp.log1p(-jnp.exp2(-5.0 - jnp.arange(H_B, dtype=f32)))


def retention_chunk(S0, q, k, v):
    lg = ret_log_gamma()
    L = q.shape[1]
    i = jnp.arange(L, dtype=f32)
    diff = i[:, None] - i[None, :]
    dec = jnp.exp(jnp.where((diff >= 0)[None], diff[None] * lg[:, None, None], -jnp.inf))
    attn = jnp.einsum('blhd,bmhd->bhlm', q, k) * dec[None]
    intra = jnp.einsum('bhlm,bmhe->blhe', attn, v)
    cross = jnp.einsum('blhd,bhde->blhe', q, S0) * jnp.exp((i[:, None] + 1.0) * lg[None, :])[None, :, :, None]
    kd = k * jnp.exp((L - 1.0 - i)[:, None] * lg[None, :])[None, :, :, None]
    S1 = jnp.exp(L * lg)[None, :, None, None] * S0 + jnp.einsum('blhd,blhe->bhde', kd, v)
    return S1, intra + cross


def retention_prompt(q, k, v):
    B, S = q.shape[:2]
    nC = S // CHUNK
    to_chunks = lambda t: t.reshape(B, nC, CHUNK, *t.shape[2:]).swapaxes(0, 1)
    S0 = jnp.zeros((B, H_B, DK_B, DV_B), f32)
    S1, out = lax.scan(lambda s, xs: retention_chunk(s, *xs), S0, (to_chunks(q), to_chunks(k), to_chunks(v)))
    return S1, out.swapaxes(0, 1).reshape(B, S, H_B, DV_B)


def retention_mix(rq, rk, rv, rg, pos, gn_w, S0):
    B, L, _ = rq.shape
    q = rotary(rq.reshape(B, L, H_B, DK_B).astype(f32), pos)
    k = rotary(rk.reshape(B, L, H_B, DK_B).astype(f32), pos) * (DK_B ** -0.5)
    v = rv.reshape(B, L, H_B, DV_B).astype(f32)
    if S0 is None:
        S1, o = retention_prompt(q, k, v)
    else:
        S1, o = retention_chunk(S0.astype(f32), q, k, v)
    mu = o.mean(-1, keepdims=True)
    var = jnp.square(o - mu).mean(-1, keepdims=True)
    o = ((o - mu) * lax.rsqrt(var + LN_EPS)).reshape(B, L, W_B) * gn_w
    y = jax.nn.silu(rg.astype(f32)) * o
    return y.astype(rq.dtype), S1


def ssd_chunk(h0, x, dt, Bm, Cm, A):
    L = x.shape[1]
    a_cs = jnp.cumsum(dt * A, axis=1)
    idx = jnp.arange(L)
    mask = idx[:, None] >= idx[None, :]
    seg = a_cs[:, :, None] - a_cs[:, None, :]
    decay = jnp.exp(jnp.where(mask[None, :, :, None, None], seg, -jnp.inf))
    cb = jnp.einsum('blgn,bmgn->blmg', Cm, Bm)
    w = cb[..., None] * decay * dt[:, None]
    y = jnp.einsum('blmgh,bmghp->blghp', w, x)
    y = y + jnp.einsum('blgn,bghpn->blghp', Cm, h0) * jnp.exp(a_cs)[..., None]
    to_end = jnp.exp(a_cs[:, -1:] - a_cs) * dt
    h1 = jnp.exp(a_cs[:, -1])[..., None, None] * h0 + jnp.einsum('blgh,blghp,blgn->bghpn', to_end, x, Bm)
    return h1, y


def ssm_mix(z, xbc, dtr, conv_w, conv_b, dt_bias, a_log, d_skip, norm_w, conv_state, h0):
    B, L, _ = xbc.shape
    if conv_state is None:
        hist = jnp.zeros((B, CONV_W - 1, CONV_DIM), xbc.dtype)
    else:
        hist = conv_state
    xp = jnp.concatenate([hist, xbc], 1)
    new_conv = xp[:, -(CONV_W - 1):]
    conv = conv_b + sum(xp[:, t:t + L] * conv_w[t] for t in range(CONV_W))
    xbc_c = jax.nn.silu(conv.astype(f32))
    xs = xbc_c[..., :W_C].reshape(B, L, G_C, HG_C, P_C)
    Bm = xbc_c[..., W_C:W_C + G_C * N_C].reshape(B, L, G_C, N_C)
    Cm = xbc_c[..., W_C + G_C * N_C:].reshape(B, L, G_C, N_C)
    dt = jax.nn.softplus(dtr.astype(f32) + dt_bias).reshape(B, L, G_C, HG_C)
    A = -jnp.exp(a_log.astype(f32)).reshape(G_C, HG_C)
    if h0 is None:
        nC = L // CHUNK
        to_chunks = lambda t: t.reshape(B, nC, CHUNK, *t.shape[2:]).swapaxes(0, 1)
        hz = jnp.zeros((B, G_C, HG_C, P_C, N_C), f32)
        h1, y = lax.scan(lambda h, a: ssd_chunk(h, *a, A), hz,
                         (to_chunks(xs), to_chunks(dt), to_chunks(Bm), to_chunks(Cm)))
        y = y.swapaxes(0, 1).reshape(B, L, G_C, HG_C, P_C)
    else:
        h1, y = ssd_chunk(h0.astype(f32).reshape(B, G_C, HG_C, P_C, N_C), xs, dt, Bm, Cm, A)
    y = y + d_skip.reshape(G_C, HG_C)[..., None] * xs
    y = rms_norm(y.reshape(B, L, W_C) * jax.nn.silu(z.astype(f32)), norm_w)
    return y.astype(z.dtype), h1.reshape(B, H_C, P_C, N_C), new_conv


def mem_attn(x, mk, mv, wq, wo):
    B, L, _ = x.shape
    q = (x @ wq).reshape(B, L, H_M, DH_M)
    s = jnp.einsum('blhd,bmhd->bhlm', q, mk).astype(f32) * (DH_M ** -0.5)
    p = jax.nn.softmax(s, axis=-1)
    o = jnp.einsum('bhlm,bmhd->blhd', p.astype(mv.dtype), mv)
    return o.reshape(B, L, W_M) @ wo


def peer(x, wq, k1, k2, u, v):
    B, L, D = x.shape
    T = B * L
    nb = -(-T // PEER_BLOCK)
    t = jnp.pad(x.reshape(T, D), ((0, nb * PEER_BLOCK - T), (0, 0)))

    def one(tb):
        q = (tb @ wq).reshape(PEER_BLOCK, PEER_HEADS, 2, PEER_DK // 2).astype(f32)
        s1 = jnp.einsum('thd,kd->thk', q[:, :, 0], k1.astype(f32))
        s2 = jnp.einsum('thd,kd->thk', q[:, :, 1], k2.astype(f32))
        v1, i1 = lax.top_k(s1, PEER_TOPK)
        v2, i2 = lax.top_k(s2, PEER_TOPK)
        cand_s = (v1[..., :, None] + v2[..., None, :]).reshape(PEER_BLOCK, PEER_HEADS, PEER_TOPK * PEER_TOPK)
        cand_i = (i1[..., :, None] * PEER_NK + i2[..., None, :]).reshape(PEER_BLOCK, PEER_HEADS, PEER_TOPK * PEER_TOPK)
        sc, sel = lax.top_k(cand_s, PEER_TOPK)
        idx = jnp.take_along_axis(cand_i, sel, axis=-1)
        g = jax.nn.softmax(sc, axis=-1)
        ue = jnp.take(u, idx, axis=0)
        ve = jnp.take(v, idx, axis=0)
        a = jax.nn.gelu(jnp.einsum('td,thkd->thk', tb, ue).astype(f32), approximate=False)
        return jnp.einsum('thk,thkd->td', (g * a).astype(ve.dtype), ve)

    out = lax.map(one, t.reshape(nb, PEER_BLOCK, D))
    return out.reshape(nb * PEER_BLOCK, D)[:T].reshape(B, L, D)


def trunk_layer(x, pos, prm, mem_k, mem_v, cache):
    (w_in, fox_fb, ret_gn_w, conv_w, conv_b, dt_bias, a_log, d_skip, ssm_norm_w, w_out,
     ln1_g, ln1_b, wq_mem, wo_mem, ln2_g, ln2_b,
     peer_wq, peer_k1, peer_k2, peer_u, peer_v, ln3_g, ln3_b) = prm
    B, L, _ = x.shape
    fq, fk, fv, ff, rq, rk, rv, rg, z, xbc, dtr = split_proj(x @ w_in)
    q = fq.reshape(B, L, H_A, DH_A)
    k = fk.reshape(B, L, H_A, DH_A)
    v = fv.reshape(B, L, H_A, DH_A)
    logf = jax.nn.log_sigmoid(ff.astype(f32) + fox_fb)
    if cache is None:
        ao = fox_prompt(q, k, v, logf)
        s_ret0, h0, cs = None, None, None
    else:
        ck, cv, clogf, s_ret0, h0, cs = cache
        ao = fox_sample(q, k, v, logf, ck, cv, clogf)
    bo, s_ret = retention_mix(rq, rk, rv, rg, pos, ret_gn_w, s_ret0)
    co, s_ssm, s_conv = ssm_mix(z, xbc, dtr, conv_w, conv_b, dt_bias, a_log, d_skip, ssm_norm_w, cs, h0)
    mix = jnp.concatenate([ao.reshape(B, L, W_A), bo, co], -1) @ w_out
    x = layer_norm(ALPHA * x + mix, ln1_g, ln1_b)
    x = layer_norm(ALPHA * x + mem_attn(x, mem_k, mem_v, wq_mem, wo_mem), ln2_g, ln2_b)
    x = layer_norm(ALPHA * x + peer(x, peer_wq, peer_k1, peer_k2, peer_u, peer_v), ln3_g, ln3_b)
    return x, (k, v, logf, s_ret, s_ssm, s_conv)


def setup_inputs(seed: int = 0) -> dict:
    key = jax.random.key(seed)
    ks = iter(jax.random.split(key, 64))
    nrm = lambda shape, scale=1.0: scale * jax.random.normal(next(ks), shape, f32)
    gain = lambda shape: 1.0 + nrm(shape, 0.02)
    D = D_MODEL
    x_prompt = nrm((BATCH, SEQ, D))
    x_sample = nrm((DEC_BATCH, DEC_SEQ, D))
    cache_fox_k = nrm((DEPTH, DEC_BATCH, PAST_LEN, H_A, DH_A))
    cache_fox_v = nrm((DEPTH, DEC_BATCH, PAST_LEN, H_A, DH_A))
    cache_fox_logf = jax.nn.log_sigmoid(FOX_BIAS_INIT + nrm((DEPTH, DEC_BATCH, PAST_LEN, H_A)))
    state_ret = nrm((DEPTH, DEC_BATCH, H_B, DK_B, DV_B))
    state_ssm = nrm((DEPTH, DEC_BATCH, H_C, P_C, N_C), 0.5)
    state_conv = nrm((DEPTH, DEC_BATCH, CONV_W - 1, CONV_DIM))
    cache_mem_k = nrm((DEPTH, DEC_BATCH, N_MEM, H_M, DH_M))
    cache_mem_v = nrm((DEPTH, DEC_BATCH, N_MEM, H_M, DH_M))
    mem_prompt = nrm((BATCH, N_MEM, D))
    ln_in_g = gain((D,))
    ln_in_b = nrm((D,), 0.02)
    w_in = nrm((DEPTH, D, P_IN), D ** -0.5)
    fox_fb = FOX_BIAS_INIT + nrm((DEPTH, H_A), 0.5)
    ret_gn_w = gain((DEPTH, W_B))
    conv_w = nrm((DEPTH, CONV_W, CONV_DIM), CONV_W ** -0.5)
    conv_b = nrm((DEPTH, CONV_DIM), 0.02)
    dt = jnp.exp(jax.random.uniform(next(ks), (DEPTH, H_C), f32, math.log(1e-3), math.log(1e-1)))
    dt_bias = dt + jnp.log(-jnp.expm1(-dt))
    a_log = jnp.log(jax.random.uniform(next(ks), (DEPTH, H_C), f32, 1.0, 16.0))
    d_skip = gain((DEPTH, H_C))
    ssm_norm_w = gain((DEPTH, W_C))
    w_out = nrm((DEPTH, D_MIX, D), BETA * D_MIX ** -0.5)
    ln1_g = gain((DEPTH, D))
    ln1_b = nrm((DEPTH, D), 0.02)
    wq_mem = nrm((DEPTH, D, W_M), D ** -0.5)
    wkv_mem = jnp.concatenate([nrm((DEPTH, D, W_M), D ** -0.5), nrm((DEPTH, D, W_M), BETA * D ** -0.5)], -1)
    wo_mem = nrm((DEPTH, W_M, D), BETA * W_M ** -0.5)
    ln2_g = gain((DEPTH, D))
    ln2_b = nrm((DEPTH, D), 0.02)
    peer_wq = nrm((DEPTH, D, PEER_HEADS * PEER_DK), D ** -0.5)
    peer_k1 = nrm((DEPTH, PEER_NK, PEER_DK // 2), (PEER_DK // 2) ** -0.5)
    peer_k2 = nrm((DEPTH, PEER_NK, PEER_DK // 2), (PEER_DK // 2) ** -0.5)
    peer_u = nrm((DEPTH, PEER_N, D), D ** -0.5)
    peer_v = nrm((DEPTH, PEER_N, D), BETA * PEER_HEADS ** -0.5)
    ln3_g = gain((DEPTH, D))
    ln3_b = nrm((DEPTH, D), 0.02)
    return {'x_prompt': x_prompt, 'x_sample': x_sample,
            'cache_fox_k': cache_fox_k, 'cache_fox_v': cache_fox_v, 'cache_fox_logf': cache_fox_logf,
            'state_ret': state_ret, 'state_ssm': state_ssm, 'state_conv': state_conv,
            'cache_mem_k': cache_mem_k, 'cache_mem_v': cache_mem_v, 'mem_prompt': mem_prompt,
            'ln_in_g': ln_in_g, 'ln_in_b': ln_in_b, 'w_in': w_in, 'fox_fb': fox_fb, 'ret_gn_w': ret_gn_w,
            'conv_w': conv_w, 'conv_b': conv_b, 'dt_bias': dt_bias, 'a_log': a_log, 'd_skip': d_skip,
            'ssm_norm_w': ssm_norm_w, 'w_out': w_out, 'ln1_g': ln1_g, 'ln1_b': ln1_b,
            'wq_mem': wq_mem, 'wkv_mem': wkv_mem, 'wo_mem': wo_mem, 'ln2_g': ln2_g, 'ln2_b': ln2_b,
            'peer_wq': peer_wq, 'peer_k1': peer_k1, 'peer_k2': peer_k2, 'peer_u': peer_u, 'peer_v': peer_v,
            'ln3_g': ln3_g, 'ln3_b': ln3_b}


def reference(x_prompt, x_sample, cache_fox_k, cache_fox_v, cache_fox_logf, state_ret, state_ssm, state_conv,
              cache_mem_k, cache_mem_v, mem_prompt, ln_in_g, ln_in_b, w_in, fox_fb, ret_gn_w, conv_w, conv_b,
              dt_bias, a_log, d_skip, ssm_norm_w, w_out, ln1_g, ln1_b, wq_mem, wkv_mem, wo_mem, ln2_g, ln2_b,
              peer_wq, peer_k1, peer_k2, peer_u, peer_v, ln3_g, ln3_b):
    Bp, S, _ = x_prompt.shape
    Ls = x_sample.shape[1]
    P = cache_fox_k.shape[2]
    pos_p = jnp.arange(S)
    pos_s = P + jnp.arange(Ls)
    n_mem = mem_prompt.shape[1]
    xp = layer_norm(x_prompt, ln_in_g, ln_in_b)
    xs = layer_norm(x_sample, ln_in_g, ln_in_b)
    st_p = [[] for _ in range(8)]
    st_s = [[] for _ in range(6)]
    for l in range(DEPTH):
        prm = (w_in[l], fox_fb[l], ret_gn_w[l], conv_w[l], conv_b[l], dt_bias[l], a_log[l], d_skip[l],
               ssm_norm_w[l], w_out[l], ln1_g[l], ln1_b[l], wq_mem[l], wo_mem[l], ln2_g[l], ln2_b[l],
               peer_wq[l], peer_k1[l], peer_k2[l], peer_u[l], peer_v[l], ln3_g[l], ln3_b[l])
        mkv = mem_prompt @ wkv_mem[l]
        mk = mkv[..., :W_M].reshape(Bp, n_mem, H_M, DH_M)
        mv = mkv[..., W_M:].reshape(Bp, n_mem, H_M, DH_M)
        xp, new_p = trunk_layer(xp, pos_p, prm, mk, mv, None)
        for j, a in enumerate(new_p + (mk, mv)):
            st_p[j].append(a)
        cache = (cache_fox_k[l], cache_fox_v[l], cache_fox_logf[l], state_ret[l], state_ssm[l], state_conv[l])
        xs, new_s = trunk_layer(xs, pos_s, prm, cache_mem_k[l], cache_mem_v[l], cache)
        for j, a in enumerate(new_s):
            st_s[j].append(a)
    fk_p, fv_p, fl_p, ret_p, ssm_p, conv_p, mk_p, mv_p = [jnp.stack(a) for a in st_p]
    fk_s, fv_s, fl_s, ret_s, ssm_s, conv_s = [jnp.stack(a) for a in st_s]
    return (xp, xs, fk_p, fv_p, fl_p, ret_p, ssm_p, conv_p, mk_p, mv_p, fk_s, fv_s, fl_s, ret_s, ssm_s, conv_s)
```

```python
import functools
import math

import numpy as np
import jax
import jax.numpy as jnp
from jax import lax
from jax.experimental import pallas as pl
from jax.experimental.pallas import tpu as pltpu

f32 = jnp.float32
bf16 = jnp.bfloat16
NEG_INF = float("-inf")

LN_EPS = 1e-5
DEPTH = 4
ALPHA = (2 * DEPTH) ** 0.25
H_A, DH_A = 8, 64
H_B, DK_B = 8, 64
H_C, P_C, N_C, G_C = 16, 64, 128, 2
W_A, W_B, W_C = 512, 512, 1024
CONV_W = 4
CONV_DIM = W_C + 2 * G_C * N_C
ROPE_BASE = 10000.0
H_M, DH_M = 4, 128
W_M = H_M * DH_M
PEER_NK, PEER_HEADS, PEER_DK, PEER_TOPK = 128, 8, 256, 16

LANES = 128
SUBLANES = 8
HALF = 64

XBC0, FQ0, Z0, FK0, FV0, RQ0, RK0, RV0, RG0, TAIL0, NH = 0, 1536, 2048, 3072, 3584, 4096, 4608, 5120, 5632, 6144, 6400
FF_LANE0, DT_LANE0 = 0, 8


def _cparams(n_axes, vmem_mb=None):
    kw = dict(dimension_semantics=("arbitrary",) * n_axes)
    if vmem_mb is not None:
        kw["vmem_limit_bytes"] = vmem_mb << 20
    return pltpu.CompilerParams(**kw)


def _pick(n, prefs):
    for p in prefs:
        if n % p == 0:
            return p
    return n


def _dot(a, b):
    return jnp.dot(a, b, preferred_element_type=f32)


def _dot_nt(a, b):
    return lax.dot_general(a, b, (((1,), (1,)), ((), ())), preferred_element_type=f32)


def _dot_tn(a, b):
    return lax.dot_general(a, b, (((0,), (0,)), ((), ())), preferred_element_type=f32)


def _ln(x, g, b):
    mu = jnp.mean(x, -1, keepdims=True)
    xc = x - mu
    var = jnp.mean(xc * xc, -1, keepdims=True)
    return xc * lax.rsqrt(var + LN_EPS) * g + b


def _silu(x):
    return x * jax.nn.sigmoid(x)


def _softplus(x):
    return jnp.maximum(x, 0.0) + jnp.log1p(jnp.exp(-jnp.abs(x)))


def _ln_kernel(x_ref, g_ref, b_ref, o_ref):
    o_ref[...] = _ln(x_ref[...], g_ref[...], b_ref[...])


def layer_norm_rows(x, g, b, tm):
    T, D = x.shape
    return pl.pallas_call(
        _ln_kernel, grid=(T // tm,),
        in_specs=[pl.BlockSpec((tm, D), lambda i: (i, 0)),
                  pl.BlockSpec((1, D), lambda i: (0, 0)),
                  pl.BlockSpec((1, D), lambda i: (0, 0))],
        out_specs=pl.BlockSpec((tm, D), lambda i: (i, 0)),
        out_shape=jax.ShapeDtypeStruct((T, D), f32),
        compiler_params=_cparams(1, 48), name="ln_in")(x, g.reshape(1, D), b.reshape(1, D))


def _ln_res_kernel(x_ref, r_ref, g_ref, b_ref, o_ref):
    o_ref[...] = _ln(ALPHA * x_ref[...] + r_ref[...], g_ref[...], b_ref[...])


def ln_residual(x, r, g, b, tm):
    T, D = x.shape
    return pl.pallas_call(
        _ln_res_kernel, grid=(T // tm,),
        in_specs=[pl.BlockSpec((tm, D), lambda i: (i, 0)),
                  pl.BlockSpec((tm, D), lambda i: (i, 0)),
                  pl.BlockSpec((1, D), lambda i: (0, 0)),
                  pl.BlockSpec((1, D), lambda i: (0, 0))],
        out_specs=pl.BlockSpec((tm, D), lambda i: (i, 0)),
        out_shape=jax.ShapeDtypeStruct((T, D), f32),
        compiler_params=_cparams(1, 48), name="ln_res")(x, r, g.reshape(1, D), b.reshape(1, D))


def _mm_kernel(x_ref, w_ref, o_ref, xb_ref):
    @pl.when(pl.program_id(1) == 0)
    def _():
        xb_ref[...] = x_ref[...].astype(bf16)

    o_ref[...] = _dot(xb_ref[...], w_ref[...]).astype(o_ref.dtype)


def matmul(x, w, tm, tn, name):
    T, K = x.shape
    N = w.shape[1]
    return pl.pallas_call(
        _mm_kernel, grid=(T // tm, N // tn),
        in_specs=[pl.BlockSpec((tm, K), lambda i, j: (i, 0)),
                  pl.BlockSpec((K, tn), lambda i, j: (0, j))],
        out_specs=pl.BlockSpec((tm, tn), lambda i, j: (i, j)),
        out_shape=jax.ShapeDtypeStruct((T, N), f32),
        scratch_shapes=[pltpu.VMEM((tm, K), bf16)],
        compiler_params=_cparams(2, 52), name=name)(x, w)


def _gate_kernel(t_ref, fb_ref, o_ref):
    x = t_ref[...] + fb_ref[...]
    o_ref[...] = jnp.minimum(x, 0.0) - jnp.log1p(jnp.exp(-jnp.abs(x)))


def forget_gate(h, fb_row, tm):
    T = h.shape[0]
    return pl.pallas_call(
        _gate_kernel, grid=(T // tm,),
        in_specs=[pl.BlockSpec((tm, LANES), lambda i: (i, TAIL0 // LANES)),
                  pl.BlockSpec((1, LANES), lambda i: (0, 0))],
        out_specs=pl.BlockSpec((tm, LANES), lambda i: (i, 0)),
        out_shape=jax.ShapeDtypeStruct((T, LANES), f32),
        compiler_params=_cparams(1), name="forget_gate")(h, fb_row)


def _cumsum_kernel(x_ref, o_ref, carry_ref):
    @pl.when(pl.program_id(1) == 0)
    def _():
        carry_ref[...] = jnp.zeros_like(carry_ref)

    bl = x_ref.shape[-1]
    r = lax.broadcasted_iota(jnp.int32, (bl, bl), 0)
    c = lax.broadcasted_iota(jnp.int32, (bl, bl), 1)
    upper = (r <= c).astype(f32)
    y = jnp.dot(x_ref[0], upper, precision=lax.Precision.HIGHEST,
                preferred_element_type=f32) + carry_ref[:, 0:1]
    o_ref[0] = y
    carry_ref[...] = jnp.broadcast_to(y[:, bl - 1:bl], carry_ref.shape)


def cumsum_lanes(x, bl):
    n, r, L = x.shape
    return pl.pallas_call(
        _cumsum_kernel, grid=(n, L // bl),
        in_specs=[pl.BlockSpec((1, r, bl), lambda s, j: (s, 0, j))],
        out_specs=pl.BlockSpec((1, r, bl), lambda s, j: (s, 0, j)),
        out_shape=jax.ShapeDtypeStruct((n, r, L), f32),
        scratch_shapes=[pltpu.VMEM((r, LANES), f32)],
        compiler_params=_cparams(2), name="cumsum")(x)


def _attn_step(q0, q1, kj, vj, ck, carry, mask, lane):
    m0, l0, m1, l1, acc = carry
    res = []
    for qh, m, l, row in ((q0, m0, l0, 0), (q1, m1, l1, 1)):
        s = _dot_nt(qh, kj) - ck[row:row + 1, :]
        if mask is not None:
            s = jnp.where(mask, s, NEG_INF)
        mn = jnp.maximum(m, jnp.max(s, -1, keepdims=True))
        a = jnp.exp(m - mn)
        p = jnp.exp(s - mn)
        ln = a * l + jnp.sum(p, -1, keepdims=True)
        pv = _dot(p.astype(bf16), vj)
        res.append((mn, ln, a, pv))
    (m0, l0, a0, pv0), (m1, l1, a1, pv1) = res
    acc = acc * jnp.where(lane, a0, a1) + jnp.where(lane, pv0, pv1)
    return m0, l0, m1, l1, acc


def _attn_init(tq):
    return (jnp.full((tq, 1), NEG_INF, f32), jnp.zeros((tq, 1), f32),
            jnp.full((tq, 1), NEG_INF, f32), jnp.zeros((tq, 1), f32),
            jnp.zeros((tq, LANES), f32))


def _split_heads(q, lane):
    zero = jnp.zeros_like(q)
    return jnp.where(lane, q, zero), jnp.where(lane, zero, q)


def _fox_prompt_kernel(q_ref, k_ref, v_ref, c_ref, o_ref, kb_ref, vb_ref, *, tq):
    i = pl.program_id(2)

    @pl.when(i == 0)
    def _():
        kb_ref[...] = k_ref[...].astype(bf16)
        vb_ref[...] = v_ref[...].astype(bf16)

    lane = lax.broadcasted_iota(jnp.int32, (tq, LANES), 1) < HALF
    q0, q1 = _split_heads((q_ref[...] * (DH_A ** -0.5)).astype(bf16), lane)

    def body(j, carry):
        off = pl.multiple_of(j * tq, tq)
        return _attn_step(q0, q1, kb_ref[pl.ds(off, tq), :], vb_ref[pl.ds(off, tq), :],
                          c_ref[0, 0, :, pl.ds(off, tq)], carry, None, lane)

    carry = lax.fori_loop(0, i, body, _attn_init(tq))
    off = pl.multiple_of(i * tq, tq)
    r = lax.broadcasted_iota(jnp.int32, (tq, tq), 0)
    c = lax.broadcasted_iota(jnp.int32, (tq, tq), 1)
    carry = _attn_step(q0, q1, kb_ref[pl.ds(off, tq), :], vb_ref[pl.ds(off, tq), :],
                       c_ref[0, 0, :, pl.ds(off, tq)], carry, c <= r, lane)
    _, l0, _, l1, acc = carry
    o_ref[...] = (acc / jnp.where(lane, l0, l1)).astype(o_ref.dtype)


def fox_prompt(h, cT, B, S, tq):
    nq = S // tq
    qb, kb, vb = FQ0 // LANES, FK0 // LANES, FV0 // LANES
    return pl.pallas_call(
        functools.partial(_fox_prompt_kernel, tq=tq), grid=(B, H_A // 2, nq),
        in_specs=[pl.BlockSpec((tq, LANES), lambda b, p, i: (b * nq + i, qb + p)),
                  pl.BlockSpec((S, LANES), lambda b, p, i: (b, kb + p)),
                  pl.BlockSpec((S, LANES), lambda b, p, i: (b, vb + p)),
                  pl.BlockSpec((1, 1, 2, S), lambda b, p, i: (b, p, 0, 0))],
        out_specs=pl.BlockSpec((tq, LANES), lambda b, p, i: (b * nq + i, p)),
        out_shape=jax.ShapeDtypeStruct((B * S, W_A), bf16),
        scratch_shapes=[pltpu.VMEM((S, LANES), bf16), pltpu.VMEM((S, LANES), bf16)],
        compiler_params=_cparams(3, 48), name="fox_prompt")(h, h, h, cT)


def _fox_sample_kernel(q_ref, kn_ref, vn_ref, kc_ref, vc_ref, c_ref, o_ref, *, Ls, P, tk):
    lane = lax.broadcasted_iota(jnp.int32, (Ls, LANES), 1) < HALF
    q0, q1 = _split_heads((q_ref[...] * (DH_A ** -0.5)).astype(bf16), lane)
    carry = _attn_init(Ls)
    for j in range(P // tk):
        carry = _attn_step(q0, q1, kc_ref[0, j * tk:(j + 1) * tk, :].astype(bf16),
                           vc_ref[0, j * tk:(j + 1) * tk, :].astype(bf16),
                           c_ref[0, 0, :, j * tk:(j + 1) * tk], carry, None, lane)
    r = lax.broadcasted_iota(jnp.int32, (Ls, Ls), 0)
    c = lax.broadcasted_iota(jnp.int32, (Ls, Ls), 1)
    carry = _attn_step(q0, q1, kn_ref[...].astype(bf16), vn_ref[...].astype(bf16),
                       c_ref[0, 0, :, P:P + Ls], carry, c <= r, lane)
    _, l0, _, l1, acc = carry
    o_ref[...] = (acc / jnp.where(lane, l0, l1)).astype(o_ref.dtype)


def fox_sample(h, ck, cv, cT, row0, Bs, Ls, P):
    qb, kb, vb = FQ0 // LANES, FK0 // LANES, FV0 // LANES
    tk = _pick(P, (1024, 512, 256, 128))
    Lp = cT.shape[-1]
    return pl.pallas_call(
        functools.partial(_fox_sample_kernel, Ls=Ls, P=P, tk=tk), grid=(Bs, H_A // 2),
        in_specs=[pl.BlockSpec((Ls, LANES), lambda b, p: (row0 + b, qb + p)),
                  pl.BlockSpec((Ls, LANES), lambda b, p: (row0 + b, kb + p)),
                  pl.BlockSpec((Ls, LANES), lambda b, p: (row0 + b, vb + p)),
                  pl.BlockSpec((1, P, LANES), lambda b, p: (b, 0, p)),
                  pl.BlockSpec((1, P, LANES), lambda b, p: (b, 0, p)),
                  pl.BlockSpec((1, 1, 2, Lp), lambda b, p: (b, p, 0, 0))],
        out_specs=pl.BlockSpec((Ls, LANES), lambda b, p: (b, p)),
        out_shape=jax.ShapeDtypeStruct((Bs * Ls, W_A), bf16),
        compiler_params=_cparams(2, 48), name="fox_sample")(h, h, h, ck, cv, cT)


def _pair_mean(x, lane):
    s0 = jnp.sum(jnp.where(lane, x, 0.0), -1, keepdims=True)
    s1 = jnp.sum(jnp.where(lane, 0.0, x), -1, keepdims=True)
    return jnp.where(lane, s0, s1) * (1.0 / HALF)


def _rotary(x, cos, sin_signed, first_half):
    xr = jnp.where(first_half, pltpu.roll(x, LANES - HALF // 2, 1), pltpu.roll(x, HALF // 2, 1))
    return x * cos + xr * sin_signed


def _retention_kernel(q_ref, k_ref, v_ref, g_ref, cos_ref, sin_ref, dec_ref, gq_ref, gk_ref, gl_ref,
                      s0_ref, gn_ref, o_ref, s1_ref, st_ref, *, Lc):
    c = pl.program_id(2)

    @pl.when(c == 0)
    def _():
        st_ref[...] = s0_ref[0, 0]

    lane_i = lax.broadcasted_iota(jnp.int32, (Lc, LANES), 1)
    lane = lane_i < HALF
    first_half = (lane_i % HALF) < (HALF // 2)
    cos, sin = cos_ref[...], sin_ref[...]
    q = _rotary(q_ref[...], cos, sin, first_half)
    k = _rotary(k_ref[...], cos, sin, first_half) * (DK_B ** -0.5)
    qb, kb, vb = q.astype(bf16), k.astype(bf16), v_ref[...].astype(bf16)
    q0, q1 = _split_heads(qb, lane)
    a0 = (_dot_nt(q0, kb) * dec_ref[0]).astype(bf16)
    a1 = (_dot_nt(q1, kb) * dec_ref[1]).astype(bf16)
    intra = jnp.where(lane, _dot(a0, vb), _dot(a1, vb))
    st = st_ref[...]
    cross = _dot(qb, st.astype(bf16)) * gq_ref[0]
    o = intra + cross
    kd = (k * gk_ref[0]).astype(bf16)
    sr = lax.broadcasted_iota(jnp.int32, (LANES, LANES), 0) < HALF
    sc = lax.broadcasted_iota(jnp.int32, (LANES, LANES), 1) < HALF
    st_new = gl_ref[0, 0:1, :] * st + jnp.where(sr == sc, _dot_tn(kd, vb), 0.0)
    st_ref[...] = st_new
    mu = _pair_mean(o, lane)
    d = o - mu
    var = _pair_mean(d * d, lane)
    on = d * lax.rsqrt(var + LN_EPS) * gn_ref[...]
    o_ref[...] = (_silu(g_ref[...]) * on).astype(o_ref.dtype)

    @pl.when(c == pl.num_programs(2) - 1)
    def _():
        s1_ref[0, 0] = st_new


def _retention_tables(pos, Lc):
    half = DK_B // 2
    inv = ROPE_BASE ** (-jnp.arange(half, dtype=f32) / half)
    ang = pos.astype(f32)[:, None] * inv[None, :]
    cos, sin = jnp.cos(ang), jnp.sin(ang)
    cos_t = jnp.tile(cos, (1, 4))
    sin_t = jnp.tile(jnp.concatenate([-sin, sin], -1), (1, 2))
    lg = jnp.log1p(-jnp.exp2(-5.0 - jnp.arange(H_B, dtype=f32)))
    i = jnp.arange(Lc, dtype=f32)
    diff = i[:, None] - i[None, :]
    dec = jnp.exp(jnp.where((diff >= 0)[None], diff[None] * lg[:, None, None], NEG_INF))
    pair = lambda t: jnp.repeat(t.reshape(t.shape[0], H_B // 2, 2), HALF, axis=-1)
    gq = pair(jnp.exp((i[:, None] + 1.0) * lg[None, :])).transpose(1, 0, 2)
    gk = pair(jnp.exp((Lc - 1.0 - i)[:, None] * lg[None, :])).transpose(1, 0, 2)
    gl = jnp.broadcast_to(pair(jnp.exp(Lc * lg)[None, :]).transpose(1, 0, 2), (H_B // 2, SUBLANES, LANES))
    return cos_t, sin_t, dec, gq, gk, gl


def retention(h, gn_w, s0, pos, row0, nseq, L, Lc):
    nch = L // Lc
    cos_t, sin_t, dec, gq, gk, gl = _retention_tables(pos, Lc)
    blk = lambda col0: pl.BlockSpec((Lc, LANES), lambda s, p, c: (row0 + s * nch + c, col0 // LANES + p))
    tab = lambda: pl.BlockSpec((1, Lc, LANES), lambda s, p, c: (p, 0, 0))
    return pl.pallas_call(
        functools.partial(_retention_kernel, Lc=Lc), grid=(nseq, H_B // 2, nch),
        in_specs=[blk(RQ0), blk(RK0), blk(RV0), blk(RG0),
                  pl.BlockSpec((Lc, LANES), lambda s, p, c: (c, 0)),
                  pl.BlockSpec((Lc, LANES), lambda s, p, c: (c, 0)),
                  pl.BlockSpec((2, Lc, Lc), lambda s, p, c: (p, 0, 0)),
                  tab(), tab(),
                  pl.BlockSpec((1, SUBLANES, LANES), lambda s, p, c: (p, 0, 0)),
                  pl.BlockSpec((1, 1, LANES, LANES), lambda s, p, c: (s, p, 0, 0)),
                  pl.BlockSpec((1, LANES), lambda s, p, c: (0, p))],
        out_specs=[pl.BlockSpec((Lc, LANES), lambda s, p, c: (s * nch + c, p)),
                   pl.BlockSpec((1, 1, LANES, LANES), lambda s, p, c: (s, p, 0, 0))],
        out_shape=[jax.ShapeDtypeStruct((nseq * L, W_B), bf16),
                   jax.ShapeDtypeStruct((nseq, H_B // 2, LANES, LANES), f32)],
        scratch_shapes=[pltpu.VMEM((LANES, LANES), f32)],
        compiler_params=_cparams(3, 32), name="retention")(
            h, h, h, h, cos_t, sin_t, dec, gq, gk, gl, s0, gn_w.reshape(1, W_B))


def _pack_ret_state(s):
    n = s.shape[0]
    s = s.reshape(n, H_B // 2, 2, DK_B, DK_B)
    z = jnp.zeros_like(s[:, :, 0])
    top = jnp.concatenate([s[:, :, 0], z], -1)
    bot = jnp.concatenate([z, s[:, :, 1]], -1)
    return jnp.concatenate([top, bot], -2)


def _unpack_ret_state(s):
    n = s.shape[0]
    return jnp.stack([s[:, :, :HALF, :HALF], s[:, :, HALF:, HALF:]], 2).reshape(n, H_B, DK_B, DK_B)


def _ssd_kernel(xbc_ref, z_ref, t_ref, cw_ref, cb_ref, dtb_ref, alog_ref, dsk_ref, nw_ref, hist_ref, h0_ref,
                o_ref, h1_ref, xpad_ref, hs_ref, *, Lc):
    c = pl.program_id(1)

    @pl.when(c == 0)
    def _():
        xpad_ref[0:SUBLANES, :] = hist_ref[0]
        hs_ref[...] = h0_ref[0]

    xpad_ref[SUBLANES:SUBLANES + Lc, :] = xbc_ref[...]
    conv = cb_ref[...]
    for t in range(CONV_W):
        r0 = SUBLANES - (CONV_W - 1) + t
        conv = conv + xpad_ref[r0:r0 + Lc, :] * cw_ref[t:t + 1, :]
    xpad_ref[0:SUBLANES, :] = xpad_ref[Lc:Lc + SUBLANES, :]
    xc = _silu(conv)
    xs = xc[:, :W_C]
    bm = [xc[:, W_C + g * N_C:W_C + (g + 1) * N_C].astype(bf16) for g in range(G_C)]
    cm = [xc[:, W_C + (G_C + g) * N_C:W_C + (G_C + g + 1) * N_C].astype(bf16) for g in range(G_C)]

    lane_i = lax.broadcasted_iota(jnp.int32, (Lc, LANES), 1)
    lane = lane_i < HALF
    dt_valid = (lane_i >= DT_LANE0) & (lane_i < DT_LANE0 + H_C)
    dt = _softplus(t_ref[...] + dtb_ref[...])
    dta = jnp.where(dt_valid, dt * (-jnp.exp(alog_ref[...])), 0.0)
    r = lax.broadcasted_iota(jnp.int32, (Lc, Lc), 0)
    cidx = lax.broadcasted_iota(jnp.int32, (Lc, Lc), 1)
    tri = cidx <= r
    a_cs = jnp.dot(tri.astype(f32), dta, precision=lax.Precision.HIGHEST, preferred_element_type=f32)
    a_cs_t = a_cs.T
    dt_t = dt.T
    row_first = lax.broadcasted_iota(jnp.int32, (LANES, LANES), 0) < HALF

    ys = []
    for p in range(H_C // 2):
        g = (2 * p) // (H_C // G_C)
        xpair = xs[:, p * LANES:(p + 1) * LANES]
        xpair_b = xpair.astype(bf16)
        if p % (H_C // G_C // 2) == 0:
            cb = _dot_nt(cm[g], bm[g])
        yh, acol, dcol = [], [], []
        for hd in (2 * p, 2 * p + 1):
            li = DT_LANE0 + hd
            ac = a_cs[:, li:li + 1]
            seg = ac - a_cs_t[li:li + 1, :]
            w = cb * jnp.exp(jnp.where(tri, seg, NEG_INF)) * dt_t[li:li + 1, :]
            yh.append(_dot(w.astype(bf16), xpair_b))
            acol.append(ac)
            dcol.append(dt[:, li:li + 1])
        acs_pair = jnp.where(lane, acol[0], acol[1])
        dt_pair = jnp.where(lane, dcol[0], dcol[1])
        hs = hs_ref[p]
        y = jnp.where(lane, yh[0], yh[1]) + _dot_nt(cm[g], hs.astype(bf16)) * jnp.exp(acs_pair)
        ys.append(y)
        a_last = acs_pair[Lc - 1:Lc, :]
        to_end = jnp.exp(a_last - acs_pair) * dt_pair
        upd = _dot_tn((xpair * to_end).astype(bf16), bm[g])
        sdec = jnp.exp(jnp.where(row_first, acol[0][Lc - 1:Lc, :], acol[1][Lc - 1:Lc, :]))
        hs_ref[p] = sdec * hs + upd

    y = jnp.concatenate(ys, axis=1) + dsk_ref[...] * xs
    y = y * _silu(z_ref[...])
    y = y * lax.rsqrt(jnp.mean(y * y, -1, keepdims=True) + LN_EPS) * nw_ref[...]
    o_ref[...] = y.astype(o_ref.dtype)

    @pl.when(c == pl.num_programs(1) - 1)
    def _():
        h1_ref[0] = hs_ref[...]


def ssd(h, conv_w, conv_b, dt_bias, a_log, d_skip, norm_w, hist, h0, row0, nseq, L, Lc):
    nch = L // Lc
    lane_row = lambda v: jnp.zeros((1, LANES), f32).at[0, DT_LANE0:DT_LANE0 + H_C].set(v)
    cw = jnp.zeros((SUBLANES, CONV_DIM), f32).at[:CONV_W].set(conv_w)
    row = lambda w: pl.BlockSpec((1, w), lambda s, c: (0, 0))
    return pl.pallas_call(
        functools.partial(_ssd_kernel, Lc=Lc), grid=(nseq, nch),
        in_specs=[pl.BlockSpec((Lc, CONV_DIM), lambda s, c: (row0 + s * nch + c, XBC0 // CONV_DIM)),
                  pl.BlockSpec((Lc, W_C), lambda s, c: (row0 + s * nch + c, Z0 // W_C)),
                  pl.BlockSpec((Lc, LANES), lambda s, c: (row0 + s * nch + c, TAIL0 // LANES)),
                  pl.BlockSpec((SUBLANES, CONV_DIM), lambda s, c: (0, 0)),
                  row(CONV_DIM), row(LANES), row(LANES), row(W_C), row(W_C),
                  pl.BlockSpec((1, SUBLANES, CONV_DIM), lambda s, c: (s, 0, 0)),
                  pl.BlockSpec((1, H_C // 2, LANES, LANES), lambda s, c: (s, 0, 0, 0))],
        out_specs=[pl.BlockSpec((Lc, W_C), lambda s, c: (s * nch + c, 0)),
                   pl.BlockSpec((1, H_C // 2, LANES, LANES), lambda s, c: (s, 0, 0, 0))],
        out_shape=[jax.ShapeDtypeStruct((nseq * L, W_C), bf16),
                   jax.ShapeDtypeStruct((nseq, H_C // 2, LANES, LANES), f32)],
        scratch_shapes=[pltpu.VMEM((Lc + SUBLANES, CONV_DIM), f32),
                        pltpu.VMEM((H_C // 2, LANES, LANES), f32)],
        compiler_params=_cparams(2, 48), name="ssd")(
            h, h, h, cw, conv_b.reshape(1, CONV_DIM), lane_row(dt_bias), lane_row(a_log),
            jnp.repeat(d_skip, P_C).reshape(1, W_C), norm_w.reshape(1, W_C), hist, h0)


def _outproj_kernel(x_ref, a_ref, b_ref, c_ref, w_ref, g_ref, beta_ref, o_ref):
    mix = (_dot(a_ref[...], w_ref[0:W_A, :]) + _dot(b_ref[...], w_ref[W_A:W_A + W_B, :])
           + _dot(c_ref[...], w_ref[W_A + W_B:, :]))
    o_ref[...] = _ln(ALPHA * x_ref[...] + mix, g_ref[...], beta_ref[...])


def outproj_ln(x, ao, bo, co, w, g, b, tm):
    T, D = x.shape
    rows = lambda wd: pl.BlockSpec((tm, wd), lambda i: (i, 0))
    const = lambda s: pl.BlockSpec(s, lambda i: (0, 0))
    return pl.pallas_call(
        _outproj_kernel, grid=(T // tm,),
        in_specs=[rows(D), rows(W_A), rows(W_B), rows(W_C), const(w.shape), const((1, D)), const((1, D))],
        out_specs=rows(D),
        out_shape=jax.ShapeDtypeStruct((T, D), f32),
        compiler_params=_cparams(1, 52), name="outproj_ln")(x, ao, bo, co, w, g.reshape(1, D), b.reshape(1, D))


def _mem_kernel(x_ref, wq_ref, mk_ref, mv_ref, wo_ref, g_ref, b_ref, o_ref):
    x = x_ref[...]
    q = _dot(x.astype(bf16), wq_ref[...])
    outs = []
    for hd in range(H_M):
        sl = slice(hd * DH_M, (hd + 1) * DH_M)
        s = _dot_nt(q[:, sl].astype(bf16), mk_ref[0, :, sl].astype(bf16)) * (DH_M ** -0.5)
        p = jnp.exp(s - jnp.max(s, -1, keepdims=True))
        p = p / jnp.sum(p, -1, keepdims=True)
        outs.append(_dot(p.astype(bf16), mv_ref[0, :, sl].astype(bf16)))
    o = jnp.concatenate(outs, axis=1).astype(bf16)
    o_ref[...] = _ln(ALPHA * x + _dot(o, wo_ref[...]), g_ref[...], b_ref[...])


def mem_attn_ln(x, wq, mk, mv, wo, g, b, row0, nseq, L, tr):
    D = x.shape[1]
    per = L // tr
    nm = mk.shape[1]
    const = lambda s: pl.BlockSpec(s, lambda i: (0,) * len(s))
    return pl.pallas_call(
        _mem_kernel, grid=(nseq * per,),
        in_specs=[pl.BlockSpec((tr, D), lambda i: (row0 + i, 0)), const(wq.shape),
                  pl.BlockSpec((1, nm, W_M), lambda i: (i // per, 0, 0)),
                  pl.BlockSpec((1, nm, W_M), lambda i: (i // per, 0, 0)),
                  const(wo.shape), const((1, D)), const((1, D))],
        out_specs=pl.BlockSpec((tr, D), lambda i: (i, 0)),
        out_shape=jax.ShapeDtypeStruct((nseq * L, D), f32),
        compiler_params=_cparams(1, 48), name="mem_attn_ln")(x, wq, mk, mv, wo, g.reshape(1, D), b.reshape(1, D))


def _peer_score_kernel(x_ref, wq_ref, k1_ref, k2_ref, o_ref):
    q = _dot(x_ref[...].astype(bf16), wq_ref[...])
    k1 = k1_ref[...].astype(bf16)
    k2 = k2_ref[...].astype(bf16)
    hk = PEER_DK // 2
    for hd in range(PEER_HEADS):
        q1 = q[:, hd * PEER_DK:hd * PEER_DK + hk].astype(bf16)
        q2 = q[:, hd * PEER_DK + hk:(hd + 1) * PEER_DK].astype(bf16)
        o_ref[hd, 0:PEER_NK, :] = _dot_nt(k1, q1)
        o_ref[hd, PEER_NK:2 * PEER_NK, :] = _dot_nt(k2, q2)


def peer_scores(x, wq, k1, k2, tm):
    T, D = x.shape
    const = lambda s: pl.BlockSpec(s, lambda i: (0, 0))
    return pl.pallas_call(
        _peer_score_kernel, grid=(T // tm,),
        in_specs=[pl.BlockSpec((tm, D), lambda i: (i, 0)), const(wq.shape), const(k1.shape), const(k2.shape)],
        out_specs=pl.BlockSpec((PEER_HEADS, 2 * PEER_NK, tm), lambda i: (0, 0, i)),
        out_shape=jax.ShapeDtypeStruct((PEER_HEADS, 2 * PEER_NK, T), f32),
        compiler_params=_cparams(1, 52), name="peer_scores")(x, wq, k1, k2)


_CAND_ROWS = PEER_TOPK + 7 * SUBLANES + SUBLANES


def _cand_flat_index():
    idx = [0 * PEER_TOPK + b for b in range(PEER_TOPK)]
    for a in range(1, 8):
        idx += [a * PEER_TOPK + b for b in range(SUBLANES)]
    idx += [a * PEER_TOPK for a in range(8, PEER_TOPK)]
    return np.broadcast_to(np.asarray(idx, np.float32)[:, None], (_CAND_ROWS, LANES)).copy()


def _extract_top(s, key_idx, n):
    rank = jnp.full(s.shape, float(n), f32)
    vals = []
    for a in range(n):
        mx = jnp.max(s, axis=0, keepdims=True)
        first = jnp.min(jnp.where(s == mx, key_idx, float(1 << 20)), axis=0, keepdims=True)
        sel = key_idx == first
        rank = jnp.where(sel, float(a), rank)
        s = jnp.where(sel, NEG_INF, s)
        vals.append(mx)
    return vals, rank


def _peer_topk_kernel(s_ref, cidx_ref, r2_ref, g2_ref, cnt_ref, g1_ref, *, tb):
    key_idx = lax.broadcasted_iota(jnp.int32, (PEER_NK, LANES), 0).astype(f32)
    cidx = cidx_ref[...]

    def chunk(ci, _):
        off = pl.multiple_of(ci * LANES, LANES)
        s1 = s_ref[0, 0:PEER_NK, pl.ds(off, LANES)]
        s2 = s_ref[0, PEER_NK:2 * PEER_NK, pl.ds(off, LANES)]
        v1, rank1 = _extract_top(s1, key_idx, PEER_TOPK)
        v2, rank2 = _extract_top(s2, key_idx, PEER_TOPK)
        v1a = jnp.concatenate(v1, axis=0)
        v2a = jnp.concatenate(v2, axis=0)
        cand = jnp.concatenate([v1[0] + v2a] + [v1[a] + v2a[0:SUBLANES] for a in range(1, 8)]
                               + [v1a[SUBLANES:] + v2[0]], axis=0)
        top = v1[0] + v2[0]
        picked = jnp.zeros(cand.shape, f32)
        zsum = jnp.zeros((1, LANES), f32)
        for _k in range(PEER_TOPK):
            mx = jnp.max(cand, axis=0, keepdims=True)
            first = jnp.min(jnp.where(cand == mx, cidx, float(1 << 20)), axis=0, keepdims=True)
            sel = cidx == first
            picked = jnp.where(sel, 1.0, picked)
            cand = jnp.where(sel, NEG_INF, cand)
            zsum = zsum + jnp.exp(mx - top)
        cnt_a = [jnp.sum(picked[0:PEER_TOPK], axis=0, keepdims=True)]
        for a in range(1, 8):
            lo = PEER_TOPK + (a - 1) * SUBLANES
            cnt_a.append(jnp.sum(picked[lo:lo + SUBLANES], axis=0, keepdims=True))
        tail = picked[PEER_TOPK + 7 * SUBLANES:]
        cnt_a += [tail[a:a + 1] for a in range(SUBLANES)]
        cnt = jnp.zeros((PEER_NK, LANES), f32)
        for a in range(PEER_TOPK):
            cnt = jnp.where(rank1 == float(a), cnt_a[a], cnt)
        r2_ref[0, :, pl.ds(off, LANES)] = rank2
        g2_ref[0, :, pl.ds(off, LANES)] = jnp.exp(s2 - v2[0])
        cnt_ref[:, 0, :, pl.ds(off, LANES)] = cnt.reshape(PEER_NK // SUBLANES, SUBLANES, LANES)
        g1_ref[:, 0, :, pl.ds(off, LANES)] = (jnp.exp(s1 - v1[0]) / zsum).reshape(
            PEER_NK // SUBLANES, SUBLANES, LANES)
        return 0

    lax.fori_loop(0, tb // LANES, chunk, 0)


def peer_topk(sT, tb):
    T = sT.shape[-1]
    nj = PEER_NK // SUBLANES
    per_key = lambda: pl.BlockSpec((1, PEER_NK, tb), lambda hd, i: (hd, 0, i))
    per_row = lambda: pl.BlockSpec((nj, 1, SUBLANES, tb), lambda hd, i: (0, hd, 0, i))
    r2, g2, cnt, g1 = pl.pallas_call(
        functools.partial(_peer_topk_kernel, tb=tb), grid=(PEER_HEADS, T // tb),
        in_specs=[pl.BlockSpec((1, 2 * PEER_NK, tb), lambda hd, i: (hd, 0, i)),
                  pl.BlockSpec((_CAND_ROWS, LANES), lambda hd, i: (0, 0))],
        out_specs=[per_key(), per_key(), per_row(), per_row()],
        out_shape=[jax.ShapeDtypeStruct((PEER_HEADS, PEER_NK, T), f32),
                   jax.ShapeDtypeStruct((PEER_HEADS, PEER_NK, T), f32),
                   jax.ShapeDtypeStruct((nj, PEER_HEADS, SUBLANES, T), f32),
                   jax.ShapeDtypeStruct((nj, PEER_HEADS, SUBLANES, T), f32)],
        compiler_params=_cparams(2, 32), name="peer_topk")(sT, jnp.asarray(_cand_flat_index()))
    return r2, g2, cnt.reshape(nj, PEER_HEADS * SUBLANES, T), g1.reshape(nj, PEER_HEADS * SUBLANES, T)


def _peer_main_kernel(x_ref, u_ref, vt_ref, r2_ref, g2_ref, cnt_ref, g1_ref, o_ref, acc_ref):
    j = pl.program_id(1)

    @pl.when(j == 0)
    def _():
        acc_ref[...] = jnp.zeros_like(acc_ref)

    a = _dot_nt(u_ref[...], x_ref[...])
    act = 0.5 * a * (1.0 + lax.erf(a * (2.0 ** -0.5)))
    parts = []
    for r in range(SUBLANES):
        w = None
        for hd in range(PEER_HEADS):
            row = hd * SUBLANES + r
            sel = jnp.where(r2_ref[hd] < cnt_ref[0, row:row + 1, :], g2_ref[hd], 0.0) * g1_ref[0, row:row + 1, :]
            w = sel if w is None else w + sel
        parts.append((w * act[r * PEER_NK:(r + 1) * PEER_NK]).astype(bf16))
    acc_ref[...] += _dot(vt_ref[...], jnp.concatenate(parts, axis=0))

    @pl.when(j == pl.num_programs(1) - 1)
    def _():
        o_ref[...] = acc_ref[...].T


def peer_main(xb, ub, vtb, r2, g2, cnt, g1, tm):
    T, D = xb.shape
    te = SUBLANES * PEER_NK
    nj = PEER_NK // SUBLANES
    per_tok = lambda: pl.BlockSpec((PEER_HEADS, PEER_NK, tm), lambda i, j: (0, 0, i))
    per_row = lambda: pl.BlockSpec((1, PEER_HEADS * SUBLANES, tm), lambda i, j: (j, 0, i))
    return pl.pallas_call(
        _peer_main_kernel, grid=(T // tm, nj),
        in_specs=[pl.BlockSpec((tm, D), lambda i, j: (i, 0)),
                  pl.BlockSpec((te, D), lambda i, j: (j, 0)),
                  pl.BlockSpec((D, te), lambda i, j: (0, j)),
                  per_tok(), per_tok(), per_row(), per_row()],
        out_specs=pl.BlockSpec((tm, D), lambda i, j: (i, 0)),
        out_shape=jax.ShapeDtypeStruct((T, D), f32),
        scratch_shapes=[pltpu.VMEM((D, tm), f32)],
        compiler_params=_cparams(2, 56), name="peer_main")(xb, ub, vtb, r2, g2, cnt, g1)


def _rearrange_w_in(w):
    o = np.cumsum([0, W_A, W_A, W_A, H_A, H_B * DK_B, H_B * DK_B, W_B, W_B, W_C, CONV_DIM, H_C])
    fq, fk, fv, ff, rq, rk, rv, rg, z, xbc, dtr = [slice(int(o[i]), int(o[i + 1])) for i in range(11)]
    parts = [w[:, s] for s in (xbc, fq, z, fk, fv, rq, rk, rv, rg, ff, dtr)]
    pad = NH - TAIL0 - H_A - H_C
    return jnp.concatenate(parts + [jnp.zeros((w.shape[0], pad), w.dtype)], axis=1).astype(bf16)


def kernel(x_prompt, x_sample, cache_fox_k, cache_fox_v, cache_fox_logf, state_ret, state_ssm, state_conv, cache_mem_k, cache_mem_v, mem_prompt, ln_in_g, ln_in_b, w_in, fox_fb, ret_gn_w, conv_w, conv_b, dt_bias, a_log, d_skip, ssm_norm_w, w_out, ln1_g, ln1_b, wq_mem, wkv_mem, wo_mem, ln2_g, ln2_b, peer_wq, peer_k1, peer_k2, peer_u, peer_v, ln3_g, ln3_b):
    B, S, D = x_prompt.shape
    Bs, Ls, _ = x_sample.shape
    P = cache_fox_k.shape[2]
    NM = mem_prompt.shape[1]
    Tp, Ts = B * S, Bs * Ls
    T = Tp + Ts
    assert Tp % Ls == 0 and T % LANES == 0 and Ls >= CONV_W - 1

    tm = _pick(T, (640, 768, 384, 256, 128))
    tm_peer = _pick(T, (640, 768, 384, 256, 128))
    tq = _pick(S, (512, 256, 128))
    lc = _pick(S, (256, 128, 64))
    tr_mem = _pick(S, (512, 256, 128))
    cs_bl = 512
    tn_in = _pick(NH, (1280, 640))

    x = jnp.concatenate([x_prompt.reshape(Tp, D), x_sample.reshape(Ts, D)], axis=0)
    x = layer_norm_rows(x, ln_in_g, ln_in_b, tm)

    pos_p = jnp.arange(S)
    pos_s = P + jnp.arange(Ls)
    zeros_ret = jnp.zeros((B, H_B // 2, LANES, LANES), f32)
    zeros_ssm = jnp.zeros((B, H_C // 2, LANES, LANES), f32)
    zeros_hist = jnp.zeros((B, SUBLANES, CONV_DIM), f32)
    lp_s = -(-(P + Ls) // cs_bl) * cs_bl

    st_p = [[] for _ in range(8)]
    st_s = [[] for _ in range(6)]
    for l in range(DEPTH):
        h = matmul(x, _rearrange_w_in(w_in[l]), tm, tn_in, "in_proj")
        fb_row = jnp.zeros((1, LANES), f32).at[0, FF_LANE0:FF_LANE0 + H_A].set(fox_fb[l])
        logf = forget_gate(h, fb_row, tm)[:, FF_LANE0:FF_LANE0 + H_A]
        logf_p = logf[:Tp].reshape(B, S, H_A)
        logf_s = logf[Tp:].reshape(Bs, Ls, H_A)

        c_p = cumsum_lanes(logf_p.transpose(0, 2, 1), min(cs_bl, S)).reshape(B, H_A // 2, 2, S)
        ao_p = fox_prompt(h, c_p, B, S, tq)
        lf_all = jnp.concatenate([cache_fox_logf[l].astype(f32), logf_s], axis=1).transpose(0, 2, 1)
        lf_all = jnp.pad(lf_all, ((0, 0), (0, 0), (0, lp_s - (P + Ls))))
        c_s = cumsum_lanes(lf_all, cs_bl).reshape(Bs, H_A // 2, 2, lp_s)
        ao_s = fox_sample(h, cache_fox_k[l].reshape(Bs, P, W_A), cache_fox_v[l].reshape(Bs, P, W_A), c_s,
                          Tp // Ls, Bs, Ls, P)

        bo_p, ret_p = retention(h, ret_gn_w[l], zeros_ret, pos_p, 0, B, S, lc)
        bo_s, ret_s = retention(h, ret_gn_w[l], _pack_ret_state(state_ret[l].astype(f32)), pos_s,
                                Tp // Ls, Bs, Ls, Ls)

        ssd_prm = (conv_w[l], conv_b[l], dt_bias[l], a_log[l], d_skip[l], ssm_norm_w[l])
        co_p, ssm_p = ssd(h, *ssd_prm, zeros_hist, zeros_ssm, 0, B, S, lc)
        hist_s = jnp.pad(state_conv[l].astype(f32), ((0, 0), (SUBLANES - (CONV_W - 1), 0), (0, 0)))
        co_s, ssm_s = ssd(h, *ssd_prm, hist_s, state_ssm[l].astype(f32).reshape(Bs, H_C // 2, LANES, LANES),
                          Tp // Ls, Bs, Ls, Ls)

        cat = lambda a, b: jnp.concatenate([a, b], axis=0)
        x = outproj_ln(x, cat(ao_p, ao_s), cat(bo_p, bo_s), cat(co_p, co_s), w_out[l].astype(bf16),
                       ln1_g[l], ln1_b[l], tm)

        mkv = matmul(mem_prompt.reshape(B * NM, D), wkv_mem[l].astype(bf16), _pick(B * NM, (512, 256, 128)),
                     _pick(2 * W_M, (512, 256)), "mem_kv")
        mk_p = mkv[:, :W_M].reshape(B, NM, W_M)
        mv_p = mkv[:, W_M:].reshape(B, NM, W_M)
        wq_b, wo_b = wq_mem[l].astype(bf16), wo_mem[l].astype(bf16)
        x_p = mem_attn_ln(x, wq_b, mk_p, mv_p, wo_b, ln2_g[l], ln2_b[l], 0, B, S, tr_mem)
        x_s = mem_attn_ln(x, wq_b, cache_mem_k[l].reshape(Bs, NM, W_M), cache_mem_v[l].reshape(Bs, NM, W_M),
                          wo_b, ln2_g[l], ln2_b[l], Tp // Ls, Bs, Ls, Ls)
        x = cat(x_p, x_s)

        sT = peer_scores(x, peer_wq[l].astype(bf16), peer_k1[l], peer_k2[l], tm)
        r2, g2, cnt, g1 = peer_topk(sT, tm_peer)
        pe = peer_main(x.astype(bf16), peer_u[l].astype(bf16), peer_v[l].T.astype(bf16), r2, g2, cnt, g1, tm_peer)
        x = ln_residual(x, pe, ln3_g[l], ln3_b[l], tm)

        hp = h[:Tp].reshape(B, S, NH)
        hs = h[Tp:].reshape(Bs, Ls, NH)
        new_p = (hp[..., FK0:FK0 + W_A].reshape(B, S, H_A, DH_A), hp[..., FV0:FV0 + W_A].reshape(B, S, H_A, DH_A),
                 logf_p, _unpack_ret_state(ret_p), ssm_p.reshape(B, H_C, P_C, N_C),
                 hp[:, S - (CONV_W - 1):, XBC0:XBC0 + CONV_DIM],
                 mk_p.reshape(B, NM, H_M, DH_M), mv_p.reshape(B, NM, H_M, DH_M))
        new_s = (hs[..., FK0:FK0 + W_A].reshape(Bs, Ls, H_A, DH_A), hs[..., FV0:FV0 + W_A].reshape(Bs, Ls, H_A, DH_A),
                 logf_s, _unpack_ret_state(ret_s), ssm_s.reshape(Bs, H_C, P_C, N_C),
                 hs[:, Ls - (CONV_W - 1):, XBC0:XBC0 + CONV_DIM])
        for j, a in enumerate(new_p):
            st_p[j].append(a)
        for j, a in enumerate(new_s):
            st_s[j].append(a)

    outs_p = [jnp.stack(a) for a in st_p]
    outs_s = [jnp.stack(a) for a in st_s]
    return (x[:Tp].reshape(B, S, D), x[Tp:].reshape(Bs, Ls, D), *outs_p, *outs_s)
```

```python
import functools
import math

import numpy as np
import jax
import jax.numpy as jnp
from jax import lax
from jax.experimental import pallas as pl
from jax.experimental.pallas import tpu as pltpu

f32 = jnp.float32
bf16 = jnp.bfloat16
NEG_INF = float("-inf")

LN_EPS = 1e-5
DEPTH = 4
ALPHA = (2 * DEPTH) ** 0.25
H_A, DH_A = 8, 64
H_B, DK_B = 8, 64
H_C, P_C, N_C, G_C = 16, 64, 128, 2
W_A, W_B, W_C = 512, 512, 1024
CONV_W = 4
CONV_DIM = W_C + 2 * G_C * N_C
ROPE_BASE = 10000.0
H_M, DH_M = 4, 128
W_M = H_M * DH_M
PEER_NK, PEER_HEADS, PEER_DK, PEER_TOPK = 128, 8, 256, 16

LANES = 128
SUBLANES = 8
BF16_ROWS = 16
HALF = 64

XBC0, FQ0, Z0, FK0, FV0, RQ0, RK0, RV0, RG0, TAIL0, NH = 0, 1536, 2048, 3072, 3584, 4096, 4608, 5120, 5632, 6144, 6400
FF_LANE0, DT_LANE0 = 0, 8


def _cparams(n_axes, vmem_mb=None):
    kw = dict(dimension_semantics=("arbitrary",) * n_axes)
    if vmem_mb is not None:
        kw["vmem_limit_bytes"] = vmem_mb << 20
    return pltpu.CompilerParams(**kw)


def _pick(n, prefs):
    for p in prefs:
        if n % p == 0:
            return p
    return n


def _dot(a, b):
    return jnp.dot(a, b, preferred_element_type=f32)


def _dot_nt(a, b):
    return lax.dot_general(a, b, (((1,), (1,)), ((), ())), preferred_element_type=f32)


def _dot_tn(a, b):
    return lax.dot_general(a, b, (((0,), (0,)), ((), ())), preferred_element_type=f32)


def _ln(x, g, b):
    mu = jnp.mean(x, -1, keepdims=True)
    xc = x - mu
    var = jnp.mean(xc * xc, -1, keepdims=True)
    return xc * lax.rsqrt(var + LN_EPS) * g + b


def _silu(x):
    return x * jax.nn.sigmoid(x)


def _softplus(x):
    return jnp.maximum(x, 0.0) + jnp.log1p(jnp.exp(-jnp.abs(x)))


def _ln_kernel(x_ref, g_ref, b_ref, o_ref):
    o_ref[...] = _ln(x_ref[...], g_ref[...], b_ref[...])


def layer_norm_rows(x, g, b, tm):
    T, D = x.shape
    return pl.pallas_call(
        _ln_kernel, grid=(T // tm,),
        in_specs=[pl.BlockSpec((tm, D), lambda i: (i, 0)),
                  pl.BlockSpec((1, D), lambda i: (0, 0)),
                  pl.BlockSpec((1, D), lambda i: (0, 0))],
        out_specs=pl.BlockSpec((tm, D), lambda i: (i, 0)),
        out_shape=jax.ShapeDtypeStruct((T, D), f32),
        compiler_params=_cparams(1, 48), name="ln_in")(x, g.reshape(1, D), b.reshape(1, D))


def _ln_res_kernel(x_ref, rt_ref, g_ref, b_ref, o_ref):
    o_ref[...] = _ln(ALPHA * x_ref[...] + rt_ref[...].T, g_ref[...], b_ref[...])


def ln_residual_t(x, rt, g, b, tm):
    T, D = x.shape
    return pl.pallas_call(
        _ln_res_kernel, grid=(T // tm,),
        in_specs=[pl.BlockSpec((tm, D), lambda i: (i, 0)),
                  pl.BlockSpec((D, tm), lambda i: (0, i)),
                  pl.BlockSpec((1, D), lambda i: (0, 0)),
                  pl.BlockSpec((1, D), lambda i: (0, 0))],
        out_specs=pl.BlockSpec((tm, D), lambda i: (i, 0)),
        out_shape=jax.ShapeDtypeStruct((T, D), f32),
        compiler_params=_cparams(1, 48), name="ln_res")(x, rt, g.reshape(1, D), b.reshape(1, D))


def _mm_kernel(x_ref, w_ref, o_ref, xb_ref):
    @pl.when(pl.program_id(1) == 0)
    def _():
        xb_ref[...] = x_ref[...].astype(bf16)

    o_ref[...] = _dot(xb_ref[...], w_ref[...]).astype(o_ref.dtype)


def matmul(x, w, tm, tn, name):
    T, K = x.shape
    N = w.shape[1]
    return pl.pallas_call(
        _mm_kernel, grid=(T // tm, N // tn),
        in_specs=[pl.BlockSpec((tm, K), lambda i, j: (i, 0)),
                  pl.BlockSpec((K, tn), lambda i, j: (0, j))],
        out_specs=pl.BlockSpec((tm, tn), lambda i, j: (i, j)),
        out_shape=jax.ShapeDtypeStruct((T, N), f32),
        scratch_shapes=[pltpu.VMEM((tm, K), bf16)],
        compiler_params=_cparams(2, 52), name=name)(x, w)


def _gate_kernel(t_ref, fb_ref, o_ref):
    x = t_ref[...] + fb_ref[...]
    o_ref[...] = jnp.minimum(x, 0.0) - jnp.log1p(jnp.exp(-jnp.abs(x)))


def forget_gate(h, fb_row, tm):
    T = h.shape[0]
    return pl.pallas_call(
        _gate_kernel, grid=(T // tm,),
        in_specs=[pl.BlockSpec((tm, LANES), lambda i: (i, TAIL0 // LANES)),
                  pl.BlockSpec((1, LANES), lambda i: (0, 0))],
        out_specs=pl.BlockSpec((tm, LANES), lambda i: (i, 0)),
        out_shape=jax.ShapeDtypeStruct((T, LANES), f32),
        compiler_params=_cparams(1), name="forget_gate")(h, fb_row)


def _cumsum_kernel(x_ref, o_ref, carry_ref):
    @pl.when(pl.program_id(1) == 0)
    def _():
        carry_ref[...] = jnp.zeros_like(carry_ref)

    bl = x_ref.shape[-1]
    r = lax.broadcasted_iota(jnp.int32, (bl, bl), 0)
    c = lax.broadcasted_iota(jnp.int32, (bl, bl), 1)
    upper = (r <= c).astype(f32)
    y = jnp.dot(x_ref[0], upper, precision=lax.Precision.HIGHEST,
                preferred_element_type=f32) + carry_ref[:, 0:1]
    o_ref[0] = y
    carry_ref[...] = jnp.broadcast_to(y[:, bl - 1:bl], carry_ref.shape)


def cumsum_lanes(x, bl):
    n, r, L = x.shape
    return pl.pallas_call(
        _cumsum_kernel, grid=(n, L // bl),
        in_specs=[pl.BlockSpec((1, r, bl), lambda s, j: (s, 0, j))],
        out_specs=pl.BlockSpec((1, r, bl), lambda s, j: (s, 0, j)),
        out_shape=jax.ShapeDtypeStruct((n, r, L), f32),
        scratch_shapes=[pltpu.VMEM((r, LANES), f32)],
        compiler_params=_cparams(2), name="cumsum")(x)


def _cumsum_rows_kernel(x_ref, o_ref, carry_ref):
    @pl.when(pl.program_id(1) == 0)
    def _():
        carry_ref[...] = jnp.zeros_like(carry_ref)

    bl = x_ref.shape[0]
    r = lax.broadcasted_iota(jnp.int32, (bl, bl), 0)
    c = lax.broadcasted_iota(jnp.int32, (bl, bl), 1)
    lower = (c <= r).astype(f32)
    y = jnp.dot(lower, x_ref[...], precision=lax.Precision.HIGHEST,
                preferred_element_type=f32) + carry_ref[0:1, :]
    o_ref[...] = y
    carry_ref[...] = jnp.broadcast_to(y[bl - 1:bl, :], carry_ref.shape)


def cumsum_rows(x, nseq, L, bl):
    nb = L // bl
    return pl.pallas_call(
        _cumsum_rows_kernel, grid=(nseq, nb),
        in_specs=[pl.BlockSpec((bl, LANES), lambda s, j: (s * nb + j, 0))],
        out_specs=pl.BlockSpec((bl, LANES), lambda s, j: (s * nb + j, 0)),
        out_shape=jax.ShapeDtypeStruct((nseq * L, LANES), f32),
        scratch_shapes=[pltpu.VMEM((SUBLANES, LANES), f32)],
        compiler_params=_cparams(2), name="cumsum_rows")(x)


_BIAS_TERMS = 3


def _split_heads(q, lane):
    zero = jnp.zeros_like(q)
    return jnp.where(lane, q, zero), jnp.where(lane, zero, q)


def _pack_queries(q, lane_i):
    qs = q * (DH_A ** -0.5)
    q0 = jnp.where(lane_i < HALF, qs, jnp.where(lane_i < HALF + _BIAS_TERMS, 1.0, 0.0))
    q1 = jnp.where(lane_i >= HALF, qs, jnp.where(lane_i < _BIAS_TERMS, 1.0, 0.0))
    return q0.astype(bf16), q1.astype(bf16)


def _pack_keys(k, c0, c1, lane_i):
    def one(own, c, base):
        out = jnp.where(own, k, 0.0)
        rest = -c
        for t in range(_BIAS_TERMS):
            term = rest.astype(bf16).astype(f32)
            out = jnp.where(lane_i == base + t, term, out)
            rest = rest - term
        return out.astype(bf16)
    return one(lane_i < HALF, c0, HALF), one(lane_i >= HALF, c1, 0)


def _pack_values(v, lane):
    return jnp.where(lane, v, 1.0).astype(bf16), jnp.where(lane, 1.0, v).astype(bf16)


def _attn_step(qs, ks, vs, carry, mask):
    out = []
    for q, k, v, (m, acc) in zip(qs, ks, vs, carry):
        s = _dot_nt(q, k)
        if mask is not None:
            s = jnp.where(mask, s, NEG_INF)
        mn = jnp.maximum(m, jnp.max(s, -1, keepdims=True))
        p = jnp.exp(s - mn).astype(bf16)
        out.append((mn, acc * jnp.exp(m - mn) + _dot(p, v)))
    return tuple(out)


def _attn_init(tq):
    return tuple((jnp.full((tq, 1), NEG_INF, f32), jnp.zeros((tq, LANES), f32)) for _ in range(2))


def _attn_finish(carry, lane):
    (_, acc0), (_, acc1) = carry
    o0 = acc0 / pltpu.roll(acc0, HALF, 1)
    o1 = acc1 / pltpu.roll(acc1, HALF, 1)
    return jnp.where(lane, o0, o1)


def _fox_prompt_kernel(q_ref, k_ref, v_ref, c_ref, o_ref, k0_ref, k1_ref, v0_ref, v1_ref, *, tq):
    i = pl.program_id(2)
    lane_i = lax.broadcasted_iota(jnp.int32, (tq, LANES), 1)
    lane = lane_i < HALF

    @pl.when(i == 0)
    def _():
        def pack(j, _):
            rows = pl.ds(pl.multiple_of(j * tq, tq), tq)
            c = c_ref[0, 0, rows, :]
            k0, k1 = _pack_keys(k_ref[rows, :], c[:, 0:1], c[:, 1:2], lane_i)
            v0, v1 = _pack_values(v_ref[rows, :], lane)
            k0_ref[rows, :] = k0
            k1_ref[rows, :] = k1
            v0_ref[rows, :] = v0
            v1_ref[rows, :] = v1
            return 0

        lax.fori_loop(0, k_ref.shape[0] // tq, pack, 0)

    qs = _pack_queries(q_ref[...], lane_i)

    def step(j, carry, mask):
        rows = pl.ds(pl.multiple_of(j * tq, tq), tq)
        return _attn_step(qs, (k0_ref[rows, :], k1_ref[rows, :]), (v0_ref[rows, :], v1_ref[rows, :]), carry, mask)

    carry = lax.fori_loop(0, i, lambda j, c: step(j, c, None), _attn_init(tq))
    r = lax.broadcasted_iota(jnp.int32, (tq, tq), 0)
    c = lax.broadcasted_iota(jnp.int32, (tq, tq), 1)
    carry = step(i, carry, c <= r)
    o_ref[...] = _attn_finish(carry, lane).astype(o_ref.dtype)


def fox_prompt(h, cp, B, S, tq):
    nq = S // tq
    qb, kb, vb = FQ0 // LANES, FK0 // LANES, FV0 // LANES
    return pl.pallas_call(
        functools.partial(_fox_prompt_kernel, tq=tq), grid=(B, H_A // 2, nq),
        in_specs=[pl.BlockSpec((tq, LANES), lambda b, p, i: (b * nq + i, qb + p)),
                  pl.BlockSpec((S, LANES), lambda b, p, i: (b, kb + p)),
                  pl.BlockSpec((S, LANES), lambda b, p, i: (b, vb + p)),
                  pl.BlockSpec((1, 1, S, 2), lambda b, p, i: (b, p, 0, 0))],
        out_specs=pl.BlockSpec((tq, LANES), lambda b, p, i: (b * nq + i, p)),
        out_shape=jax.ShapeDtypeStruct((B * S, W_A), bf16),
        scratch_shapes=[pltpu.VMEM((S, LANES), bf16) for _ in range(4)],
        compiler_params=_cparams(3, 52), name="fox_prompt")(h, h, h, cp)


def _fox_sample_kernel(q_ref, kn_ref, vn_ref, kc_ref, vc_ref, c_ref, o_ref, *, Ls, P, tk):
    lane = lax.broadcasted_iota(jnp.int32, (Ls, LANES), 1) < HALF
    qs = _pack_queries(q_ref[...], lax.broadcasted_iota(jnp.int32, (Ls, LANES), 1))

    def block(k, v, c, carry, mask):
        lane_i = lax.broadcasted_iota(jnp.int32, k.shape, 1)
        return _attn_step(qs, _pack_keys(k, c[:, 0:1], c[:, 1:2], lane_i), _pack_values(v, lane_i < HALF),
                          carry, mask)

    carry = _attn_init(Ls)
    for j in range(P // tk):
        rows = slice(j * tk, (j + 1) * tk)
        carry = block(kc_ref[0, rows, :], vc_ref[0, rows, :], c_ref[0, 0, rows, :], carry, None)
    r = lax.broadcasted_iota(jnp.int32, (Ls, Ls), 0)
    c = lax.broadcasted_iota(jnp.int32, (Ls, Ls), 1)
    carry = block(kn_ref[...], vn_ref[...], c_ref[0, 0, P:P + Ls, :], carry, c <= r)
    o_ref[...] = _attn_finish(carry, lane).astype(o_ref.dtype)


def fox_sample(h, ck, cv, cp, row0, Bs, Ls, P):
    qb, kb, vb = FQ0 // LANES, FK0 // LANES, FV0 // LANES
    tk = _pick(P, (1024, 512, 256, 128))
    Lp = cp.shape[2]
    return pl.pallas_call(
        functools.partial(_fox_sample_kernel, Ls=Ls, P=P, tk=tk), grid=(Bs, H_A // 2),
        in_specs=[pl.BlockSpec((Ls, LANES), lambda b, p: (row0 + b, qb + p)),
                  pl.BlockSpec((Ls, LANES), lambda b, p: (row0 + b, kb + p)),
                  pl.BlockSpec((Ls, LANES), lambda b, p: (row0 + b, vb + p)),
                  pl.BlockSpec((1, P, LANES), lambda b, p: (b, 0, p)),
                  pl.BlockSpec((1, P, LANES), lambda b, p: (b, 0, p)),
                  pl.BlockSpec((1, 1, Lp, 2), lambda b, p: (b, p, 0, 0))],
        out_specs=pl.BlockSpec((Ls, LANES), lambda b, p: (b, p)),
        out_shape=jax.ShapeDtypeStruct((Bs * Ls, W_A), bf16),
        compiler_params=_cparams(2, 48), name="fox_sample")(h, h, h, ck, cv, cp)


def _pair_mean(x, lane):
    s0 = jnp.sum(jnp.where(lane, x, 0.0), -1, keepdims=True)
    s1 = jnp.sum(jnp.where(lane, 0.0, x), -1, keepdims=True)
    return jnp.where(lane, s0, s1) * (1.0 / HALF)


def _rotary(x, cos, sin_signed, first_half):
    xr = jnp.where(first_half, pltpu.roll(x, LANES - HALF // 2, 1), pltpu.roll(x, HALF // 2, 1))
    return x * cos + xr * sin_signed


def _retention_kernel(q_ref, k_ref, v_ref, g_ref, cos_ref, sin_ref, dec_ref, gq_ref, gk_ref, gl_ref,
                      s0_ref, gn_ref, o_ref, s1_ref, st_ref, *, Lc):
    c = pl.program_id(2)

    @pl.when(c == 0)
    def _():
        st_ref[...] = s0_ref[0, 0]

    lane_i = lax.broadcasted_iota(jnp.int32, (Lc, LANES), 1)
    lane = lane_i < HALF
    first_half = (lane_i % HALF) < (HALF // 2)
    cos, sin = cos_ref[...], sin_ref[...]
    q = _rotary(q_ref[...], cos, sin, first_half)
    k = _rotary(k_ref[...], cos, sin, first_half) * (DK_B ** -0.5)
    qb, kb, vb = q.astype(bf16), k.astype(bf16), v_ref[...].astype(bf16)
    q0, q1 = _split_heads(qb, lane)
    a0 = (_dot_nt(q0, kb) * dec_ref[0]).astype(bf16)
    a1 = (_dot_nt(q1, kb) * dec_ref[1]).astype(bf16)
    intra = jnp.where(lane, _dot(a0, vb), _dot(a1, vb))
    st = st_ref[...]
    cross = _dot(qb, st.astype(bf16)) * gq_ref[0]
    o = intra + cross
    kd = (k * gk_ref[0]).astype(bf16)
    sr = lax.broadcasted_iota(jnp.int32, (LANES, LANES), 0) < HALF
    sc = lax.broadcasted_iota(jnp.int32, (LANES, LANES), 1) < HALF
    st_new = gl_ref[0, 0:1, :] * st + jnp.where(sr == sc, _dot_tn(kd, vb), 0.0)
    st_ref[...] = st_new
    mu = _pair_mean(o, lane)
    d = o - mu
    var = _pair_mean(d * d, lane)
    on = d * lax.rsqrt(var + LN_EPS) * gn_ref[...]
    o_ref[...] = (_silu(g_ref[...]) * on).astype(o_ref.dtype)

    @pl.when(c == pl.num_programs(2) - 1)
    def _():
        s1_ref[0, 0] = st_new


def _retention_tables(pos, Lc):
    half = DK_B // 2
    inv = ROPE_BASE ** (-jnp.arange(half, dtype=f32) / half)
    ang = pos.astype(f32)[:, None] * inv[None, :]
    cos, sin = jnp.cos(ang), jnp.sin(ang)
    cos_t = jnp.tile(cos, (1, 4))
    sin_t = jnp.tile(jnp.concatenate([-sin, sin], -1), (1, 2))
    lg = jnp.log1p(-jnp.exp2(-5.0 - jnp.arange(H_B, dtype=f32)))
    i = jnp.arange(Lc, dtype=f32)
    diff = i[:, None] - i[None, :]
    dec = jnp.exp(jnp.where((diff >= 0)[None], diff[None] * lg[:, None, None], NEG_INF))
    pair = lambda t: jnp.repeat(t.reshape(t.shape[0], H_B // 2, 2), HALF, axis=-1)
    gq = pair(jnp.exp((i[:, None] + 1.0) * lg[None, :])).transpose(1, 0, 2)
    gk = pair(jnp.exp((Lc - 1.0 - i)[:, None] * lg[None, :])).transpose(1, 0, 2)
    gl = jnp.broadcast_to(pair(jnp.exp(Lc * lg)[None, :]).transpose(1, 0, 2), (H_B // 2, SUBLANES, LANES))
    return cos_t, sin_t, dec, gq, gk, gl


def retention(h, gn_w, s0, pos, row0, nseq, L, Lc):
    nch = L // Lc
    cos_t, sin_t, dec, gq, gk, gl = _retention_tables(pos, Lc)
    blk = lambda col0: pl.BlockSpec((Lc, LANES), lambda s, p, c: (row0 + s * nch + c, col0 // LANES + p))
    tab = lambda: pl.BlockSpec((1, Lc, LANES), lambda s, p, c: (p, 0, 0))
    return pl.pallas_call(
        functools.partial(_retention_kernel, Lc=Lc), grid=(nseq, H_B // 2, nch),
        in_specs=[blk(RQ0), blk(RK0), blk(RV0), blk(RG0),
                  pl.BlockSpec((Lc, LANES), lambda s, p, c: (c, 0)),
                  pl.BlockSpec((Lc, LANES), lambda s, p, c: (c, 0)),
                  pl.BlockSpec((2, Lc, Lc), lambda s, p, c: (p, 0, 0)),
                  tab(), tab(),
                  pl.BlockSpec((1, SUBLANES, LANES), lambda s, p, c: (p, 0, 0)),
                  pl.BlockSpec((1, 1, LANES, LANES), lambda s, p, c: (s, p, 0, 0)),
                  pl.BlockSpec((1, LANES), lambda s, p, c: (0, p))],
        out_specs=[pl.BlockSpec((Lc, LANES), lambda s, p, c: (s * nch + c, p)),
                   pl.BlockSpec((1, 1, LANES, LANES), lambda s, p, c: (s, p, 0, 0))],
        out_shape=[jax.ShapeDtypeStruct((nseq * L, W_B), bf16),
                   jax.ShapeDtypeStruct((nseq, H_B // 2, LANES, LANES), f32)],
        scratch_shapes=[pltpu.VMEM((LANES, LANES), f32)],
        compiler_params=_cparams(3, 32), name="retention")(
            h, h, h, h, cos_t, sin_t, dec, gq, gk, gl, s0, gn_w.reshape(1, W_B))


def _pack_ret_state(s):
    n = s.shape[0]
    s = s.reshape(n, H_B // 2, 2, DK_B, DK_B)
    z = jnp.zeros_like(s[:, :, 0])
    top = jnp.concatenate([s[:, :, 0], z], -1)
    bot = jnp.concatenate([z, s[:, :, 1]], -1)
    return jnp.concatenate([top, bot], -2)


def _unpack_ret_state(s):
    n = s.shape[0]
    return jnp.stack([s[:, :, :HALF, :HALF], s[:, :, HALF:, HALF:]], 2).reshape(n, H_B, DK_B, DK_B)


def _ssd_kernel(xbc_ref, z_ref, t_ref, cw_ref, cb_ref, dtb_ref, alog_ref, dsk_ref, nw_ref, hist_ref, h0_ref,
                o_ref, h1_ref, xpad_ref, hs_ref, *, Lc):
    c = pl.program_id(1)

    @pl.when(c == 0)
    def _():
        xpad_ref[0:SUBLANES, :] = hist_ref[0]
        hs_ref[...] = h0_ref[0]

    xpad_ref[SUBLANES:SUBLANES + Lc, :] = xbc_ref[...]
    conv = cb_ref[...]
    for t in range(CONV_W):
        r0 = SUBLANES - (CONV_W - 1) + t
        conv = conv + xpad_ref[r0:r0 + Lc, :] * cw_ref[t:t + 1, :]
    xpad_ref[0:SUBLANES, :] = xpad_ref[Lc:Lc + SUBLANES, :]
    xc = _silu(conv)
    xs = xc[:, :W_C]
    bm = [xc[:, W_C + g * N_C:W_C + (g + 1) * N_C].astype(bf16) for g in range(G_C)]
    cm = [xc[:, W_C + (G_C + g) * N_C:W_C + (G_C + g + 1) * N_C].astype(bf16) for g in range(G_C)]

    lane_i = lax.broadcasted_iota(jnp.int32, (Lc, LANES), 1)
    lane = lane_i < HALF
    dt_valid = (lane_i >= DT_LANE0) & (lane_i < DT_LANE0 + H_C)
    dt = _softplus(t_ref[...] + dtb_ref[...])
    dta = jnp.where(dt_valid, dt * (-jnp.exp(alog_ref[...])), 0.0)
    r = lax.broadcasted_iota(jnp.int32, (Lc, Lc), 0)
    cidx = lax.broadcasted_iota(jnp.int32, (Lc, Lc), 1)
    tri = cidx <= r
    a_cs = jnp.dot(tri.astype(f32), dta, precision=lax.Precision.HIGHEST, preferred_element_type=f32)
    a_cs_t = a_cs.T
    dt_t = dt.T
    row_first = lax.broadcasted_iota(jnp.int32, (LANES, LANES), 0) < HALF

    ys = []
    for p in range(H_C // 2):
        g = (2 * p) // (H_C // G_C)
        xpair = xs[:, p * LANES:(p + 1) * LANES]
        xpair_b = xpair.astype(bf16)
        if p % (H_C // G_C // 2) == 0:
            cb = _dot_nt(cm[g], bm[g])
        yh, acol, dcol = [], [], []
        for hd in (2 * p, 2 * p + 1):
            li = DT_LANE0 + hd
            ac = a_cs[:, li:li + 1]
            seg = ac - a_cs_t[li:li + 1, :]
            w = cb * jnp.exp(jnp.where(tri, seg, NEG_INF)) * dt_t[li:li + 1, :]
            yh.append(_dot(w.astype(bf16), xpair_b))
            acol.append(ac)
            dcol.append(dt[:, li:li + 1])
        acs_pair = jnp.where(lane, acol[0], acol[1])
        dt_pair = jnp.where(lane, dcol[0], dcol[1])
        hs = hs_ref[p]
        y = jnp.where(lane, yh[0], yh[1]) + _dot_nt(cm[g], hs.astype(bf16)) * jnp.exp(acs_pair)
        ys.append(y)
        a_last = acs_pair[Lc - 1:Lc, :]
        to_end = jnp.exp(a_last - acs_pair) * dt_pair
        upd = _dot_tn((xpair * to_end).astype(bf16), bm[g])
        sdec = jnp.exp(jnp.where(row_first, acol[0][Lc - 1:Lc, :], acol[1][Lc - 1:Lc, :]))
        hs_ref[p] = sdec * hs + upd

    y = jnp.concatenate(ys, axis=1) + dsk_ref[...] * xs
    y = y * _silu(z_ref[...])
    y = y * lax.rsqrt(jnp.mean(y * y, -1, keepdims=True) + LN_EPS) * nw_ref[...]
    o_ref[...] = y.astype(o_ref.dtype)

    @pl.when(c == pl.num_programs(1) - 1)
    def _():
        h1_ref[0] = hs_ref[...]


def ssd(h, conv_w, conv_b, dt_bias, a_log, d_skip, norm_w, hist, h0, row0, nseq, L, Lc):
    nch = L // Lc
    lane_row = lambda v: jnp.zeros((1, LANES), f32).at[0, DT_LANE0:DT_LANE0 + H_C].set(v)
    cw = jnp.zeros((SUBLANES, CONV_DIM), f32).at[:CONV_W].set(conv_w)
    row = lambda w: pl.BlockSpec((1, w), lambda s, c: (0, 0))
    return pl.pallas_call(
        functools.partial(_ssd_kernel, Lc=Lc), grid=(nseq, nch),
        in_specs=[pl.BlockSpec((Lc, CONV_DIM), lambda s, c: (row0 + s * nch + c, XBC0 // CONV_DIM)),
                  pl.BlockSpec((Lc, W_C), lambda s, c: (row0 + s * nch + c, Z0 // W_C)),
                  pl.BlockSpec((Lc, LANES), lambda s, c: (row0 + s * nch + c, TAIL0 // LANES)),
                  pl.BlockSpec((SUBLANES, CONV_DIM), lambda s, c: (0, 0)),
                  row(CONV_DIM), row(LANES), row(LANES), row(W_C), row(W_C),
                  pl.BlockSpec((1, SUBLANES, CONV_DIM), lambda s, c: (s, 0, 0)),
                  pl.BlockSpec((1, H_C // 2, LANES, LANES), lambda s, c: (s, 0, 0, 0))],
        out_specs=[pl.BlockSpec((Lc, W_C), lambda s, c: (s * nch + c, 0)),
                   pl.BlockSpec((1, H_C // 2, LANES, LANES), lambda s, c: (s, 0, 0, 0))],
        out_shape=[jax.ShapeDtypeStruct((nseq * L, W_C), bf16),
                   jax.ShapeDtypeStruct((nseq, H_C // 2, LANES, LANES), f32)],
        scratch_shapes=[pltpu.VMEM((Lc + SUBLANES, CONV_DIM), f32),
                        pltpu.VMEM((H_C // 2, LANES, LANES), f32)],
        compiler_params=_cparams(2, 48), name="ssd")(
            h, h, h, cw, conv_b.reshape(1, CONV_DIM), lane_row(dt_bias), lane_row(a_log),
            jnp.repeat(d_skip, P_C).reshape(1, W_C), norm_w.reshape(1, W_C), hist, h0)


def _outproj_kernel(x_ref, a_ref, b_ref, c_ref, w_ref, g_ref, beta_ref, o_ref):
    mix = (_dot(a_ref[...], w_ref[0:W_A, :]) + _dot(b_ref[...], w_ref[W_A:W_A + W_B, :])
           + _dot(c_ref[...], w_ref[W_A + W_B:, :]))
    o_ref[...] = _ln(ALPHA * x_ref[...] + mix, g_ref[...], beta_ref[...])


def outproj_ln(x, ao, bo, co, w, g, b, tm):
    T, D = x.shape
    rows = lambda wd: pl.BlockSpec((tm, wd), lambda i: (i, 0))
    const = lambda s: pl.BlockSpec(s, lambda i: (0, 0))
    return pl.pallas_call(
        _outproj_kernel, grid=(T // tm,),
        in_specs=[rows(D), rows(W_A), rows(W_B), rows(W_C), const(w.shape), const((1, D)), const((1, D))],
        out_specs=rows(D),
        out_shape=jax.ShapeDtypeStruct((T, D), f32),
        compiler_params=_cparams(1, 52), name="outproj_ln")(x, ao, bo, co, w, g.reshape(1, D), b.reshape(1, D))


def _mem_kernel(x_ref, wq_ref, mk_ref, mv_ref, wo_ref, g_ref, b_ref, o_ref):
    x = x_ref[...]
    q = _dot(x.astype(bf16), wq_ref[...])
    outs = []
    for hd in range(H_M):
        sl = slice(hd * DH_M, (hd + 1) * DH_M)
        s = _dot_nt(q[:, sl].astype(bf16), mk_ref[0, :, sl].astype(bf16)) * (DH_M ** -0.5)
        p = jnp.exp(s - jnp.max(s, -1, keepdims=True))
        p = p / jnp.sum(p, -1, keepdims=True)
        outs.append(_dot(p.astype(bf16), mv_ref[0, :, sl].astype(bf16)))
    o = jnp.concatenate(outs, axis=1).astype(bf16)
    o_ref[...] = _ln(ALPHA * x + _dot(o, wo_ref[...]), g_ref[...], b_ref[...])


def mem_attn_ln(x, wq, mk, mv, wo, g, b, row0, nseq, L, tr):
    D = x.shape[1]
    per = L // tr
    nm = mk.shape[1]
    const = lambda s: pl.BlockSpec(s, lambda i: (0,) * len(s))
    return pl.pallas_call(
        _mem_kernel, grid=(nseq * per,),
        in_specs=[pl.BlockSpec((tr, D), lambda i: (row0 + i, 0)), const(wq.shape),
                  pl.BlockSpec((1, nm, W_M), lambda i: (i // per, 0, 0)),
                  pl.BlockSpec((1, nm, W_M), lambda i: (i // per, 0, 0)),
                  const(wo.shape), const((1, D)), const((1, D))],
        out_specs=pl.BlockSpec((tr, D), lambda i: (i, 0)),
        out_shape=jax.ShapeDtypeStruct((nseq * L, D), f32),
        compiler_params=_cparams(1, 48), name="mem_attn_ln")(x, wq, mk, mv, wo, g.reshape(1, D), b.reshape(1, D))


def _peer_score_kernel(x_ref, wq_ref, k1_ref, k2_ref, o_ref):
    q = _dot(x_ref[...].astype(bf16), wq_ref[...])
    k1 = k1_ref[...].astype(bf16)
    k2 = k2_ref[...].astype(bf16)
    hk = PEER_DK // 2
    for hd in range(PEER_HEADS):
        q1 = q[:, hd * PEER_DK:hd * PEER_DK + hk].astype(bf16)
        q2 = q[:, hd * PEER_DK + hk:(hd + 1) * PEER_DK].astype(bf16)
        o_ref[hd, 0:PEER_NK, :] = _dot_nt(k1, q1)
        o_ref[hd, PEER_NK:2 * PEER_NK, :] = _dot_nt(k2, q2)


def peer_scores(x, wq, k1, k2, tm):
    T, D = x.shape
    const = lambda s: pl.BlockSpec(s, lambda i: (0, 0))
    return pl.pallas_call(
        _peer_score_kernel, grid=(T // tm,),
        in_specs=[pl.BlockSpec((tm, D), lambda i: (i, 0)), const(wq.shape), const(k1.shape), const(k2.shape)],
        out_specs=pl.BlockSpec((PEER_HEADS, 2 * PEER_NK, tm), lambda i: (0, 0, i)),
        out_shape=jax.ShapeDtypeStruct((PEER_HEADS, 2 * PEER_NK, T), f32),
        compiler_params=_cparams(1, 52), name="peer_scores")(x, wq, k1, k2)


_CAND_ROWS = PEER_TOPK + 7 * SUBLANES + SUBLANES


def _cand_flat_index():
    idx = [0 * PEER_TOPK + b for b in range(PEER_TOPK)]
    for a in range(1, 8):
        idx += [a * PEER_TOPK + b for b in range(SUBLANES)]
    idx += [a * PEER_TOPK for a in range(8, PEER_TOPK)]
    return np.broadcast_to(np.asarray(idx, np.float32)[:, None], (_CAND_ROWS, LANES)).copy()


def _extract_top(s, key_idx, n):
    rank = jnp.full(s.shape, float(n), f32)
    vals = []
    for a in range(n):
        mx = jnp.max(s, axis=0, keepdims=True)
        first = jnp.min(jnp.where(s == mx, key_idx, float(1 << 20)), axis=0, keepdims=True)
        sel = key_idx == first
        rank = jnp.where(sel, float(a), rank)
        s = jnp.where(sel, NEG_INF, s)
        vals.append(mx)
    return vals, rank


def _top_values(s, n, with_rank):
    rank = jnp.full(s.shape, float(n), f32) if with_rank else None
    vals = []
    for a in range(n):
        mx = jnp.max(s, axis=0, keepdims=True)
        hit = s == mx
        if with_rank:
            rank = jnp.where(hit, float(a), rank)
        s = jnp.where(hit, NEG_INF, s)
        vals.append(mx)
    return vals, rank


def _candidates(v1, v2):
    v1a = jnp.concatenate(v1, axis=0)
    v2a = jnp.concatenate(v2, axis=0)
    return jnp.concatenate([v1[0] + v2a] + [v1[a] + v2a[0:SUBLANES] for a in range(1, 8)]
                           + [v1a[SUBLANES:] + v2[0]], axis=0)


def _counts_per_rank(picked):
    cnt_a = [jnp.sum(picked[0:PEER_TOPK], axis=0, keepdims=True)]
    for a in range(1, 8):
        lo = PEER_TOPK + (a - 1) * SUBLANES
        cnt_a.append(jnp.sum(picked[lo:lo + SUBLANES], axis=0, keepdims=True))
    tail = picked[PEER_TOPK + 7 * SUBLANES:]
    return cnt_a + [tail[a:a + 1] for a in range(SUBLANES)]


def _select_fast(s1, s2):
    v1, _ = _top_values(s1, PEER_TOPK, False)
    v2, rank2 = _top_values(s2, PEER_TOPK, True)
    cand = _candidates(v1, v2)
    vc, _ = _top_values(cand, PEER_TOPK, False)
    top = v1[0] + v2[0]
    zsum = jnp.exp(vc[0] - top)
    for k in range(1, PEER_TOPK):
        zsum = zsum + jnp.exp(vc[k] - top)
    picked = jnp.where(cand >= vc[PEER_TOPK - 1], 1.0, 0.0)
    cnt_a = _counts_per_rank(picked)
    cnt = jnp.zeros(s1.shape, f32)
    for a in range(PEER_TOPK):
        cnt = jnp.where(s1 == v1[a], cnt_a[a], cnt)
    n1 = jnp.sum(jnp.where(s1 >= v1[PEER_TOPK - 1], 1.0, 0.0), axis=0, keepdims=True)
    n2 = jnp.sum(jnp.where(rank2 < float(PEER_TOPK), 1.0, 0.0), axis=0, keepdims=True)
    nc = jnp.sum(picked, axis=0, keepdims=True)
    k = float(PEER_TOPK)
    distinct = (n1 == k) & (n2 == k) & (nc == k)
    tie = jnp.max(jnp.where(distinct, 0.0, 1.0)) > 0.0
    return rank2, cnt, zsum, v1[0], v2[0], tie


def _select_exact(s1, s2, key_idx, cidx):
    v1, rank1 = _extract_top(s1, key_idx, PEER_TOPK)
    v2, rank2 = _extract_top(s2, key_idx, PEER_TOPK)
    cand = _candidates(v1, v2)
    top = v1[0] + v2[0]
    picked = jnp.zeros(cand.shape, f32)
    zsum = jnp.zeros((1, LANES), f32)
    for _k in range(PEER_TOPK):
        mx = jnp.max(cand, axis=0, keepdims=True)
        first = jnp.min(jnp.where(cand == mx, cidx, float(1 << 20)), axis=0, keepdims=True)
        sel = cidx == first
        picked = jnp.where(sel, 1.0, picked)
        cand = jnp.where(sel, NEG_INF, cand)
        zsum = zsum + jnp.exp(mx - top)
    cnt_a = _counts_per_rank(picked)
    cnt = jnp.zeros(s1.shape, f32)
    for a in range(PEER_TOPK):
        cnt = jnp.where(rank1 == float(a), cnt_a[a], cnt)
    return rank2, cnt, zsum


def _peer_topk_kernel(s_ref, cidx_ref, r2_ref, g2_ref, cnt_ref, g1_ref, *, tb):
    nj = PEER_NK // SUBLANES

    def chunk(ci, _):
        off = pl.multiple_of(ci * LANES, LANES)
        s1 = s_ref[0, 0:PEER_NK, pl.ds(off, LANES)]
        s2 = s_ref[0, PEER_NK:2 * PEER_NK, pl.ds(off, LANES)]
        rank2, cnt, zsum, m1, m2, tie = _select_fast(s1, s2)
        e1 = jnp.exp(s1 - m1)
        r2_ref[0, :, pl.ds(off, LANES)] = rank2.astype(bf16)
        g2_ref[0, :, pl.ds(off, LANES)] = jnp.exp(s2 - m2).astype(bf16)
        cnt_ref[:, 0, :, pl.ds(off, LANES)] = cnt.reshape(nj, SUBLANES, LANES)
        g1_ref[:, 0, :, pl.ds(off, LANES)] = (e1 / zsum).reshape(nj, SUBLANES, LANES)

        @pl.when(tie)
        def _():
            key_idx = lax.broadcasted_iota(jnp.int32, (PEER_NK, LANES), 0).astype(f32)
            rank2x, cntx, zsumx = _select_exact(s1, s2, key_idx, cidx_ref[...])
            r2_ref[0, :, pl.ds(off, LANES)] = rank2x.astype(bf16)
            cnt_ref[:, 0, :, pl.ds(off, LANES)] = cntx.reshape(nj, SUBLANES, LANES)
            g1_ref[:, 0, :, pl.ds(off, LANES)] = (e1 / zsumx).reshape(nj, SUBLANES, LANES)

        return 0

    lax.fori_loop(0, tb // LANES, chunk, 0)


def peer_topk(sT, tb):
    T = sT.shape[-1]
    nj = PEER_NK // SUBLANES
    per_key = lambda: pl.BlockSpec((1, PEER_NK, tb), lambda hd, i: (hd, 0, i))
    per_row = lambda: pl.BlockSpec((nj, 1, SUBLANES, tb), lambda hd, i: (0, hd, 0, i))
    r2, g2, cnt, g1 = pl.pallas_call(
        functools.partial(_peer_topk_kernel, tb=tb), grid=(PEER_HEADS, T // tb),
        in_specs=[pl.BlockSpec((1, 2 * PEER_NK, tb), lambda hd, i: (hd, 0, i)),
                  pl.BlockSpec((_CAND_ROWS, LANES), lambda hd, i: (0, 0))],
        out_specs=[per_key(), per_key(), per_row(), per_row()],
        out_shape=[jax.ShapeDtypeStruct((PEER_HEADS, PEER_NK, T), bf16),
                   jax.ShapeDtypeStruct((PEER_HEADS, PEER_NK, T), bf16),
                   jax.ShapeDtypeStruct((nj, PEER_HEADS, SUBLANES, T), f32),
                   jax.ShapeDtypeStruct((nj, PEER_HEADS, SUBLANES, T), f32)],
        compiler_params=_cparams(2, 32), name="peer_topk")(sT, jnp.asarray(_cand_flat_index()))
    packed = (PEER_HEADS, PEER_NK // BF16_ROWS, BF16_ROWS, T)
    return (r2.reshape(packed), g2.reshape(packed),
            cnt.reshape(nj, PEER_HEADS * SUBLANES, T), g1.reshape(nj, PEER_HEADS * SUBLANES, T))


def _peer_main_kernel(x_ref, u_ref, vt_ref, r2_ref, g2_ref, cnt_ref, g1_ref, o_ref):
    j = pl.program_id(1)
    tm = x_ref.shape[0]

    @pl.when(j == 0)
    def _():
        o_ref[...] = jnp.zeros_like(o_ref)

    a = _dot_nt(u_ref[...], x_ref[...])
    act = (0.5 * a * (1.0 + lax.erf(a * (2.0 ** -0.5)))).astype(bf16)
    zero = jnp.zeros((), bf16)
    parts = []
    for r in range(SUBLANES):
        w = None
        for hd in range(PEER_HEADS):
            row = hd * SUBLANES + r
            cnt = jnp.broadcast_to(cnt_ref[0, row:row + 1, :], (BF16_ROWS, tm)).astype(bf16)[None]
            g1 = jnp.broadcast_to(g1_ref[0, row:row + 1, :], (BF16_ROWS, tm)).astype(bf16)[None]
            sel = jnp.where(r2_ref[hd] < cnt, g2_ref[hd], zero) * g1
            w = sel if w is None else w + sel
        parts.append(w.reshape(PEER_NK, tm) * act[r * PEER_NK:(r + 1) * PEER_NK])
    o_ref[...] += _dot(vt_ref[...], jnp.concatenate(parts, axis=0))


def peer_main(xb, ub, vtb, r2, g2, cnt, g1, tm):
    T, D = xb.shape
    te = SUBLANES * PEER_NK
    nj = PEER_NK // SUBLANES
    once = pl.Buffered(1)
    per_tok = lambda: pl.BlockSpec((PEER_HEADS, PEER_NK // BF16_ROWS, BF16_ROWS, tm),
                                   lambda i, j: (0, 0, 0, i), pipeline_mode=once)
    per_row = lambda: pl.BlockSpec((1, PEER_HEADS * SUBLANES, tm), lambda i, j: (j, 0, i))
    return pl.pallas_call(
        _peer_main_kernel, grid=(T // tm, nj),
        in_specs=[pl.BlockSpec((tm, D), lambda i, j: (i, 0), pipeline_mode=once),
                  pl.BlockSpec((te, D), lambda i, j: (j, 0)),
                  pl.BlockSpec((D, te), lambda i, j: (0, j)),
                  per_tok(), per_tok(), per_row(), per_row()],
        out_specs=pl.BlockSpec((D, tm), lambda i, j: (0, i), pipeline_mode=once),
        out_shape=jax.ShapeDtypeStruct((D, T), f32),
        compiler_params=_cparams(2, 58), name="peer_main")(xb, ub, vtb, r2, g2, cnt, g1)


def _rearrange_w_in(w):
    o = np.cumsum([0, W_A, W_A, W_A, H_A, H_B * DK_B, H_B * DK_B, W_B, W_B, W_C, CONV_DIM, H_C])
    fq, fk, fv, ff, rq, rk, rv, rg, z, xbc, dtr = [slice(int(o[i]), int(o[i + 1])) for i in range(11)]
    parts = [w[:, s] for s in (xbc, fq, z, fk, fv, rq, rk, rv, rg, ff, dtr)]
    pad = NH - TAIL0 - H_A - H_C
    return jnp.concatenate(parts + [jnp.zeros((w.shape[0], pad), w.dtype)], axis=1).astype(bf16)


def kernel(x_prompt, x_sample, cache_fox_k, cache_fox_v, cache_fox_logf, state_ret, state_ssm, state_conv, cache_mem_k, cache_mem_v, mem_prompt, ln_in_g, ln_in_b, w_in, fox_fb, ret_gn_w, conv_w, conv_b, dt_bias, a_log, d_skip, ssm_norm_w, w_out, ln1_g, ln1_b, wq_mem, wkv_mem, wo_mem, ln2_g, ln2_b, peer_wq, peer_k1, peer_k2, peer_u, peer_v, ln3_g, ln3_b):
    B, S, D = x_prompt.shape
    Bs, Ls, _ = x_sample.shape
    P = cache_fox_k.shape[2]
    NM = mem_prompt.shape[1]
    Tp, Ts = B * S, Bs * Ls
    T = Tp + Ts
    assert Tp % Ls == 0 and T % LANES == 0 and Ls >= CONV_W - 1

    tm = _pick(T, (640, 768, 384, 256, 128))
    tm_peer = _pick(T, (1280, 768, 256, 128))
    tq = _pick(S, (512, 256, 128))
    lc = _pick(S, (256, 128, 64))
    tr_mem = _pick(S, (512, 256, 128))
    cs_bl = 512
    tn_in = _pick(NH, (1280, 640))

    x = jnp.concatenate([x_prompt.reshape(Tp, D), x_sample.reshape(Ts, D)], axis=0)
    x = layer_norm_rows(x, ln_in_g, ln_in_b, tm)

    pos_p = jnp.arange(S)
    pos_s = P + jnp.arange(Ls)
    zeros_ret = jnp.zeros((B, H_B // 2, LANES, LANES), f32)
    zeros_ssm = jnp.zeros((B, H_C // 2, LANES, LANES), f32)
    zeros_hist = jnp.zeros((B, SUBLANES, CONV_DIM), f32)
    lp_s = -(-(P + Ls) // cs_bl) * cs_bl

    st_p = [[] for _ in range(8)]
    st_s = [[] for _ in range(6)]
    for l in range(DEPTH):
        h = matmul(x, _rearrange_w_in(w_in[l]), tm, tn_in, "in_proj")
        fb_row = jnp.zeros((1, LANES), f32).at[0, FF_LANE0:FF_LANE0 + H_A].set(fox_fb[l])
        logf_pad = forget_gate(h, fb_row, tm)
        logf = logf_pad[:, FF_LANE0:FF_LANE0 + H_A]
        logf_p = logf[:Tp].reshape(B, S, H_A)
        logf_s = logf[Tp:].reshape(Bs, Ls, H_A)

        c_p = cumsum_rows(logf_pad, B, S, min(cs_bl, S))[:, FF_LANE0:FF_LANE0 + H_A]
        c_p = c_p.reshape(B, S, H_A // 2, 2).transpose(0, 2, 1, 3)
        ao_p = fox_prompt(h, c_p, B, S, tq)
        lf_all = jnp.concatenate([cache_fox_logf[l].astype(f32), logf_s], axis=1).transpose(0, 2, 1)
        lf_all = jnp.pad(lf_all, ((0, 0), (0, 0), (0, lp_s - (P + Ls))))
        c_s = cumsum_lanes(lf_all, cs_bl)[:, :, :P + Ls].reshape(Bs, H_A // 2, 2, P + Ls).transpose(0, 1, 3, 2)
        ao_s = fox_sample(h, cache_fox_k[l].reshape(Bs, P, W_A), cache_fox_v[l].reshape(Bs, P, W_A), c_s,
                          Tp // Ls, Bs, Ls, P)

        bo_p, ret_p = retention(h, ret_gn_w[l], zeros_ret, pos_p, 0, B, S, lc)
        bo_s, ret_s = retention(h, ret_gn_w[l], _pack_ret_state(state_ret[l].astype(f32)), pos_s,
                                Tp // Ls, Bs, Ls, Ls)

        ssd_prm = (conv_w[l], conv_b[l], dt_bias[l], a_log[l], d_skip[l], ssm_norm_w[l])
        co_p, ssm_p = ssd(h, *ssd_prm, zeros_hist, zeros_ssm, 0, B, S, lc)
        hist_s = jnp.pad(state_conv[l].astype(f32), ((0, 0), (SUBLANES - (CONV_W - 1), 0), (0, 0)))
        co_s, ssm_s = ssd(h, *ssd_prm, hist_s, state_ssm[l].astype(f32).reshape(Bs, H_C // 2, LANES, LANES),
                          Tp // Ls, Bs, Ls, Ls)

        cat = lambda a, b: jnp.concatenate([a, b], axis=0)
        x = outproj_ln(x, cat(ao_p, ao_s), cat(bo_p, bo_s), cat(co_p, co_s), w_out[l].astype(bf16),
                       ln1_g[l], ln1_b[l], tm)

        mkv = matmul(mem_prompt.reshape(B * NM, D), wkv_mem[l].astype(bf16), _pick(B * NM, (512, 256, 128)),
                     _pick(2 * W_M, (512, 256)), "mem_kv")
        mk_p = mkv[:, :W_M].reshape(B, NM, W_M)
        mv_p = mkv[:, W_M:].reshape(B, NM, W_M)
        wq_b, wo_b = wq_mem[l].astype(bf16), wo_mem[l].astype(bf16)
        x_p = mem_attn_ln(x, wq_b, mk_p, mv_p, wo_b, ln2_g[l], ln2_b[l], 0, B, S, tr_mem)
        x_s = mem_attn_ln(x, wq_b, cache_mem_k[l].reshape(Bs, NM, W_M), cache_mem_v[l].reshape(Bs, NM, W_M),
                          wo_b, ln2_g[l], ln2_b[l], Tp // Ls, Bs, Ls, Ls)
        x = cat(x_p, x_s)

        sT = peer_scores(x, peer_wq[l].astype(bf16), peer_k1[l], peer_k2[l], tm)
        r2, g2, cnt, g1 = peer_topk(sT, tm_peer)
        pe_t = peer_main(x.astype(bf16), peer_u[l].astype(bf16), peer_v[l].T.astype(bf16), r2, g2, cnt, g1,
                         tm_peer)
        x = ln_residual_t(x, pe_t, ln3_g[l], ln3_b[l], tm)

        hp = h[:Tp].reshape(B, S, NH)
        hs = h[Tp:].reshape(Bs, Ls, NH)
        new_p = (hp[..., FK0:FK0 + W_A].reshape(B, S, H_A, DH_A), hp[..., FV0:FV0 + W_A].reshape(B, S, H_A, DH_A),
                 logf_p, _unpack_ret_state(ret_p), ssm_p.reshape(B, H_C, P_C, N_C),
                 hp[:, S - (CONV_W - 1):, XBC0:XBC0 + CONV_DIM],
                 mk_p.reshape(B, NM, H_M, DH_M), mv_p.reshape(B, NM, H_M, DH_M))
        new_s = (hs[..., FK0:FK0 + W_A].reshape(Bs, Ls, H_A, DH_A), hs[..., FV0:FV0 + W_A].reshape(Bs, Ls, H_A, DH_A),
                 logf_s, _unpack_ret_state(ret_s), ssm_s.reshape(Bs, H_C, P_C, N_C),
                 hs[:, Ls - (CONV_W - 1):, XBC0:XBC0 + CONV_DIM])
        for j, a in enumerate(new_p):
            st_p[j].append(a)
        for j, a in enumerate(new_s):
            st_s[j].append(a)

    outs_p = [jnp.stack(a) for a in st_p]
    outs_s = [jnp.stack(a) for a in st_s]
    return (x[:Tp].reshape(B, S, D), x[Tp:].reshape(Bs, Ls, D), *outs_p, *outs_s)
```

```python
import functools
import math

import numpy as np
import jax
import jax.numpy as jnp
from jax import lax
from jax.experimental import pallas as pl
from jax.experimental.pallas import tpu as pltpu

f32 = jnp.float32
bf16 = jnp.bfloat16
NEG_INF = float("-inf")

LN_EPS = 1e-5
DEPTH = 4
ALPHA = (2 * DEPTH) ** 0.25
H_A, DH_A = 8, 64
H_B, DK_B = 8, 64
H_C, P_C, N_C, G_C = 16, 64, 128, 2
W_A, W_B, W_C = 512, 512, 1024
CONV_W = 4
CONV_DIM = W_C + 2 * G_C * N_C
ROPE_BASE = 10000.0
H_M, DH_M = 4, 128
W_M = H_M * DH_M
PEER_NK, PEER_HEADS, PEER_DK, PEER_TOPK = 128, 8, 256, 16

LANES = 128
SUBLANES = 8
BF16_ROWS = 16
HALF = 64

XBC0, FQ0, Z0, FK0, FV0, RQ0, RK0, RV0, RG0, TAIL0, NH = 0, 1536, 2048, 3072, 3584, 4096, 4608, 5120, 5632, 6144, 6400
FF_LANE0, DT_LANE0 = 0, 8


def _cparams(n_axes, vmem_mb=None):
    kw = dict(dimension_semantics=("arbitrary",) * n_axes)
    if vmem_mb is not None:
        kw["vmem_limit_bytes"] = vmem_mb << 20
    return pltpu.CompilerParams(**kw)


def _pick(n, prefs):
    for p in prefs:
        if n % p == 0:
            return p
    return n


def _dot(a, b):
    return jnp.dot(a, b, preferred_element_type=f32)


def _dot_nt(a, b):
    return lax.dot_general(a, b, (((1,), (1,)), ((), ())), preferred_element_type=f32)


def _dot_tn(a, b):
    return lax.dot_general(a, b, (((0,), (0,)), ((), ())), preferred_element_type=f32)


def _ln(x, g, b):
    mu = jnp.mean(x, -1, keepdims=True)
    xc = x - mu
    var = jnp.mean(xc * xc, -1, keepdims=True)
    return xc * lax.rsqrt(var + LN_EPS) * g + b


def _silu(x):
    return x * jax.nn.sigmoid(x)


def _softplus(x):
    return jnp.maximum(x, 0.0) + jnp.log1p(jnp.exp(-jnp.abs(x)))


def _ln_kernel(x_ref, g_ref, b_ref, o_ref):
    o_ref[...] = _ln(x_ref[...], g_ref[...], b_ref[...])


def layer_norm_rows(x, g, b, tm):
    T, D = x.shape
    return pl.pallas_call(
        _ln_kernel, grid=(T // tm,),
        in_specs=[pl.BlockSpec((tm, D), lambda i: (i, 0)),
                  pl.BlockSpec((1, D), lambda i: (0, 0)),
                  pl.BlockSpec((1, D), lambda i: (0, 0))],
        out_specs=pl.BlockSpec((tm, D), lambda i: (i, 0)),
        out_shape=jax.ShapeDtypeStruct((T, D), f32),
        compiler_params=_cparams(1, 48), name="ln_in")(x, g.reshape(1, D), b.reshape(1, D))


def _ln_res_kernel(x_ref, rt_ref, g_ref, b_ref, o_ref):
    o_ref[...] = _ln(ALPHA * x_ref[...] + rt_ref[...].T, g_ref[...], b_ref[...])


def ln_residual_t(x, rt, g, b, tm):
    T, D = x.shape
    return pl.pallas_call(
        _ln_res_kernel, grid=(T // tm,),
        in_specs=[pl.BlockSpec((tm, D), lambda i: (i, 0)),
                  pl.BlockSpec((D, tm), lambda i: (0, i)),
                  pl.BlockSpec((1, D), lambda i: (0, 0)),
                  pl.BlockSpec((1, D), lambda i: (0, 0))],
        out_specs=pl.BlockSpec((tm, D), lambda i: (i, 0)),
        out_shape=jax.ShapeDtypeStruct((T, D), f32),
        compiler_params=_cparams(1, 48), name="ln_res")(x, rt, g.reshape(1, D), b.reshape(1, D))


def _mm_kernel(x_ref, w_ref, o_ref, xb_ref):
    @pl.when(pl.program_id(1) == 0)
    def _():
        xb_ref[...] = x_ref[...].astype(bf16)

    o_ref[...] = _dot(xb_ref[...], w_ref[...]).astype(o_ref.dtype)


def matmul(x, w, tm, tn, name, layer=None):
    T, K = x.shape
    N = w.shape[-1]
    if layer is None:
        w_spec = pl.BlockSpec((K, tn), lambda i, j: (0, j))
    else:
        w_spec = pl.BlockSpec((None, K, tn), lambda i, j: (layer, 0, j))
    return pl.pallas_call(
        _mm_kernel, grid=(T // tm, N // tn),
        in_specs=[pl.BlockSpec((tm, K), lambda i, j: (i, 0)), w_spec],
        out_specs=pl.BlockSpec((tm, tn), lambda i, j: (i, j)),
        out_shape=jax.ShapeDtypeStruct((T, N), f32),
        scratch_shapes=[pltpu.VMEM((tm, K), bf16)],
        compiler_params=_cparams(2, 52), name=name)(x, w)


def _gate_kernel(t_ref, fb_ref, o_ref):
    x = t_ref[...] + fb_ref[...]
    o_ref[...] = jnp.minimum(x, 0.0) - jnp.log1p(jnp.exp(-jnp.abs(x)))


def forget_gate(h, fb_row, tm):
    T = h.shape[0]
    return pl.pallas_call(
        _gate_kernel, grid=(T // tm,),
        in_specs=[pl.BlockSpec((tm, LANES), lambda i: (i, TAIL0 // LANES)),
                  pl.BlockSpec((1, LANES), lambda i: (0, 0))],
        out_specs=pl.BlockSpec((tm, LANES), lambda i: (i, 0)),
        out_shape=jax.ShapeDtypeStruct((T, LANES), f32),
        compiler_params=_cparams(1), name="forget_gate")(h, fb_row)


def _cumsum_kernel(x_ref, o_ref, carry_ref):
    @pl.when(pl.program_id(1) == 0)
    def _():
        carry_ref[...] = jnp.zeros_like(carry_ref)

    bl = x_ref.shape[-1]
    r = lax.broadcasted_iota(jnp.int32, (bl, bl), 0)
    c = lax.broadcasted_iota(jnp.int32, (bl, bl), 1)
    upper = (r <= c).astype(f32)
    y = jnp.dot(x_ref[0], upper, precision=lax.Precision.HIGHEST,
                preferred_element_type=f32) + carry_ref[:, 0:1]
    o_ref[0] = y
    carry_ref[...] = jnp.broadcast_to(y[:, bl - 1:bl], carry_ref.shape)


def cumsum_lanes(x, bl):
    n, r, L = x.shape
    return pl.pallas_call(
        _cumsum_kernel, grid=(n, L // bl),
        in_specs=[pl.BlockSpec((1, r, bl), lambda s, j: (s, 0, j))],
        out_specs=pl.BlockSpec((1, r, bl), lambda s, j: (s, 0, j)),
        out_shape=jax.ShapeDtypeStruct((n, r, L), f32),
        scratch_shapes=[pltpu.VMEM((r, LANES), f32)],
        compiler_params=_cparams(2), name="cumsum")(x)


def _cumsum_rows_kernel(x_ref, o_ref, carry_ref):
    @pl.when(pl.program_id(1) == 0)
    def _():
        carry_ref[...] = jnp.zeros_like(carry_ref)

    bl = x_ref.shape[0]
    r = lax.broadcasted_iota(jnp.int32, (bl, bl), 0)
    c = lax.broadcasted_iota(jnp.int32, (bl, bl), 1)
    lower = (c <= r).astype(f32)
    y = jnp.dot(lower, x_ref[...], precision=lax.Precision.HIGHEST,
                preferred_element_type=f32) + carry_ref[0:1, :]
    o_ref[...] = y
    carry_ref[...] = jnp.broadcast_to(y[bl - 1:bl, :], carry_ref.shape)


def cumsum_rows(x, nseq, L, bl):
    nb = L // bl
    return pl.pallas_call(
        _cumsum_rows_kernel, grid=(nseq, nb),
        in_specs=[pl.BlockSpec((bl, LANES), lambda s, j: (s * nb + j, 0))],
        out_specs=pl.BlockSpec((bl, LANES), lambda s, j: (s * nb + j, 0)),
        out_shape=jax.ShapeDtypeStruct((nseq * L, LANES), f32),
        scratch_shapes=[pltpu.VMEM((SUBLANES, LANES), f32)],
        compiler_params=_cparams(2), name="cumsum_rows")(x)


_BIAS_TERMS = 3


def _split_heads(q, lane):
    zero = jnp.zeros_like(q)
    return jnp.where(lane, q, zero), jnp.where(lane, zero, q)


def _pack_queries(q, lane_i):
    qs = q * (DH_A ** -0.5)
    q0 = jnp.where(lane_i < HALF, qs, jnp.where(lane_i < HALF + _BIAS_TERMS, 1.0, 0.0))
    q1 = jnp.where(lane_i >= HALF, qs, jnp.where(lane_i < _BIAS_TERMS, 1.0, 0.0))
    return q0.astype(bf16), q1.astype(bf16)


def _pack_keys(k, c0, c1, lane_i):
    def one(own, c, base):
        out = jnp.where(own, k, 0.0)
        rest = -c
        for t in range(_BIAS_TERMS):
            term = rest.astype(bf16).astype(f32)
            out = jnp.where(lane_i == base + t, term, out)
            rest = rest - term
        return out.astype(bf16)
    return one(lane_i < HALF, c0, HALF), one(lane_i >= HALF, c1, 0)


def _pack_values(v, lane):
    return jnp.where(lane, v, 1.0).astype(bf16), jnp.where(lane, 1.0, v).astype(bf16)


def _attn_step(qs, ks, vs, carry, mask):
    out = []
    for q, k, v, (m, acc) in zip(qs, ks, vs, carry):
        s = _dot_nt(q, k)
        if mask is not None:
            s = jnp.where(mask, s, NEG_INF)
        mn = jnp.maximum(m, jnp.max(s, -1, keepdims=True))
        p = jnp.exp(s - mn).astype(bf16)
        out.append((mn, acc * jnp.exp(m - mn) + _dot(p, v)))
    return tuple(out)


def _attn_init(tq):
    return tuple((jnp.full((tq, 1), NEG_INF, f32), jnp.zeros((tq, LANES), f32)) for _ in range(2))


def _attn_finish(carry, lane):
    (_, acc0), (_, acc1) = carry
    o0 = acc0 / pltpu.roll(acc0, HALF, 1)
    o1 = acc1 / pltpu.roll(acc1, HALF, 1)
    return jnp.where(lane, o0, o1)


def _fox_prompt_kernel(q_ref, k_ref, v_ref, c_ref, o_ref, k0_ref, k1_ref, v0_ref, v1_ref, *, tq, tk):
    i = pl.program_id(2)
    per = tq // tk
    lane_k = lax.broadcasted_iota(jnp.int32, (tk, LANES), 1)
    lane_q = lax.broadcasted_iota(jnp.int32, (tq, LANES), 1)

    @pl.when(i == 0)
    def _():
        def pack(j, _):
            rows = pl.ds(pl.multiple_of(j * tk, tk), tk)
            c = c_ref[0, 0, rows, :]
            k0, k1 = _pack_keys(k_ref[rows, :], c[:, 0:1], c[:, 1:2], lane_k)
            v0, v1 = _pack_values(v_ref[rows, :], lane_k < HALF)
            k0_ref[rows, :] = k0
            k1_ref[rows, :] = k1
            v0_ref[rows, :] = v0
            v1_ref[rows, :] = v1
            return 0

        lax.fori_loop(0, k_ref.shape[0] // tk, pack, 0)

    qs = _pack_queries(q_ref[...], lane_q)

    def step(j, carry, mask):
        rows = pl.ds(pl.multiple_of(j * tk, tk), tk)
        return _attn_step(qs, (k0_ref[rows, :], k1_ref[rows, :]), (v0_ref[rows, :], v1_ref[rows, :]), carry, mask)

    carry = lax.fori_loop(0, i * per, lambda j, c: step(j, c, None), _attn_init(tq))
    r = lax.broadcasted_iota(jnp.int32, (tq, tk), 0)
    c = lax.broadcasted_iota(jnp.int32, (tq, tk), 1)
    for d in range(per):
        carry = step(i * per + d, carry, c + d * tk <= r)
    o_ref[...] = _attn_finish(carry, lane_q < HALF).astype(o_ref.dtype)


def fox_prompt(h, cp, B, S, tq, tk):
    nq = S // tq
    qb, kb, vb = FQ0 // LANES, FK0 // LANES, FV0 // LANES
    return pl.pallas_call(
        functools.partial(_fox_prompt_kernel, tq=tq, tk=tk), grid=(B, H_A // 2, nq),
        in_specs=[pl.BlockSpec((tq, LANES), lambda b, p, i: (b * nq + i, qb + p)),
                  pl.BlockSpec((S, LANES), lambda b, p, i: (b, kb + p)),
                  pl.BlockSpec((S, LANES), lambda b, p, i: (b, vb + p)),
                  pl.BlockSpec((1, 1, S, 2), lambda b, p, i: (b, p, 0, 0))],
        out_specs=pl.BlockSpec((tq, LANES), lambda b, p, i: (b * nq + i, p)),
        out_shape=jax.ShapeDtypeStruct((B * S, W_A), bf16),
        scratch_shapes=[pltpu.VMEM((S, LANES), bf16) for _ in range(4)],
        compiler_params=_cparams(3, 52), name="fox_prompt")(h, h, h, cp)


def _fox_sample_kernel(q_ref, kn_ref, vn_ref, kc_ref, vc_ref, c_ref, o_ref, *, Ls, P, tk):
    lane = lax.broadcasted_iota(jnp.int32, (Ls, LANES), 1) < HALF
    qs = _pack_queries(q_ref[...], lax.broadcasted_iota(jnp.int32, (Ls, LANES), 1))

    def block(k, v, c, carry, mask):
        lane_i = lax.broadcasted_iota(jnp.int32, k.shape, 1)
        return _attn_step(qs, _pack_keys(k, c[:, 0:1], c[:, 1:2], lane_i), _pack_values(v, lane_i < HALF),
                          carry, mask)

    carry = _attn_init(Ls)
    for j in range(P // tk):
        rows = slice(j * tk, (j + 1) * tk)
        carry = block(kc_ref[0, rows, :], vc_ref[0, rows, :], c_ref[0, 0, rows, :], carry, None)
    r = lax.broadcasted_iota(jnp.int32, (Ls, Ls), 0)
    c = lax.broadcasted_iota(jnp.int32, (Ls, Ls), 1)
    carry = block(kn_ref[...], vn_ref[...], c_ref[0, 0, P:P + Ls, :], carry, c <= r)
    o_ref[...] = _attn_finish(carry, lane).astype(o_ref.dtype)


def fox_sample(h, ck, cv, cp, row0, Bs, Ls, P):
    qb, kb, vb = FQ0 // LANES, FK0 // LANES, FV0 // LANES
    tk = _pick(P, (1024, 512, 256, 128))
    Lp = cp.shape[2]
    return pl.pallas_call(
        functools.partial(_fox_sample_kernel, Ls=Ls, P=P, tk=tk), grid=(Bs, H_A // 2),
        in_specs=[pl.BlockSpec((Ls, LANES), lambda b, p: (row0 + b, qb + p)),
                  pl.BlockSpec((Ls, LANES), lambda b, p: (row0 + b, kb + p)),
                  pl.BlockSpec((Ls, LANES), lambda b, p: (row0 + b, vb + p)),
                  pl.BlockSpec((1, P, LANES), lambda b, p: (b, 0, p)),
                  pl.BlockSpec((1, P, LANES), lambda b, p: (b, 0, p)),
                  pl.BlockSpec((1, 1, Lp, 2), lambda b, p: (b, p, 0, 0))],
        out_specs=pl.BlockSpec((Ls, LANES), lambda b, p: (b, p)),
        out_shape=jax.ShapeDtypeStruct((Bs * Ls, W_A), bf16),
        compiler_params=_cparams(2, 48), name="fox_sample")(h, h, h, ck, cv, cp)


def _pair_mean(x, lane):
    s0 = jnp.sum(jnp.where(lane, x, 0.0), -1, keepdims=True)
    s1 = jnp.sum(jnp.where(lane, 0.0, x), -1, keepdims=True)
    return jnp.where(lane, s0, s1) * (1.0 / HALF)


def _rotary(x, cos, sin_signed, first_half):
    xr = jnp.where(first_half, pltpu.roll(x, LANES - HALF // 2, 1), pltpu.roll(x, HALF // 2, 1))
    return x * cos + xr * sin_signed


def _retention_kernel(q_ref, k_ref, v_ref, g_ref, cos_ref, sin_ref, dec_ref, gq_ref, gk_ref, gl_ref,
                      s0_ref, gn_ref, o_ref, s1_ref, st_ref, *, Lc):
    c = pl.program_id(2)

    @pl.when(c == 0)
    def _():
        st_ref[...] = s0_ref[0, 0]

    lane_i = lax.broadcasted_iota(jnp.int32, (Lc, LANES), 1)
    lane = lane_i < HALF
    first_half = (lane_i % HALF) < (HALF // 2)
    cos, sin = cos_ref[...], sin_ref[...]
    q = _rotary(q_ref[...], cos, sin, first_half)
    k = _rotary(k_ref[...], cos, sin, first_half) * (DK_B ** -0.5)
    qb, kb, vb = q.astype(bf16), k.astype(bf16), v_ref[...].astype(bf16)
    q0, q1 = _split_heads(qb, lane)
    a0 = (_dot_nt(q0, kb) * dec_ref[0]).astype(bf16)
    a1 = (_dot_nt(q1, kb) * dec_ref[1]).astype(bf16)
    intra = jnp.where(lane, _dot(a0, vb), _dot(a1, vb))
    st = st_ref[...]
    cross = _dot(qb, st.astype(bf16)) * gq_ref[0]
    o = intra + cross
    kd = (k * gk_ref[0]).astype(bf16)
    sr = lax.broadcasted_iota(jnp.int32, (LANES, LANES), 0) < HALF
    sc = lax.broadcasted_iota(jnp.int32, (LANES, LANES), 1) < HALF
    st_new = gl_ref[0, 0:1, :] * st + jnp.where(sr == sc, _dot_tn(kd, vb), 0.0)
    st_ref[...] = st_new
    mu = _pair_mean(o, lane)
    d = o - mu
    var = _pair_mean(d * d, lane)
    on = d * lax.rsqrt(var + LN_EPS) * gn_ref[...]
    o_ref[...] = (_silu(g_ref[...]) * on).astype(o_ref.dtype)

    @pl.when(c == pl.num_programs(2) - 1)
    def _():
        s1_ref[0, 0] = st_new


def _retention_tables(pos, Lc):
    half = DK_B // 2
    inv = ROPE_BASE ** (-jnp.arange(half, dtype=f32) / half)
    ang = pos.astype(f32)[:, None] * inv[None, :]
    cos, sin = jnp.cos(ang), jnp.sin(ang)
    cos_t = jnp.tile(cos, (1, 4))
    sin_t = jnp.tile(jnp.concatenate([-sin, sin], -1), (1, 2))
    lg = jnp.log1p(-jnp.exp2(-5.0 - jnp.arange(H_B, dtype=f32)))
    i = jnp.arange(Lc, dtype=f32)
    diff = i[:, None] - i[None, :]
    dec = jnp.exp(jnp.where((diff >= 0)[None], diff[None] * lg[:, None, None], NEG_INF))
    pair = lambda t: jnp.repeat(t.reshape(t.shape[0], H_B // 2, 2), HALF, axis=-1)
    gq = pair(jnp.exp((i[:, None] + 1.0) * lg[None, :])).transpose(1, 0, 2)
    gk = pair(jnp.exp((Lc - 1.0 - i)[:, None] * lg[None, :])).transpose(1, 0, 2)
    gl = jnp.broadcast_to(pair(jnp.exp(Lc * lg)[None, :]).transpose(1, 0, 2), (H_B // 2, SUBLANES, LANES))
    return cos_t, sin_t, dec, gq, gk, gl


def retention(h, gn_w, s0, pos, row0, nseq, L, Lc):
    nch = L // Lc
    cos_t, sin_t, dec, gq, gk, gl = _retention_tables(pos, Lc)
    blk = lambda col0: pl.BlockSpec((Lc, LANES), lambda s, p, c: (row0 + s * nch + c, col0 // LANES + p))
    tab = lambda: pl.BlockSpec((1, Lc, LANES), lambda s, p, c: (p, 0, 0))
    return pl.pallas_call(
        functools.partial(_retention_kernel, Lc=Lc), grid=(nseq, H_B // 2, nch),
        in_specs=[blk(RQ0), blk(RK0), blk(RV0), blk(RG0),
                  pl.BlockSpec((Lc, LANES), lambda s, p, c: (c, 0)),
                  pl.BlockSpec((Lc, LANES), lambda s, p, c: (c, 0)),
                  pl.BlockSpec((2, Lc, Lc), lambda s, p, c: (p, 0, 0)),
                  tab(), tab(),
                  pl.BlockSpec((1, SUBLANES, LANES), lambda s, p, c: (p, 0, 0)),
                  pl.BlockSpec((1, 1, LANES, LANES), lambda s, p, c: (s, p, 0, 0)),
                  pl.BlockSpec((1, LANES), lambda s, p, c: (0, p))],
        out_specs=[pl.BlockSpec((Lc, LANES), lambda s, p, c: (s * nch + c, p)),
                   pl.BlockSpec((1, 1, LANES, LANES), lambda s, p, c: (s, p, 0, 0))],
        out_shape=[jax.ShapeDtypeStruct((nseq * L, W_B), bf16),
                   jax.ShapeDtypeStruct((nseq, H_B // 2, LANES, LANES), f32)],
        scratch_shapes=[pltpu.VMEM((LANES, LANES), f32)],
        compiler_params=_cparams(3, 32), name="retention")(
            h, h, h, h, cos_t, sin_t, dec, gq, gk, gl, s0, gn_w.reshape(1, W_B))


def _pack_ret_state(s):
    n = s.shape[0]
    s = s.reshape(n, H_B // 2, 2, DK_B, DK_B)
    z = jnp.zeros_like(s[:, :, 0])
    top = jnp.concatenate([s[:, :, 0], z], -1)
    bot = jnp.concatenate([z, s[:, :, 1]], -1)
    return jnp.concatenate([top, bot], -2)


def _unpack_ret_state(s):
    n = s.shape[0]
    return jnp.stack([s[:, :, :HALF, :HALF], s[:, :, HALF:, HALF:]], 2).reshape(n, H_B, DK_B, DK_B)


def _ssd_kernel(xbc_ref, z_ref, t_ref, cw_ref, cb_ref, dtb_ref, alog_ref, dsk_ref, nw_ref, hist_ref, h0_ref,
                o_ref, h1_ref, xpad_ref, hs_ref, *, Lc):
    c = pl.program_id(1)

    @pl.when(c == 0)
    def _():
        xpad_ref[0:SUBLANES, :] = hist_ref[0]
        hs_ref[...] = h0_ref[0]

    xpad_ref[SUBLANES:SUBLANES + Lc, :] = xbc_ref[...]
    conv = cb_ref[...]
    for t in range(CONV_W):
        r0 = SUBLANES - (CONV_W - 1) + t
        conv = conv + xpad_ref[r0:r0 + Lc, :] * cw_ref[t:t + 1, :]
    xpad_ref[0:SUBLANES, :] = xpad_ref[Lc:Lc + SUBLANES, :]
    xc = _silu(conv)
    xs = xc[:, :W_C]
    bm = [xc[:, W_C + g * N_C:W_C + (g + 1) * N_C].astype(bf16) for g in range(G_C)]
    cm = [xc[:, W_C + (G_C + g) * N_C:W_C + (G_C + g + 1) * N_C].astype(bf16) for g in range(G_C)]

    lane_i = lax.broadcasted_iota(jnp.int32, (Lc, LANES), 1)
    lane = lane_i < HALF
    dt_valid = (lane_i >= DT_LANE0) & (lane_i < DT_LANE0 + H_C)
    dt = _softplus(t_ref[...] + dtb_ref[...])
    dta = jnp.where(dt_valid, dt * (-jnp.exp(alog_ref[...])), 0.0)
    r = lax.broadcasted_iota(jnp.int32, (Lc, Lc), 0)
    cidx = lax.broadcasted_iota(jnp.int32, (Lc, Lc), 1)
    tri = cidx <= r
    a_cs = jnp.dot(tri.astype(f32), dta, precision=lax.Precision.HIGHEST, preferred_element_type=f32)
    a_cs_t = a_cs.T
    dt_t = dt.T
    row_first = lax.broadcasted_iota(jnp.int32, (LANES, LANES), 0) < HALF

    ys = []
    for p in range(H_C // 2):
        g = (2 * p) // (H_C // G_C)
        xpair = xs[:, p * LANES:(p + 1) * LANES]
        xpair_b = xpair.astype(bf16)
        if p % (H_C // G_C // 2) == 0:
            cb = _dot_nt(cm[g], bm[g])
        yh, acol, dcol = [], [], []
        for hd in (2 * p, 2 * p + 1):
            li = DT_LANE0 + hd
            ac = a_cs[:, li:li + 1]
            seg = ac - a_cs_t[li:li + 1, :]
            w = cb * jnp.exp(jnp.where(tri, seg, NEG_INF)) * dt_t[li:li + 1, :]
            yh.append(_dot(w.astype(bf16), xpair_b))
            acol.append(ac)
            dcol.append(dt[:, li:li + 1])
        acs_pair = jnp.where(lane, acol[0], acol[1])
        dt_pair = jnp.where(lane, dcol[0], dcol[1])
        hs = hs_ref[p]
        y = jnp.where(lane, yh[0], yh[1]) + _dot_nt(cm[g], hs.astype(bf16)) * jnp.exp(acs_pair)
        ys.append(y)
        a_last = acs_pair[Lc - 1:Lc, :]
        to_end = jnp.exp(a_last - acs_pair) * dt_pair
        upd = _dot_tn((xpair * to_end).astype(bf16), bm[g])
        sdec = jnp.exp(jnp.where(row_first, acol[0][Lc - 1:Lc, :], acol[1][Lc - 1:Lc, :]))
        hs_ref[p] = sdec * hs + upd

    y = jnp.concatenate(ys, axis=1) + dsk_ref[...] * xs
    y = y * _silu(z_ref[...])
    y = y * lax.rsqrt(jnp.mean(y * y, -1, keepdims=True) + LN_EPS) * nw_ref[...]
    o_ref[...] = y.astype(o_ref.dtype)

    @pl.when(c == pl.num_programs(1) - 1)
    def _():
        h1_ref[0] = hs_ref[...]


def ssd(h, conv_w, conv_b, dt_bias, a_log, d_skip, norm_w, hist, h0, row0, nseq, L, Lc):
    nch = L // Lc
    lane_row = lambda v: jnp.zeros((1, LANES), f32).at[0, DT_LANE0:DT_LANE0 + H_C].set(v)
    cw = jnp.zeros((SUBLANES, CONV_DIM), f32).at[:CONV_W].set(conv_w)
    row = lambda w: pl.BlockSpec((1, w), lambda s, c: (0, 0))
    return pl.pallas_call(
        functools.partial(_ssd_kernel, Lc=Lc), grid=(nseq, nch),
        in_specs=[pl.BlockSpec((Lc, CONV_DIM), lambda s, c: (row0 + s * nch + c, XBC0 // CONV_DIM)),
                  pl.BlockSpec((Lc, W_C), lambda s, c: (row0 + s * nch + c, Z0 // W_C)),
                  pl.BlockSpec((Lc, LANES), lambda s, c: (row0 + s * nch + c, TAIL0 // LANES)),
                  pl.BlockSpec((SUBLANES, CONV_DIM), lambda s, c: (0, 0)),
                  row(CONV_DIM), row(LANES), row(LANES), row(W_C), row(W_C),
                  pl.BlockSpec((1, SUBLANES, CONV_DIM), lambda s, c: (s, 0, 0)),
                  pl.BlockSpec((1, H_C // 2, LANES, LANES), lambda s, c: (s, 0, 0, 0))],
        out_specs=[pl.BlockSpec((Lc, W_C), lambda s, c: (s * nch + c, 0)),
                   pl.BlockSpec((1, H_C // 2, LANES, LANES), lambda s, c: (s, 0, 0, 0))],
        out_shape=[jax.ShapeDtypeStruct((nseq * L, W_C), bf16),
                   jax.ShapeDtypeStruct((nseq, H_C // 2, LANES, LANES), f32)],
        scratch_shapes=[pltpu.VMEM((Lc + SUBLANES, CONV_DIM), f32),
                        pltpu.VMEM((H_C // 2, LANES, LANES), f32)],
        compiler_params=_cparams(2, 48), name="ssd")(
            h, h, h, cw, conv_b.reshape(1, CONV_DIM), lane_row(dt_bias), lane_row(a_log),
            jnp.repeat(d_skip, P_C).reshape(1, W_C), norm_w.reshape(1, W_C), hist, h0)


def _outproj_kernel(x_ref, a_ref, b_ref, c_ref, w_ref, g_ref, beta_ref, o_ref):
    mix = (_dot(a_ref[...], w_ref[0:W_A, :]) + _dot(b_ref[...], w_ref[W_A:W_A + W_B, :])
           + _dot(c_ref[...], w_ref[W_A + W_B:, :]))
    o_ref[...] = _ln(ALPHA * x_ref[...] + mix, g_ref[...], beta_ref[...])


def outproj_ln(x, ao, bo, co, w, g, b, tm):
    T, D = x.shape
    rows = lambda wd: pl.BlockSpec((tm, wd), lambda i: (i, 0))
    const = lambda s: pl.BlockSpec(s, lambda i: (0, 0))
    return pl.pallas_call(
        _outproj_kernel, grid=(T // tm,),
        in_specs=[rows(D), rows(W_A), rows(W_B), rows(W_C), const(w.shape), const((1, D)), const((1, D))],
        out_specs=rows(D),
        out_shape=jax.ShapeDtypeStruct((T, D), f32),
        compiler_params=_cparams(1, 52), name="outproj_ln")(x, ao, bo, co, w, g.reshape(1, D), b.reshape(1, D))


def _mem_body(x_ref, wq_ref, mk_ref, mv_ref, wo_ref, g_ref, b_ref, o_ref, ob_ref):
    x = x_ref[...]
    q = _dot(x.astype(bf16), wq_ref[...])
    outs = []
    for hd in range(H_M):
        sl = slice(hd * DH_M, (hd + 1) * DH_M)
        s = _dot_nt(q[:, sl].astype(bf16), mk_ref[0, :, sl].astype(bf16)) * (DH_M ** -0.5)
        p = jnp.exp(s - jnp.max(s, -1, keepdims=True))
        p = p / jnp.sum(p, -1, keepdims=True)
        outs.append(_dot(p.astype(bf16), mv_ref[0, :, sl].astype(bf16)))
    o = jnp.concatenate(outs, axis=1).astype(bf16)
    y = _ln(ALPHA * x + _dot(o, wo_ref[...]), g_ref[...], b_ref[...])
    o_ref[...] = y
    ob_ref[...] = y.astype(bf16)


def _mem_kernel_first(x_ref, wq_ref, mk_ref, mv_ref, wo_ref, g_ref, b_ref, o_ref, ob_ref):
    _mem_body(x_ref, wq_ref, mk_ref, mv_ref, wo_ref, g_ref, b_ref, o_ref, ob_ref)


def _mem_kernel_into(x_ref, wq_ref, mk_ref, mv_ref, wo_ref, g_ref, b_ref, prev_ref, prevb_ref, o_ref, ob_ref):
    del prev_ref, prevb_ref
    _mem_body(x_ref, wq_ref, mk_ref, mv_ref, wo_ref, g_ref, b_ref, o_ref, ob_ref)


def mem_attn_ln(x, wq, mk, mv, wo, g, b, row0, nseq, L, tr, into=None):
    T, D = x.shape
    per = L // tr
    nm = mk.shape[1]
    const = lambda s: pl.BlockSpec(s, lambda i: (0,) * len(s))
    in_specs = [pl.BlockSpec((tr, D), lambda i: (row0 + i, 0)), const(wq.shape),
                pl.BlockSpec((1, nm, W_M), lambda i: (i // per, 0, 0)),
                pl.BlockSpec((1, nm, W_M), lambda i: (i // per, 0, 0)),
                const(wo.shape), const((1, D)), const((1, D))]
    args = [x, wq, mk, mv, wo, g.reshape(1, D), b.reshape(1, D)]
    aliases = {}
    if into is not None:
        in_specs += [pl.BlockSpec(memory_space=pl.ANY), pl.BlockSpec(memory_space=pl.ANY)]
        aliases = {len(args): 0, len(args) + 1: 1}
        args += list(into)
    return pl.pallas_call(
        _mem_kernel_first if into is None else _mem_kernel_into, grid=(nseq * per,),
        in_specs=in_specs,
        out_specs=[pl.BlockSpec((tr, D), lambda i: (row0 + i, 0)), pl.BlockSpec((tr, D), lambda i: (row0 + i, 0))],
        out_shape=[jax.ShapeDtypeStruct((T, D), f32), jax.ShapeDtypeStruct((T, D), bf16)],
        input_output_aliases=aliases,
        compiler_params=_cparams(1, 48), name="mem_attn_ln")(*args)


def _peer_score_kernel(x_ref, wq_ref, k1_ref, k2_ref, o_ref):
    q = _dot(x_ref[...].astype(bf16), wq_ref[...])
    k1 = k1_ref[...].astype(bf16)
    k2 = k2_ref[...].astype(bf16)
    hk = PEER_DK // 2
    for hd in range(PEER_HEADS):
        q1 = q[:, hd * PEER_DK:hd * PEER_DK + hk].astype(bf16)
        q2 = q[:, hd * PEER_DK + hk:(hd + 1) * PEER_DK].astype(bf16)
        o_ref[hd, 0:PEER_NK, :] = _dot_nt(k1, q1)
        o_ref[hd, PEER_NK:2 * PEER_NK, :] = _dot_nt(k2, q2)


def peer_scores(x, wq, k1, k2, tm):
    T, D = x.shape
    const = lambda s: pl.BlockSpec(s, lambda i: (0, 0))
    return pl.pallas_call(
        _peer_score_kernel, grid=(T // tm,),
        in_specs=[pl.BlockSpec((tm, D), lambda i: (i, 0)), const(wq.shape), const(k1.shape), const(k2.shape)],
        out_specs=pl.BlockSpec((PEER_HEADS, 2 * PEER_NK, tm), lambda i: (0, 0, i)),
        out_shape=jax.ShapeDtypeStruct((PEER_HEADS, 2 * PEER_NK, T), f32),
        compiler_params=_cparams(1, 52), name="peer_scores")(x, wq, k1, k2)


_CAND_ROWS = PEER_TOPK + 7 * SUBLANES + SUBLANES


def _cand_flat_index(cw):
    idx = [0 * PEER_TOPK + b for b in range(PEER_TOPK)]
    for a in range(1, 8):
        idx += [a * PEER_TOPK + b for b in range(SUBLANES)]
    idx += [a * PEER_TOPK for a in range(8, PEER_TOPK)]
    return np.broadcast_to(np.asarray(idx, np.float32)[:, None], (_CAND_ROWS, cw)).copy()


def _extract_top(s, key_idx, n):
    rank = jnp.full(s.shape, float(n), f32)
    vals = []
    for a in range(n):
        mx = jnp.max(s, axis=0, keepdims=True)
        first = jnp.min(jnp.where(s == mx, key_idx, float(1 << 20)), axis=0, keepdims=True)
        sel = key_idx == first
        rank = jnp.where(sel, float(a), rank)
        s = jnp.where(sel, NEG_INF, s)
        vals.append(mx)
    return vals, rank


def _top_values(s, n, with_rank):
    rank = jnp.full(s.shape, float(n), f32) if with_rank else None
    vals = []
    for a in range(n):
        mx = jnp.max(s, axis=0, keepdims=True)
        hit = s == mx
        if with_rank:
            rank = jnp.where(hit, float(a), rank)
        s = jnp.where(hit, NEG_INF, s)
        vals.append(mx)
    return vals, rank


def _candidates(v1, v2):
    v1a = jnp.concatenate(v1, axis=0)
    v2a = jnp.concatenate(v2, axis=0)
    return jnp.concatenate([v1[0] + v2a] + [v1[a] + v2a[0:SUBLANES] for a in range(1, 8)]
                           + [v1a[SUBLANES:] + v2[0]], axis=0)


def _counts_per_rank(picked):
    cnt_a = [jnp.sum(picked[0:PEER_TOPK], axis=0, keepdims=True)]
    for a in range(1, 8):
        lo = PEER_TOPK + (a - 1) * SUBLANES
        cnt_a.append(jnp.sum(picked[lo:lo + SUBLANES], axis=0, keepdims=True))
    tail = picked[PEER_TOPK + 7 * SUBLANES:]
    return cnt_a + [tail[a:a + 1] for a in range(SUBLANES)]


def _select_fast(s1, s2):
    v1, _ = _top_values(s1, PEER_TOPK, False)
    v2, rank2 = _top_values(s2, PEER_TOPK, True)
    cand = _candidates(v1, v2)
    vc, _ = _top_values(cand, PEER_TOPK, False)
    top = v1[0] + v2[0]
    zsum = jnp.exp(vc[0] - top)
    for k in range(1, PEER_TOPK):
        zsum = zsum + jnp.exp(vc[k] - top)
    picked = jnp.where(cand >= vc[PEER_TOPK - 1], 1.0, 0.0)
    cnt_a = _counts_per_rank(picked)
    cnt = jnp.zeros(s1.shape, f32)
    for a in range(PEER_TOPK):
        cnt = jnp.where(s1 == v1[a], cnt_a[a], cnt)
    n1 = jnp.sum(jnp.where(s1 >= v1[PEER_TOPK - 1], 1.0, 0.0), axis=0, keepdims=True)
    n2 = jnp.sum(jnp.where(rank2 < float(PEER_TOPK), 1.0, 0.0), axis=0, keepdims=True)
    nc = jnp.sum(picked, axis=0, keepdims=True)
    k = float(PEER_TOPK)
    distinct = (n1 == k) & (n2 == k) & (nc == k)
    tie = jnp.max(jnp.where(distinct, 0.0, 1.0)) > 0.0
    return rank2, cnt, zsum, v1[0], v2[0], tie


def _select_exact(s1, s2, key_idx, cidx):
    v1, rank1 = _extract_top(s1, key_idx, PEER_TOPK)
    v2, rank2 = _extract_top(s2, key_idx, PEER_TOPK)
    cand = _candidates(v1, v2)
    top = v1[0] + v2[0]
    picked = jnp.zeros(cand.shape, f32)
    zsum = jnp.zeros((1, s1.shape[1]), f32)
    for _k in range(PEER_TOPK):
        mx = jnp.max(cand, axis=0, keepdims=True)
        first = jnp.min(jnp.where(cand == mx, cidx, float(1 << 20)), axis=0, keepdims=True)
        sel = cidx == first
        picked = jnp.where(sel, 1.0, picked)
        cand = jnp.where(sel, NEG_INF, cand)
        zsum = zsum + jnp.exp(mx - top)
    cnt_a = _counts_per_rank(picked)
    cnt = jnp.zeros(s1.shape, f32)
    for a in range(PEER_TOPK):
        cnt = jnp.where(rank1 == float(a), cnt_a[a], cnt)
    return rank2, cnt, zsum


def _peer_topk_kernel(s_ref, cidx_ref, r2_ref, g2_ref, cnt_ref, g1_ref, *, tb, cw):
    nj = PEER_NK // SUBLANES

    def chunk(ci, _):
        off = pl.multiple_of(ci * cw, cw)
        s1 = s_ref[0, 0:PEER_NK, pl.ds(off, cw)]
        s2 = s_ref[0, PEER_NK:2 * PEER_NK, pl.ds(off, cw)]
        rank2, cnt, zsum, m1, m2, tie = _select_fast(s1, s2)
        e1 = jnp.exp(s1 - m1)
        r2_ref[0, :, pl.ds(off, cw)] = rank2.astype(bf16)
        g2_ref[0, :, pl.ds(off, cw)] = jnp.exp(s2 - m2).astype(bf16)
        cnt_ref[:, 0, :, pl.ds(off, cw)] = cnt.reshape(nj, SUBLANES, cw)
        g1_ref[:, 0, :, pl.ds(off, cw)] = (e1 / zsum).reshape(nj, SUBLANES, cw)

        @pl.when(tie)
        def _():
            key_idx = lax.broadcasted_iota(jnp.int32, (PEER_NK, cw), 0).astype(f32)
            rank2x, cntx, zsumx = _select_exact(s1, s2, key_idx, cidx_ref[...])
            r2_ref[0, :, pl.ds(off, cw)] = rank2x.astype(bf16)
            cnt_ref[:, 0, :, pl.ds(off, cw)] = cntx.reshape(nj, SUBLANES, cw)
            g1_ref[:, 0, :, pl.ds(off, cw)] = (e1 / zsumx).reshape(nj, SUBLANES, cw)

        return 0

    lax.fori_loop(0, tb // cw, chunk, 0)


def peer_topk(sT, tb):
    T = sT.shape[-1]
    nj = PEER_NK // SUBLANES
    cw = _pick(tb, (2 * LANES, LANES))
    per_key = lambda: pl.BlockSpec((1, PEER_NK, tb), lambda hd, i: (hd, 0, i))
    per_row = lambda: pl.BlockSpec((nj, 1, SUBLANES, tb), lambda hd, i: (0, hd, 0, i))
    r2, g2, cnt, g1 = pl.pallas_call(
        functools.partial(_peer_topk_kernel, tb=tb, cw=cw), grid=(PEER_HEADS, T // tb),
        in_specs=[pl.BlockSpec((1, 2 * PEER_NK, tb), lambda hd, i: (hd, 0, i)),
                  pl.BlockSpec((_CAND_ROWS, cw), lambda hd, i: (0, 0))],
        out_specs=[per_key(), per_key(), per_row(), per_row()],
        out_shape=[jax.ShapeDtypeStruct((PEER_HEADS, PEER_NK, T), bf16),
                   jax.ShapeDtypeStruct((PEER_HEADS, PEER_NK, T), bf16),
                   jax.ShapeDtypeStruct((nj, PEER_HEADS, SUBLANES, T), f32),
                   jax.ShapeDtypeStruct((nj, PEER_HEADS, SUBLANES, T), f32)],
        compiler_params=_cparams(2, 32), name="peer_topk")(sT, jnp.asarray(_cand_flat_index(cw)))
    packed = (PEER_HEADS, PEER_NK // BF16_ROWS, BF16_ROWS, T)
    return (r2.reshape(packed), g2.reshape(packed),
            cnt.reshape(nj, PEER_HEADS * SUBLANES, T), g1.reshape(nj, PEER_HEADS * SUBLANES, T))


def _peer_main_kernel(x_ref, u_ref, vt_ref, r2_ref, g2_ref, cnt_ref, g1_ref, o_ref):
    j = pl.program_id(1)
    tm = x_ref.shape[0]

    @pl.when(j == 0)
    def _():
        o_ref[...] = jnp.zeros_like(o_ref)

    a = _dot_nt(u_ref[...], x_ref[...])
    act = (0.5 * a * (1.0 + lax.erf(a * (2.0 ** -0.5)))).astype(bf16)
    zero = jnp.zeros((), bf16)
    parts = []
    for r in range(SUBLANES):
        w = None
        for hd in range(PEER_HEADS):
            row = hd * SUBLANES + r
            cnt = jnp.broadcast_to(cnt_ref[0, row:row + 1, :], (BF16_ROWS, tm)).astype(bf16)[None]
            g1 = jnp.broadcast_to(g1_ref[0, row:row + 1, :], (BF16_ROWS, tm)).astype(bf16)[None]
            sel = jnp.where(r2_ref[hd] < cnt, g2_ref[hd], zero) * g1
            w = sel if w is None else w + sel
        parts.append(w.reshape(PEER_NK, tm) * act[r * PEER_NK:(r + 1) * PEER_NK])
    o_ref[...] += _dot(vt_ref[...], jnp.concatenate(parts, axis=0))


def peer_main(xb, ub, vtb, layer, r2, g2, cnt, g1, tm):
    T, D = xb.shape
    te = SUBLANES * PEER_NK
    nj = PEER_NK // SUBLANES
    once = pl.Buffered(1)
    per_tok = lambda: pl.BlockSpec((PEER_HEADS, PEER_NK // BF16_ROWS, BF16_ROWS, tm),
                                   lambda i, j: (0, 0, 0, i), pipeline_mode=once)
    per_row = lambda: pl.BlockSpec((1, PEER_HEADS * SUBLANES, tm), lambda i, j: (j, 0, i))
    return pl.pallas_call(
        _peer_main_kernel, grid=(T // tm, nj),
        in_specs=[pl.BlockSpec((tm, D), lambda i, j: (i, 0), pipeline_mode=once),
                  pl.BlockSpec((None, te, D), lambda i, j: (layer, j, 0)),
                  pl.BlockSpec((None, D, te), lambda i, j: (layer, 0, j)),
                  per_tok(), per_tok(), per_row(), per_row()],
        out_specs=pl.BlockSpec((D, tm), lambda i, j: (0, i), pipeline_mode=once),
        out_shape=jax.ShapeDtypeStruct((D, T), f32),
        compiler_params=_cparams(2, 58), name="peer_main")(xb, ub, vtb, r2, g2, cnt, g1)


def _rearrange_w_in(w):
    o = np.cumsum([0, W_A, W_A, W_A, H_A, H_B * DK_B, H_B * DK_B, W_B, W_B, W_C, CONV_DIM, H_C])
    fq, fk, fv, ff, rq, rk, rv, rg, z, xbc, dtr = [slice(int(o[i]), int(o[i + 1])) for i in range(11)]
    parts = [w[:, s] for s in (xbc, fq, z, fk, fv, rq, rk, rv, rg, ff, dtr)]
    pad = NH - TAIL0 - H_A - H_C
    return jnp.concatenate(parts + [jnp.zeros((w.shape[0], pad), w.dtype)], axis=1).astype(bf16)


def kernel(x_prompt, x_sample, cache_fox_k, cache_fox_v, cache_fox_logf, state_ret, state_ssm, state_conv, cache_mem_k, cache_mem_v, mem_prompt, ln_in_g, ln_in_b, w_in, fox_fb, ret_gn_w, conv_w, conv_b, dt_bias, a_log, d_skip, ssm_norm_w, w_out, ln1_g, ln1_b, wq_mem, wkv_mem, wo_mem, ln2_g, ln2_b, peer_wq, peer_k1, peer_k2, peer_u, peer_v, ln3_g, ln3_b):
    B, S, D = x_prompt.shape
    Bs, Ls, _ = x_sample.shape
    P = cache_fox_k.shape[2]
    NM = mem_prompt.shape[1]
    Tp, Ts = B * S, Bs * Ls
    T = Tp + Ts
    assert Tp % Ls == 0 and T % LANES == 0 and Ls >= CONV_W - 1

    tm = _pick(T, (640, 768, 384, 256, 128))
    tm_peer = _pick(T, (1280, 768, 256, 128))
    tq = _pick(S, (1024, 512, 256, 128))
    tk = min(tq, 512)
    lc = _pick(S, (256, 128, 64))
    tr_mem = _pick(S, (512, 256, 128))
    cs_bl = 512
    tn_in = _pick(NH, (1280, 640))

    x = jnp.concatenate([x_prompt.reshape(Tp, D), x_sample.reshape(Ts, D)], axis=0)
    x = layer_norm_rows(x, ln_in_g, ln_in_b, tm)

    pos_p = jnp.arange(S)
    pos_s = P + jnp.arange(Ls)
    zeros_ret = jnp.zeros((B, H_B // 2, LANES, LANES), f32)
    zeros_ssm = jnp.zeros((B, H_C // 2, LANES, LANES), f32)
    zeros_hist = jnp.zeros((B, SUBLANES, CONV_DIM), f32)
    lp_s = -(-(P + Ls) // cs_bl) * cs_bl

    w_in_b = jax.vmap(_rearrange_w_in)(w_in)
    peer_u_b = peer_u.astype(bf16)
    peer_vt_b = peer_v.transpose(0, 2, 1).astype(bf16)

    st_p = [[] for _ in range(8)]
    st_s = [[] for _ in range(6)]
    for l in range(DEPTH):
        h = matmul(x, w_in_b, tm, tn_in, "in_proj", layer=l)
        fb_row = jnp.zeros((1, LANES), f32).at[0, FF_LANE0:FF_LANE0 + H_A].set(fox_fb[l])
        logf_pad = forget_gate(h, fb_row, tm)
        logf = logf_pad[:, FF_LANE0:FF_LANE0 + H_A]
        logf_p = logf[:Tp].reshape(B, S, H_A)
        logf_s = logf[Tp:].reshape(Bs, Ls, H_A)

        c_p = cumsum_rows(logf_pad, B, S, min(cs_bl, S))[:, FF_LANE0:FF_LANE0 + H_A]
        c_p = c_p.reshape(B, S, H_A // 2, 2).transpose(0, 2, 1, 3)
        ao_p = fox_prompt(h, c_p, B, S, tq, tk)
        lf_all = jnp.concatenate([cache_fox_logf[l].astype(f32), logf_s], axis=1).transpose(0, 2, 1)
        lf_all = jnp.pad(lf_all, ((0, 0), (0, 0), (0, lp_s - (P + Ls))))
        c_s = cumsum_lanes(lf_all, cs_bl)[:, :, :P + Ls].reshape(Bs, H_A // 2, 2, P + Ls).transpose(0, 1, 3, 2)
        ao_s = fox_sample(h, cache_fox_k[l].reshape(Bs, P, W_A), cache_fox_v[l].reshape(Bs, P, W_A), c_s,
                          Tp // Ls, Bs, Ls, P)

        bo_p, ret_p = retention(h, ret_gn_w[l], zeros_ret, pos_p, 0, B, S, lc)
        bo_s, ret_s = retention(h, ret_gn_w[l], _pack_ret_state(state_ret[l].astype(f32)), pos_s,
                                Tp // Ls, Bs, Ls, Ls)

        ssd_prm = (conv_w[l], conv_b[l], dt_bias[l], a_log[l], d_skip[l], ssm_norm_w[l])
        co_p, ssm_p = ssd(h, *ssd_prm, zeros_hist, zeros_ssm, 0, B, S, lc)
        hist_s = jnp.pad(state_conv[l].astype(f32), ((0, 0), (SUBLANES - (CONV_W - 1), 0), (0, 0)))
        co_s, ssm_s = ssd(h, *ssd_prm, hist_s, state_ssm[l].astype(f32).reshape(Bs, H_C // 2, LANES, LANES),
                          Tp // Ls, Bs, Ls, Ls)

        cat = lambda a, b: jnp.concatenate([a, b], axis=0)
        x = outproj_ln(x, cat(ao_p, ao_s), cat(bo_p, bo_s), cat(co_p, co_s), w_out[l].astype(bf16),
                       ln1_g[l], ln1_b[l], tm)

        mkv = matmul(mem_prompt.reshape(B * NM, D), wkv_mem[l].astype(bf16), _pick(B * NM, (512, 256, 128)),
                     _pick(2 * W_M, (512, 256)), "mem_kv")
        mk_p = mkv[:, :W_M].reshape(B, NM, W_M)
        mv_p = mkv[:, W_M:].reshape(B, NM, W_M)
        wq_b, wo_b = wq_mem[l].astype(bf16), wo_mem[l].astype(bf16)
        xx = mem_attn_ln(x, wq_b, mk_p, mv_p, wo_b, ln2_g[l], ln2_b[l], 0, B, S, tr_mem)
        x, xb = mem_attn_ln(x, wq_b, cache_mem_k[l].reshape(Bs, NM, W_M), cache_mem_v[l].reshape(Bs, NM, W_M),
                            wo_b, ln2_g[l], ln2_b[l], Tp // Ls, Bs, Ls, Ls, into=xx)

        sT = peer_scores(x, peer_wq[l].astype(bf16), peer_k1[l], peer_k2[l], tm)
        r2, g2, cnt, g1 = peer_topk(sT, tm_peer)
        pe_t = peer_main(xb, peer_u_b, peer_vt_b, l, r2, g2, cnt, g1, tm_peer)
        x = ln_residual_t(x, pe_t, ln3_g[l], ln3_b[l], tm)

        hs = h[Tp:].reshape(Bs, Ls, NH)
        conv_p = jnp.stack([h[(b + 1) * S - (CONV_W - 1):(b + 1) * S, XBC0:XBC0 + CONV_DIM] for b in range(B)])
        new_p = (h[:Tp, FK0:FK0 + W_A].reshape(B, S, H_A, DH_A), h[:Tp, FV0:FV0 + W_A].reshape(B, S, H_A, DH_A),
                 logf_p, _unpack_ret_state(ret_p), ssm_p.reshape(B, H_C, P_C, N_C), conv_p,
                 mk_p.reshape(B, NM, H_M, DH_M), mv_p.reshape(B, NM, H_M, DH_M))
        new_s = (hs[..., FK0:FK0 + W_A].reshape(Bs, Ls, H_A, DH_A), hs[..., FV0:FV0 + W_A].reshape(Bs, Ls, H_A, DH_A),
                 logf_s, _unpack_ret_state(ret_s), ssm_s.reshape(Bs, H_C, P_C, N_C),
                 hs[:, Ls - (CONV_W - 1):, XBC0:XBC0 + CONV_DIM])
        for j, a in enumerate(new_p):
            st_p[j].append(a)
        for j, a in enumerate(new_s):
            st_s[j].append(a)

    outs_p = [jnp.stack(a) for a in st_p]
    outs_s = [jnp.stack(a) for a in st_s]
    return (x[:Tp].reshape(B, S, D), x[Tp:].reshape(Bs, Ls, D), *outs_p, *outs_s)
```

```python
import functools
import math

import numpy as np
import jax
import jax.numpy as jnp
from jax import lax
from jax.experimental import pallas as pl
from jax.experimental.pallas import tpu as pltpu

f32 = jnp.float32
bf16 = jnp.bfloat16
NEG_INF = float("-inf")

LN_EPS = 1e-5
DEPTH = 4
ALPHA = (2 * DEPTH) ** 0.25
H_A, DH_A = 8, 64
H_B, DK_B = 8, 64
H_C, P_C, N_C, G_C = 16, 64, 128, 2
W_A, W_B, W_C = 512, 512, 1024
CONV_W = 4
CONV_DIM = W_C + 2 * G_C * N_C
ROPE_BASE = 10000.0
H_M, DH_M = 4, 128
W_M = H_M * DH_M
PEER_NK, PEER_HEADS, PEER_DK, PEER_TOPK = 128, 8, 256, 16

LANES = 128
SUBLANES = 8
BF16_ROWS = 16
HALF = 64

XBC0, FQ0, Z0, FK0, FV0, RQ0, RK0, RV0, RG0, TAIL0, NH = 0, 1536, 2048, 3072, 3584, 4096, 4608, 5120, 5632, 6144, 6400
FF_LANE0, DT_LANE0 = 0, 8


def _cparams(n_axes, vmem_mb=None):
    kw = dict(dimension_semantics=("arbitrary",) * n_axes)
    if vmem_mb is not None:
        kw["vmem_limit_bytes"] = vmem_mb << 20
    return pltpu.CompilerParams(**kw)


def _pick(n, prefs):
    for p in prefs:
        if n % p == 0:
            return p
    return n


def _dot(a, b):
    return jnp.dot(a, b, preferred_element_type=f32)


def _dot_nt(a, b):
    return lax.dot_general(a, b, (((1,), (1,)), ((), ())), preferred_element_type=f32)


def _dot_tn(a, b):
    return lax.dot_general(a, b, (((0,), (0,)), ((), ())), preferred_element_type=f32)


def _ln(x, g, b):
    mu = jnp.mean(x, -1, keepdims=True)
    xc = x - mu
    var = jnp.mean(xc * xc, -1, keepdims=True)
    return xc * lax.rsqrt(var + LN_EPS) * g + b


def _silu(x):
    return x * jax.nn.sigmoid(x)


def _softplus(x):
    return jnp.maximum(x, 0.0) + jnp.log1p(jnp.exp(-jnp.abs(x)))


def _ln_kernel(x_ref, g_ref, b_ref, o_ref):
    o_ref[...] = _ln(x_ref[...], g_ref[...], b_ref[...])


def layer_norm_rows(x, g, b, tm):
    T, D = x.shape
    return pl.pallas_call(
        _ln_kernel, grid=(T // tm,),
        in_specs=[pl.BlockSpec((tm, D), lambda i: (i, 0)),
                  pl.BlockSpec((1, D), lambda i: (0, 0)),
                  pl.BlockSpec((1, D), lambda i: (0, 0))],
        out_specs=pl.BlockSpec((tm, D), lambda i: (i, 0)),
        out_shape=jax.ShapeDtypeStruct((T, D), f32),
        compiler_params=_cparams(1, 48), name="ln_in")(x, g.reshape(1, D), b.reshape(1, D))


def _ln_res_kernel(x_ref, rt_ref, g_ref, b_ref, o_ref):
    o_ref[...] = _ln(ALPHA * x_ref[...] + rt_ref[...].T, g_ref[...], b_ref[...])


def ln_residual_t(x, rt, g, b, tm):
    T, D = x.shape
    return pl.pallas_call(
        _ln_res_kernel, grid=(T // tm,),
        in_specs=[pl.BlockSpec((tm, D), lambda i: (i, 0)),
                  pl.BlockSpec((D, tm), lambda i: (0, i)),
                  pl.BlockSpec((1, D), lambda i: (0, 0)),
                  pl.BlockSpec((1, D), lambda i: (0, 0))],
        out_specs=pl.BlockSpec((tm, D), lambda i: (i, 0)),
        out_shape=jax.ShapeDtypeStruct((T, D), f32),
        compiler_params=_cparams(1, 48), name="ln_res")(x, rt, g.reshape(1, D), b.reshape(1, D))


def _mm_kernel(x_ref, w_ref, o_ref, xb_ref):
    @pl.when(pl.program_id(1) == 0)
    def _():
        xb_ref[...] = x_ref[...].astype(bf16)

    o_ref[...] = _dot(xb_ref[...], w_ref[...]).astype(o_ref.dtype)


def matmul(x, w, tm, tn, name, layer=None):
    T, K = x.shape
    N = w.shape[-1]
    if layer is None:
        w_spec = pl.BlockSpec((K, tn), lambda i, j: (0, j))
    else:
        w_spec = pl.BlockSpec((None, K, tn), lambda i, j: (layer, 0, j))
    return pl.pallas_call(
        _mm_kernel, grid=(T // tm, N // tn),
        in_specs=[pl.BlockSpec((tm, K), lambda i, j: (i, 0)), w_spec],
        out_specs=pl.BlockSpec((tm, tn), lambda i, j: (i, j)),
        out_shape=jax.ShapeDtypeStruct((T, N), f32),
        scratch_shapes=[pltpu.VMEM((tm, K), bf16)],
        compiler_params=_cparams(2, 52), name=name)(x, w)


def _gate_kernel(t_ref, fb_ref, o_ref):
    x = t_ref[...] + fb_ref[...]
    o_ref[...] = jnp.minimum(x, 0.0) - jnp.log1p(jnp.exp(-jnp.abs(x)))


def forget_gate(h, fb_row, tm):
    T = h.shape[0]
    return pl.pallas_call(
        _gate_kernel, grid=(T // tm,),
        in_specs=[pl.BlockSpec((tm, LANES), lambda i: (i, TAIL0 // LANES)),
                  pl.BlockSpec((1, LANES), lambda i: (0, 0))],
        out_specs=pl.BlockSpec((tm, LANES), lambda i: (i, 0)),
        out_shape=jax.ShapeDtypeStruct((T, LANES), f32),
        compiler_params=_cparams(1), name="forget_gate")(h, fb_row)


def _cumsum_kernel(x_ref, o_ref, carry_ref):
    @pl.when(pl.program_id(1) == 0)
    def _():
        carry_ref[...] = jnp.zeros_like(carry_ref)

    bl = x_ref.shape[-1]
    r = lax.broadcasted_iota(jnp.int32, (bl, bl), 0)
    c = lax.broadcasted_iota(jnp.int32, (bl, bl), 1)
    upper = (r <= c).astype(f32)
    y = jnp.dot(x_ref[0], upper, precision=lax.Precision.HIGHEST,
                preferred_element_type=f32) + carry_ref[:, 0:1]
    o_ref[0] = y
    carry_ref[...] = jnp.broadcast_to(y[:, bl - 1:bl], carry_ref.shape)


def cumsum_lanes(x, bl):
    n, r, L = x.shape
    return pl.pallas_call(
        _cumsum_kernel, grid=(n, L // bl),
        in_specs=[pl.BlockSpec((1, r, bl), lambda s, j: (s, 0, j))],
        out_specs=pl.BlockSpec((1, r, bl), lambda s, j: (s, 0, j)),
        out_shape=jax.ShapeDtypeStruct((n, r, L), f32),
        scratch_shapes=[pltpu.VMEM((r, LANES), f32)],
        compiler_params=_cparams(2), name="cumsum")(x)


def _cumsum_rows_kernel(x_ref, o_ref, carry_ref):
    @pl.when(pl.program_id(1) == 0)
    def _():
        carry_ref[...] = jnp.zeros_like(carry_ref)

    bl = x_ref.shape[0]
    r = lax.broadcasted_iota(jnp.int32, (bl, bl), 0)
    c = lax.broadcasted_iota(jnp.int32, (bl, bl), 1)
    lower = (c <= r).astype(f32)
    y = jnp.dot(lower, x_ref[...], precision=lax.Precision.HIGHEST,
                preferred_element_type=f32) + carry_ref[0:1, :]
    o_ref[...] = y
    carry_ref[...] = jnp.broadcast_to(y[bl - 1:bl, :], carry_ref.shape)


def cumsum_rows(x, nseq, L, bl):
    nb = L // bl
    return pl.pallas_call(
        _cumsum_rows_kernel, grid=(nseq, nb),
        in_specs=[pl.BlockSpec((bl, LANES), lambda s, j: (s * nb + j, 0))],
        out_specs=pl.BlockSpec((bl, LANES), lambda s, j: (s * nb + j, 0)),
        out_shape=jax.ShapeDtypeStruct((nseq * L, LANES), f32),
        scratch_shapes=[pltpu.VMEM((SUBLANES, LANES), f32)],
        compiler_params=_cparams(2), name="cumsum_rows")(x)


_BIAS_TERMS = 3


def _split_heads(q, lane):
    zero = jnp.zeros_like(q)
    return jnp.where(lane, q, zero), jnp.where(lane, zero, q)


def _pack_queries(q, lane_i):
    qs = q * (DH_A ** -0.5)
    q0 = jnp.where(lane_i < HALF, qs, jnp.where(lane_i < HALF + _BIAS_TERMS, 1.0, 0.0))
    q1 = jnp.where(lane_i >= HALF, qs, jnp.where(lane_i < _BIAS_TERMS, 1.0, 0.0))
    return q0.astype(bf16), q1.astype(bf16)


def _pack_keys(k, c0, c1, lane_i):
    def one(own, c, base):
        out = jnp.where(own, k, 0.0)
        rest = -c
        for t in range(_BIAS_TERMS):
            term = rest.astype(bf16).astype(f32)
            out = jnp.where(lane_i == base + t, term, out)
            rest = rest - term
        return out.astype(bf16)
    return one(lane_i < HALF, c0, HALF), one(lane_i >= HALF, c1, 0)


def _pack_values(v, lane):
    return jnp.where(lane, v, 1.0).astype(bf16), jnp.where(lane, 1.0, v).astype(bf16)


def _attn_step(qs, ks, vs, carry, mask):
    out = []
    for q, k, v, (m, acc) in zip(qs, ks, vs, carry):
        s = _dot_nt(q, k)
        if mask is not None:
            s = jnp.where(mask, s, NEG_INF)
        mn = jnp.maximum(m, jnp.max(s, -1, keepdims=True))
        p = jnp.exp(s - mn).astype(bf16)
        out.append((mn, acc * jnp.exp(m - mn) + _dot(p, v)))
    return tuple(out)


def _attn_init(tq):
    return tuple((jnp.full((tq, 1), NEG_INF, f32), jnp.zeros((tq, LANES), f32)) for _ in range(2))


def _attn_finish(carry, lane):
    (_, acc0), (_, acc1) = carry
    o0 = acc0 / pltpu.roll(acc0, HALF, 1)
    o1 = acc1 / pltpu.roll(acc1, HALF, 1)
    return jnp.where(lane, o0, o1)


def _fox_prompt_kernel(q_ref, k_ref, v_ref, c_ref, o_ref, k0_ref, k1_ref, v0_ref, v1_ref, *, tq, tk):
    i = pl.program_id(2)
    per = tq // tk
    lane_k = lax.broadcasted_iota(jnp.int32, (tk, LANES), 1)
    lane_q = lax.broadcasted_iota(jnp.int32, (tq, LANES), 1)

    @pl.when(i == 0)
    def _():
        def pack(j, _):
            rows = pl.ds(pl.multiple_of(j * tk, tk), tk)
            c = c_ref[0, 0, rows, :]
            k0, k1 = _pack_keys(k_ref[rows, :], c[:, 0:1], c[:, 1:2], lane_k)
            v0, v1 = _pack_values(v_ref[rows, :], lane_k < HALF)
            k0_ref[rows, :] = k0
            k1_ref[rows, :] = k1
            v0_ref[rows, :] = v0
            v1_ref[rows, :] = v1
            return 0

        lax.fori_loop(0, k_ref.shape[0] // tk, pack, 0)

    qs = _pack_queries(q_ref[...], lane_q)

    def step(j, carry, mask):
        rows = pl.ds(pl.multiple_of(j * tk, tk), tk)
        return _attn_step(qs, (k0_ref[rows, :], k1_ref[rows, :]), (v0_ref[rows, :], v1_ref[rows, :]), carry, mask)

    group = 2 if per % 2 == 0 else 1

    def body(g, carry):
        for u in range(group):
            carry = step(g * group + u, carry, None)
        return carry

    carry = lax.fori_loop(0, i * (per // group), body, _attn_init(tq))
    r = lax.broadcasted_iota(jnp.int32, (tq, tk), 0)
    c = lax.broadcasted_iota(jnp.int32, (tq, tk), 1)
    for d in range(per):
        carry = step(i * per + d, carry, c + d * tk <= r)
    o_ref[...] = _attn_finish(carry, lane_q < HALF).astype(o_ref.dtype)


def fox_prompt(h, cp, B, S, tq, tk):
    nq = S // tq
    qb, kb, vb = FQ0 // LANES, FK0 // LANES, FV0 // LANES
    return pl.pallas_call(
        functools.partial(_fox_prompt_kernel, tq=tq, tk=tk), grid=(B, H_A // 2, nq),
        in_specs=[pl.BlockSpec((tq, LANES), lambda b, p, i: (b * nq + i, qb + p)),
                  pl.BlockSpec((S, LANES), lambda b, p, i: (b, kb + p)),
                  pl.BlockSpec((S, LANES), lambda b, p, i: (b, vb + p)),
                  pl.BlockSpec((1, 1, S, 2), lambda b, p, i: (b, p, 0, 0))],
        out_specs=pl.BlockSpec((tq, LANES), lambda b, p, i: (b * nq + i, p)),
        out_shape=jax.ShapeDtypeStruct((B * S, W_A), bf16),
        scratch_shapes=[pltpu.VMEM((S, LANES), bf16) for _ in range(4)],
        compiler_params=_cparams(3, 52), name="fox_prompt")(h, h, h, cp)


def _fox_sample_kernel(q_ref, kn_ref, vn_ref, kc_ref, vc_ref, c_ref, o_ref, *, Ls, P, tk):
    lane = lax.broadcasted_iota(jnp.int32, (Ls, LANES), 1) < HALF
    qs = _pack_queries(q_ref[...], lax.broadcasted_iota(jnp.int32, (Ls, LANES), 1))

    def block(k, v, c, carry, mask):
        lane_i = lax.broadcasted_iota(jnp.int32, k.shape, 1)
        return _attn_step(qs, _pack_keys(k, c[:, 0:1], c[:, 1:2], lane_i), _pack_values(v, lane_i < HALF),
                          carry, mask)

    carry = _attn_init(Ls)
    for j in range(P // tk):
        rows = slice(j * tk, (j + 1) * tk)
        carry = block(kc_ref[0, rows, :], vc_ref[0, rows, :], c_ref[0, 0, rows, :], carry, None)
    r = lax.broadcasted_iota(jnp.int32, (Ls, Ls), 0)
    c = lax.broadcasted_iota(jnp.int32, (Ls, Ls), 1)
    carry = block(kn_ref[...], vn_ref[...], c_ref[0, 0, P:P + Ls, :], carry, c <= r)
    o_ref[...] = _attn_finish(carry, lane).astype(o_ref.dtype)


def fox_sample(h, ck, cv, cp, row0, Bs, Ls, P):
    qb, kb, vb = FQ0 // LANES, FK0 // LANES, FV0 // LANES
    tk = _pick(P, (1024, 512, 256, 128))
    Lp = cp.shape[2]
    return pl.pallas_call(
        functools.partial(_fox_sample_kernel, Ls=Ls, P=P, tk=tk), grid=(Bs, H_A // 2),
        in_specs=[pl.BlockSpec((Ls, LANES), lambda b, p: (row0 + b, qb + p)),
                  pl.BlockSpec((Ls, LANES), lambda b, p: (row0 + b, kb + p)),
                  pl.BlockSpec((Ls, LANES), lambda b, p: (row0 + b, vb + p)),
                  pl.BlockSpec((1, P, LANES), lambda b, p: (b, 0, p)),
                  pl.BlockSpec((1, P, LANES), lambda b, p: (b, 0, p)),
                  pl.BlockSpec((1, 1, Lp, 2), lambda b, p: (b, p, 0, 0))],
        out_specs=pl.BlockSpec((Ls, LANES), lambda b, p: (b, p)),
        out_shape=jax.ShapeDtypeStruct((Bs * Ls, W_A), bf16),
        compiler_params=_cparams(2, 48), name="fox_sample")(h, h, h, ck, cv, cp)


def _pair_mean(x, lane):
    s0 = jnp.sum(jnp.where(lane, x, 0.0), -1, keepdims=True)
    s1 = jnp.sum(jnp.where(lane, 0.0, x), -1, keepdims=True)
    return jnp.where(lane, s0, s1) * (1.0 / HALF)


def _rotary(x, cos, sin_signed, first_half):
    xr = jnp.where(first_half, pltpu.roll(x, LANES - HALF // 2, 1), pltpu.roll(x, HALF // 2, 1))
    return x * cos + xr * sin_signed


def _retention_kernel(q_ref, k_ref, v_ref, g_ref, cos_ref, sin_ref, dec_ref, gq_ref, gk_ref, gl_ref,
                      s0_ref, gn_ref, o_ref, s1_ref, st_ref, *, Lc):
    c = pl.program_id(1)

    @pl.when(c == 0)
    def _():
        st_ref[...] = s0_ref[0]

    lane_i = lax.broadcasted_iota(jnp.int32, (Lc, LANES), 1)
    lane = lane_i < HALF
    first_half = (lane_i % HALF) < (HALF // 2)
    cos, sin = cos_ref[...], sin_ref[...]
    sr = lax.broadcasted_iota(jnp.int32, (LANES, LANES), 0) < HALF
    sc = lax.broadcasted_iota(jnp.int32, (LANES, LANES), 1) < HALF
    for p in range(H_B // 2):
        cols = slice(p * LANES, (p + 1) * LANES)
        q = _rotary(q_ref[:, cols], cos, sin, first_half)
        k = _rotary(k_ref[:, cols], cos, sin, first_half) * (DK_B ** -0.5)
        qb, kb, vb = q.astype(bf16), k.astype(bf16), v_ref[:, cols].astype(bf16)
        q0, q1 = _split_heads(qb, lane)
        a0 = (_dot_nt(q0, kb) * dec_ref[2 * p]).astype(bf16)
        a1 = (_dot_nt(q1, kb) * dec_ref[2 * p + 1]).astype(bf16)
        intra = jnp.where(lane, _dot(a0, vb), _dot(a1, vb))
        st = st_ref[p]
        o = intra + _dot(qb, st.astype(bf16)) * gq_ref[p]
        kd = (k * gk_ref[p]).astype(bf16)
        st_ref[p] = gl_ref[p, 0:1, :] * st + jnp.where(sr == sc, _dot_tn(kd, vb), 0.0)
        mu = _pair_mean(o, lane)
        d = o - mu
        var = _pair_mean(d * d, lane)
        on = d * lax.rsqrt(var + LN_EPS) * gn_ref[:, cols]
        o_ref[:, cols] = (_silu(g_ref[:, cols]) * on).astype(o_ref.dtype)

    @pl.when(c == pl.num_programs(1) - 1)
    def _():
        s1_ref[0] = st_ref[...]


def _retention_tables(pos, Lc):
    half = DK_B // 2
    inv = ROPE_BASE ** (-jnp.arange(half, dtype=f32) / half)
    ang = pos.astype(f32)[:, None] * inv[None, :]
    cos, sin = jnp.cos(ang), jnp.sin(ang)
    cos_t = jnp.tile(cos, (1, 4))
    sin_t = jnp.tile(jnp.concatenate([-sin, sin], -1), (1, 2))
    lg = jnp.log1p(-jnp.exp2(-5.0 - jnp.arange(H_B, dtype=f32)))
    i = jnp.arange(Lc, dtype=f32)
    diff = i[:, None] - i[None, :]
    dec = jnp.exp(jnp.where((diff >= 0)[None], diff[None] * lg[:, None, None], NEG_INF))
    pair = lambda t: jnp.repeat(t.reshape(t.shape[0], H_B // 2, 2), HALF, axis=-1)
    gq = pair(jnp.exp((i[:, None] + 1.0) * lg[None, :])).transpose(1, 0, 2)
    gk = pair(jnp.exp((Lc - 1.0 - i)[:, None] * lg[None, :])).transpose(1, 0, 2)
    gl = jnp.broadcast_to(pair(jnp.exp(Lc * lg)[None, :]).transpose(1, 0, 2), (H_B // 2, SUBLANES, LANES))
    return cos_t, sin_t, dec, gq, gk, gl


def retention(h, gn_w, s0, pos, row0, nseq, L, Lc):
    nch = L // Lc
    npair = H_B // 2
    cos_t, sin_t, dec, gq, gk, gl = _retention_tables(pos, Lc)
    blk = lambda col0: pl.BlockSpec((Lc, W_B), lambda s, c: (row0 + s * nch + c, col0 // W_B))
    full = lambda shape: pl.BlockSpec(shape, lambda s, c: (0,) * len(shape))
    return pl.pallas_call(
        functools.partial(_retention_kernel, Lc=Lc), grid=(nseq, nch),
        in_specs=[blk(RQ0), blk(RK0), blk(RV0), blk(RG0),
                  pl.BlockSpec((Lc, LANES), lambda s, c: (c, 0)),
                  pl.BlockSpec((Lc, LANES), lambda s, c: (c, 0)),
                  full((H_B, Lc, Lc)), full((npair, Lc, LANES)), full((npair, Lc, LANES)),
                  full((npair, SUBLANES, LANES)),
                  pl.BlockSpec((1, npair, LANES, LANES), lambda s, c: (s, 0, 0, 0)),
                  full((1, W_B))],
        out_specs=[pl.BlockSpec((Lc, W_B), lambda s, c: (s * nch + c, 0)),
                   pl.BlockSpec((1, npair, LANES, LANES), lambda s, c: (s, 0, 0, 0))],
        out_shape=[jax.ShapeDtypeStruct((nseq * L, W_B), bf16),
                   jax.ShapeDtypeStruct((nseq, npair, LANES, LANES), f32)],
        scratch_shapes=[pltpu.VMEM((npair, LANES, LANES), f32)],
        compiler_params=_cparams(2, 32), name="retention")(
            h, h, h, h, cos_t, sin_t, dec, gq, gk, gl, s0, gn_w.reshape(1, W_B))


def _pack_ret_state(s):
    n = s.shape[0]
    s = s.reshape(n, H_B // 2, 2, DK_B, DK_B)
    z = jnp.zeros_like(s[:, :, 0])
    top = jnp.concatenate([s[:, :, 0], z], -1)
    bot = jnp.concatenate([z, s[:, :, 1]], -1)
    return jnp.concatenate([top, bot], -2)


def _unpack_ret_state(s):
    n = s.shape[0]
    return jnp.stack([s[:, :, :HALF, :HALF], s[:, :, HALF:, HALF:]], 2).reshape(n, H_B, DK_B, DK_B)


def _ssd_kernel(xbc_ref, z_ref, t_ref, cw_ref, cb_ref, dtb_ref, alog_ref, dsk_ref, nw_ref, hist_ref, h0_ref,
                o_ref, h1_ref, xpad_ref, hs_ref, *, Lc):
    c = pl.program_id(1)

    @pl.when(c == 0)
    def _():
        xpad_ref[0:SUBLANES, :] = hist_ref[0]
        hs_ref[...] = h0_ref[0]

    xpad_ref[SUBLANES:SUBLANES + Lc, :] = xbc_ref[...]
    conv = cb_ref[...]
    for t in range(CONV_W):
        r0 = SUBLANES - (CONV_W - 1) + t
        conv = conv + xpad_ref[r0:r0 + Lc, :] * cw_ref[t:t + 1, :]
    xpad_ref[0:SUBLANES, :] = xpad_ref[Lc:Lc + SUBLANES, :]
    xc = _silu(conv)
    xs = xc[:, :W_C]
    bm = [xc[:, W_C + g * N_C:W_C + (g + 1) * N_C].astype(bf16) for g in range(G_C)]
    cm = [xc[:, W_C + (G_C + g) * N_C:W_C + (G_C + g + 1) * N_C].astype(bf16) for g in range(G_C)]

    lane_i = lax.broadcasted_iota(jnp.int32, (Lc, LANES), 1)
    lane = lane_i < HALF
    dt_valid = (lane_i >= DT_LANE0) & (lane_i < DT_LANE0 + H_C)
    dt = _softplus(t_ref[...] + dtb_ref[...])
    dta = jnp.where(dt_valid, dt * (-jnp.exp(alog_ref[...])), 0.0)
    r = lax.broadcasted_iota(jnp.int32, (Lc, Lc), 0)
    cidx = lax.broadcasted_iota(jnp.int32, (Lc, Lc), 1)
    tri = cidx <= r
    a_cs = jnp.dot(tri.astype(f32), dta, precision=lax.Precision.HIGHEST, preferred_element_type=f32)
    a_cs_t = a_cs.T
    dt_t = dt.T
    row_first = lax.broadcasted_iota(jnp.int32, (LANES, LANES), 0) < HALF

    ys = []
    for p in range(H_C // 2):
        g = (2 * p) // (H_C // G_C)
        xpair = xs[:, p * LANES:(p + 1) * LANES]
        xpair_b = xpair.astype(bf16)
        if p % (H_C // G_C // 2) == 0:
            cb = _dot_nt(cm[g], bm[g])
        yh, acol, dcol = [], [], []
        for hd in (2 * p, 2 * p + 1):
            li = DT_LANE0 + hd
            ac = a_cs[:, li:li + 1]
            seg = ac - a_cs_t[li:li + 1, :]
            w = cb * jnp.exp(jnp.where(tri, seg, NEG_INF)) * dt_t[li:li + 1, :]
            yh.append(_dot(w.astype(bf16), xpair_b))
            acol.append(ac)
            dcol.append(dt[:, li:li + 1])
        acs_pair = jnp.where(lane, acol[0], acol[1])
        dt_pair = jnp.where(lane, dcol[0], dcol[1])
        hs = hs_ref[p]
        y = jnp.where(lane, yh[0], yh[1]) + _dot_nt(cm[g], hs.astype(bf16)) * jnp.exp(acs_pair)
        ys.append(y)
        a_last = acs_pair[Lc - 1:Lc, :]
        to_end = jnp.exp(a_last - acs_pair) * dt_pair
        upd = _dot_tn((xpair * to_end).astype(bf16), bm[g])
        sdec = jnp.exp(jnp.where(row_first, acol[0][Lc - 1:Lc, :], acol[1][Lc - 1:Lc, :]))
        hs_ref[p] = sdec * hs + upd

    y = jnp.concatenate(ys, axis=1) + dsk_ref[...] * xs
    y = y * _silu(z_ref[...])
    y = y * lax.rsqrt(jnp.mean(y * y, -1, keepdims=True) + LN_EPS) * nw_ref[...]
    o_ref[...] = y.astype(o_ref.dtype)

    @pl.when(c == pl.num_programs(1) - 1)
    def _():
        h1_ref[0] = hs_ref[...]


def ssd(h, conv_w, conv_b, dt_bias, a_log, d_skip, norm_w, hist, h0, row0, nseq, L, Lc):
    nch = L // Lc
    lane_row = lambda v: jnp.zeros((1, LANES), f32).at[0, DT_LANE0:DT_LANE0 + H_C].set(v)
    cw = jnp.zeros((SUBLANES, CONV_DIM), f32).at[:CONV_W].set(conv_w)
    row = lambda w: pl.BlockSpec((1, w), lambda s, c: (0, 0))
    return pl.pallas_call(
        functools.partial(_ssd_kernel, Lc=Lc), grid=(nseq, nch),
        in_specs=[pl.BlockSpec((Lc, CONV_DIM), lambda s, c: (row0 + s * nch + c, XBC0 // CONV_DIM)),
                  pl.BlockSpec((Lc, W_C), lambda s, c: (row0 + s * nch + c, Z0 // W_C)),
                  pl.BlockSpec((Lc, LANES), lambda s, c: (row0 + s * nch + c, TAIL0 // LANES)),
                  pl.BlockSpec((SUBLANES, CONV_DIM), lambda s, c: (0, 0)),
                  row(CONV_DIM), row(LANES), row(LANES), row(W_C), row(W_C),
                  pl.BlockSpec((1, SUBLANES, CONV_DIM), lambda s, c: (s, 0, 0)),
                  pl.BlockSpec((1, H_C // 2, LANES, LANES), lambda s, c: (s, 0, 0, 0))],
        out_specs=[pl.BlockSpec((Lc, W_C), lambda s, c: (s * nch + c, 0)),
                   pl.BlockSpec((1, H_C // 2, LANES, LANES), lambda s, c: (s, 0, 0, 0))],
        out_shape=[jax.ShapeDtypeStruct((nseq * L, W_C), bf16),
                   jax.ShapeDtypeStruct((nseq, H_C // 2, LANES, LANES), f32)],
        scratch_shapes=[pltpu.VMEM((Lc + SUBLANES, CONV_DIM), f32),
                        pltpu.VMEM((H_C // 2, LANES, LANES), f32)],
        compiler_params=_cparams(2, 48), name="ssd")(
            h, h, h, cw, conv_b.reshape(1, CONV_DIM), lane_row(dt_bias), lane_row(a_log),
            jnp.repeat(d_skip, P_C).reshape(1, W_C), norm_w.reshape(1, W_C), hist, h0)


def _outproj_kernel(x_ref, a_ref, b_ref, c_ref, w_ref, g_ref, beta_ref, o_ref):
    mix = (_dot(a_ref[...], w_ref[0:W_A, :]) + _dot(b_ref[...], w_ref[W_A:W_A + W_B, :])
           + _dot(c_ref[...], w_ref[W_A + W_B:, :]))
    o_ref[...] = _ln(ALPHA * x_ref[...] + mix, g_ref[...], beta_ref[...])


def outproj_ln(x, ao, bo, co, w, g, b, tm):
    T, D = x.shape
    rows = lambda wd: pl.BlockSpec((tm, wd), lambda i: (i, 0))
    const = lambda s: pl.BlockSpec(s, lambda i: (0, 0))
    return pl.pallas_call(
        _outproj_kernel, grid=(T // tm,),
        in_specs=[rows(D), rows(W_A), rows(W_B), rows(W_C), const(w.shape), const((1, D)), const((1, D))],
        out_specs=rows(D),
        out_shape=jax.ShapeDtypeStruct((T, D), f32),
        compiler_params=_cparams(1, 52), name="outproj_ln")(x, ao, bo, co, w, g.reshape(1, D), b.reshape(1, D))


def _mem_body(x_ref, wq_ref, mk_ref, mv_ref, wo_ref, g_ref, b_ref, o_ref, ob_ref):
    x = x_ref[...]
    q = _dot(x.astype(bf16), wq_ref[...])
    outs = []
    for hd in range(H_M):
        sl = slice(hd * DH_M, (hd + 1) * DH_M)
        s = _dot_nt(q[:, sl].astype(bf16), mk_ref[0, :, sl].astype(bf16)) * (DH_M ** -0.5)
        p = jnp.exp(s - jnp.max(s, -1, keepdims=True))
        p = p / jnp.sum(p, -1, keepdims=True)
        outs.append(_dot(p.astype(bf16), mv_ref[0, :, sl].astype(bf16)))
    o = jnp.concatenate(outs, axis=1).astype(bf16)
    y = _ln(ALPHA * x + _dot(o, wo_ref[...]), g_ref[...], b_ref[...])
    o_ref[...] = y
    ob_ref[...] = y.astype(bf16)


def _mem_kernel_first(x_ref, wq_ref, mk_ref, mv_ref, wo_ref, g_ref, b_ref, o_ref, ob_ref):
    _mem_body(x_ref, wq_ref, mk_ref, mv_ref, wo_ref, g_ref, b_ref, o_ref, ob_ref)


def _mem_kernel_into(x_ref, wq_ref, mk_ref, mv_ref, wo_ref, g_ref, b_ref, prev_ref, prevb_ref, o_ref, ob_ref):
    del prev_ref, prevb_ref
    _mem_body(x_ref, wq_ref, mk_ref, mv_ref, wo_ref, g_ref, b_ref, o_ref, ob_ref)


def mem_attn_ln(x, wq, mk, mv, wo, g, b, row0, nseq, L, tr, into=None):
    T, D = x.shape
    per = L // tr
    nm = mk.shape[1]
    const = lambda s: pl.BlockSpec(s, lambda i: (0,) * len(s))
    in_specs = [pl.BlockSpec((tr, D), lambda i: (row0 + i, 0)), const(wq.shape),
                pl.BlockSpec((1, nm, W_M), lambda i: (i // per, 0, 0)),
                pl.BlockSpec((1, nm, W_M), lambda i: (i // per, 0, 0)),
                const(wo.shape), const((1, D)), const((1, D))]
    args = [x, wq, mk, mv, wo, g.reshape(1, D), b.reshape(1, D)]
    aliases = {}
    if into is not None:
        in_specs += [pl.BlockSpec(memory_space=pl.ANY), pl.BlockSpec(memory_space=pl.ANY)]
        aliases = {len(args): 0, len(args) + 1: 1}
        args += list(into)
    return pl.pallas_call(
        _mem_kernel_first if into is None else _mem_kernel_into, grid=(nseq * per,),
        in_specs=in_specs,
        out_specs=[pl.BlockSpec((tr, D), lambda i: (row0 + i, 0)), pl.BlockSpec((tr, D), lambda i: (row0 + i, 0))],
        out_shape=[jax.ShapeDtypeStruct((T, D), f32), jax.ShapeDtypeStruct((T, D), bf16)],
        input_output_aliases=aliases,
        compiler_params=_cparams(1, 48), name="mem_attn_ln")(*args)


def _peer_score_kernel(x_ref, wq_ref, k1_ref, k2_ref, o_ref):
    q = _dot(x_ref[...].astype(bf16), wq_ref[...])
    k1 = k1_ref[...].astype(bf16)
    k2 = k2_ref[...].astype(bf16)
    hk = PEER_DK // 2
    for hd in range(PEER_HEADS):
        q1 = q[:, hd * PEER_DK:hd * PEER_DK + hk].astype(bf16)
        q2 = q[:, hd * PEER_DK + hk:(hd + 1) * PEER_DK].astype(bf16)
        o_ref[hd, 0:PEER_NK, :] = _dot_nt(k1, q1)
        o_ref[hd, PEER_NK:2 * PEER_NK, :] = _dot_nt(k2, q2)


def peer_scores(x, wq, k1, k2, tm):
    T, D = x.shape
    const = lambda s: pl.BlockSpec(s, lambda i: (0, 0))
    return pl.pallas_call(
        _peer_score_kernel, grid=(T // tm,),
        in_specs=[pl.BlockSpec((tm, D), lambda i: (i, 0)), const(wq.shape), const(k1.shape), const(k2.shape)],
        out_specs=pl.BlockSpec((PEER_HEADS, 2 * PEER_NK, tm), lambda i: (0, 0, i)),
        out_shape=jax.ShapeDtypeStruct((PEER_HEADS, 2 * PEER_NK, T), f32),
        compiler_params=_cparams(1, 52), name="peer_scores")(x, wq, k1, k2)


_CAND_ROWS = PEER_TOPK + 7 * SUBLANES + SUBLANES


def _cand_flat_index(cw):
    idx = [0 * PEER_TOPK + b for b in range(PEER_TOPK)]
    for a in range(1, 8):
        idx += [a * PEER_TOPK + b for b in range(SUBLANES)]
    idx += [a * PEER_TOPK for a in range(8, PEER_TOPK)]
    return np.broadcast_to(np.asarray(idx, np.float32)[:, None], (_CAND_ROWS, cw)).copy()


def _extract_top(s, key_idx, n):
    rank = jnp.full(s.shape, float(n), f32)
    vals = []
    for a in range(n):
        mx = jnp.max(s, axis=0, keepdims=True)
        first = jnp.min(jnp.where(s == mx, key_idx, float(1 << 20)), axis=0, keepdims=True)
        sel = key_idx == first
        rank = jnp.where(sel, float(a), rank)
        s = jnp.where(sel, NEG_INF, s)
        vals.append(mx)
    return vals, rank


def _top_values(s, n, with_rank):
    rank = jnp.full(s.shape, float(n), f32) if with_rank else None
    vals = []
    for a in range(n):
        mx = jnp.max(s, axis=0, keepdims=True)
        hit = s == mx
        if with_rank:
            rank = jnp.where(hit, float(a), rank)
        s = jnp.where(hit, NEG_INF, s)
        vals.append(mx)
    return vals, rank


def _candidates(v1, v2):
    v1a = jnp.concatenate(v1, axis=0)
    v2a = jnp.concatenate(v2, axis=0)
    return jnp.concatenate([v1[0] + v2a] + [v1[a] + v2a[0:SUBLANES] for a in range(1, 8)]
                           + [v1a[SUBLANES:] + v2[0]], axis=0)


def _counts_per_rank(picked):
    cnt_a = [jnp.sum(picked[0:PEER_TOPK], axis=0, keepdims=True)]
    for a in range(1, 8):
        lo = PEER_TOPK + (a - 1) * SUBLANES
        cnt_a.append(jnp.sum(picked[lo:lo + SUBLANES], axis=0, keepdims=True))
    tail = picked[PEER_TOPK + 7 * SUBLANES:]
    return cnt_a + [tail[a:a + 1] for a in range(SUBLANES)]


def _select_fast(s1, s2):
    v1, _ = _top_values(s1, PEER_TOPK, False)
    v2, rank2 = _top_values(s2, PEER_TOPK, True)
    cand = _candidates(v1, v2)
    vc, _ = _top_values(cand, PEER_TOPK, False)
    top = v1[0] + v2[0]
    zsum = jnp.exp(vc[0] - top)
    for k in range(1, PEER_TOPK):
        zsum = zsum + jnp.exp(vc[k] - top)
    picked = jnp.where(cand >= vc[PEER_TOPK - 1], 1.0, 0.0)
    cnt_a = _counts_per_rank(picked)
    cnt = jnp.zeros(s1.shape, f32)
    for a in range(PEER_TOPK):
        cnt = jnp.where(s1 == v1[a], cnt_a[a], cnt)
    n1 = jnp.sum(jnp.where(s1 >= v1[PEER_TOPK - 1], 1.0, 0.0), axis=0, keepdims=True)
    n2 = jnp.sum(jnp.where(rank2 < float(PEER_TOPK), 1.0, 0.0), axis=0, keepdims=True)
    nc = jnp.sum(picked, axis=0, keepdims=True)
    k = float(PEER_TOPK)
    distinct = (n1 == k) & (n2 == k) & (nc == k)
    tie = jnp.max(jnp.where(distinct, 0.0, 1.0)) > 0.0
    return rank2, cnt, zsum, v1[0], v2[0], tie


def _select_exact(s1, s2, key_idx, cidx):
    v1, rank1 = _extract_top(s1, key_idx, PEER_TOPK)
    v2, rank2 = _extract_top(s2, key_idx, PEER_TOPK)
    cand = _candidates(v1, v2)
    top = v1[0] + v2[0]
    picked = jnp.zeros(cand.shape, f32)
    zsum = jnp.zeros((1, s1.shape[1]), f32)
    for _k in range(PEER_TOPK):
        mx = jnp.max(cand, axis=0, keepdims=True)
        first = jnp.min(jnp.where(cand == mx, cidx, float(1 << 20)), axis=0, keepdims=True)
        sel = cidx == first
        picked = jnp.where(sel, 1.0, picked)
        cand = jnp.where(sel, NEG_INF, cand)
        zsum = zsum + jnp.exp(mx - top)
    cnt_a = _counts_per_rank(picked)
    cnt = jnp.zeros(s1.shape, f32)
    for a in range(PEER_TOPK):
        cnt = jnp.where(rank1 == float(a), cnt_a[a], cnt)
    return rank2, cnt, zsum


def _peer_topk_kernel(s_ref, cidx_ref, r2_ref, g2_ref, cnt_ref, g1_ref, *, tb, cw):
    nj = PEER_NK // SUBLANES

    def chunk(ci, _):
        off = pl.multiple_of(ci * cw, cw)
        s1 = s_ref[0, 0:PEER_NK, pl.ds(off, cw)]
        s2 = s_ref[0, PEER_NK:2 * PEER_NK, pl.ds(off, cw)]
        rank2, cnt, zsum, m1, m2, tie = _select_fast(s1, s2)
        e1 = jnp.exp(s1 - m1)
        r2_ref[0, :, pl.ds(off, cw)] = rank2.astype(bf16)
        g2_ref[0, :, pl.ds(off, cw)] = jnp.exp(s2 - m2).astype(bf16)
        cnt_ref[:, 0, :, pl.ds(off, cw)] = cnt.reshape(nj, SUBLANES, cw)
        g1_ref[:, 0, :, pl.ds(off, cw)] = (e1 / zsum).reshape(nj, SUBLANES, cw)

        @pl.when(tie)
        def _():
            key_idx = lax.broadcasted_iota(jnp.int32, (PEER_NK, cw), 0).astype(f32)
            rank2x, cntx, zsumx = _select_exact(s1, s2, key_idx, cidx_ref[...])
            r2_ref[0, :, pl.ds(off, cw)] = rank2x.astype(bf16)
            cnt_ref[:, 0, :, pl.ds(off, cw)] = cntx.reshape(nj, SUBLANES, cw)
            g1_ref[:, 0, :, pl.ds(off, cw)] = (e1 / zsumx).reshape(nj, SUBLANES, cw)

        return 0

    lax.fori_loop(0, tb // cw, chunk, 0)


def peer_topk(sT, tb):
    T = sT.shape[-1]
    nj = PEER_NK // SUBLANES
    cw = _pick(tb, (2 * LANES, LANES))
    per_key = lambda: pl.BlockSpec((1, PEER_NK, tb), lambda hd, i: (hd, 0, i))
    per_row = lambda: pl.BlockSpec((nj, 1, SUBLANES, tb), lambda hd, i: (0, hd, 0, i))
    r2, g2, cnt, g1 = pl.pallas_call(
        functools.partial(_peer_topk_kernel, tb=tb, cw=cw), grid=(PEER_HEADS, T // tb),
        in_specs=[pl.BlockSpec((1, 2 * PEER_NK, tb), lambda hd, i: (hd, 0, i)),
                  pl.BlockSpec((_CAND_ROWS, cw), lambda hd, i: (0, 0))],
        out_specs=[per_key(), per_key(), per_row(), per_row()],
        out_shape=[jax.ShapeDtypeStruct((PEER_HEADS, PEER_NK, T), bf16),
                   jax.ShapeDtypeStruct((PEER_HEADS, PEER_NK, T), bf16),
                   jax.ShapeDtypeStruct((nj, PEER_HEADS, SUBLANES, T), f32),
                   jax.ShapeDtypeStruct((nj, PEER_HEADS, SUBLANES, T), f32)],
        compiler_params=_cparams(2, 32), name="peer_topk")(sT, jnp.asarray(_cand_flat_index(cw)))
    packed = (PEER_HEADS, PEER_NK // BF16_ROWS, BF16_ROWS, T)
    return (r2.reshape(packed), g2.reshape(packed),
            cnt.reshape(nj, PEER_HEADS * SUBLANES, T), g1.reshape(nj, PEER_HEADS * SUBLANES, T))


def _peer_main_kernel(x_ref, u_ref, vt_ref, r2_ref, g2_ref, cnt_ref, g1_ref, o_ref):
    j = pl.program_id(1)
    tm = x_ref.shape[0]

    @pl.when(j == 0)
    def _():
        o_ref[...] = jnp.zeros_like(o_ref)

    a = _dot_nt(u_ref[...], x_ref[...])
    act = (0.5 * a * (1.0 + lax.erf(a * (2.0 ** -0.5)))).astype(bf16)
    zero = jnp.zeros((), bf16)
    parts = []
    for r in range(SUBLANES):
        w = None
        for hd in range(PEER_HEADS):
            row = hd * SUBLANES + r
            cnt = jnp.broadcast_to(cnt_ref[0, row:row + 1, :], (BF16_ROWS, tm)).astype(bf16)[None]
            g1 = jnp.broadcast_to(g1_ref[0, row:row + 1, :], (BF16_ROWS, tm)).astype(bf16)[None]
            sel = jnp.where(r2_ref[hd] < cnt, g2_ref[hd], zero) * g1
            w = sel if w is None else w + sel
        parts.append(w.reshape(PEER_NK, tm) * act[r * PEER_NK:(r + 1) * PEER_NK])
    o_ref[...] += _dot(vt_ref[...], jnp.concatenate(parts, axis=0))


def peer_main(xb, ub, vtb, layer, r2, g2, cnt, g1, tm):
    T, D = xb.shape
    te = SUBLANES * PEER_NK
    nj = PEER_NK // SUBLANES
    once = pl.Buffered(1)
    per_tok = lambda: pl.BlockSpec((PEER_HEADS, PEER_NK // BF16_ROWS, BF16_ROWS, tm),
                                   lambda i, j: (0, 0, 0, i), pipeline_mode=once)
    per_row = lambda: pl.BlockSpec((1, PEER_HEADS * SUBLANES, tm), lambda i, j: (j, 0, i))
    return pl.pallas_call(
        _peer_main_kernel, grid=(T // tm, nj),
        in_specs=[pl.BlockSpec((tm, D), lambda i, j: (i, 0), pipeline_mode=once),
                  pl.BlockSpec((None, te, D), lambda i, j: (layer, j, 0)),
                  pl.BlockSpec((None, D, te), lambda i, j: (layer, 0, j)),
                  per_tok(), per_tok(), per_row(), per_row()],
        out_specs=pl.BlockSpec((D, tm), lambda i, j: (0, i), pipeline_mode=once),
        out_shape=jax.ShapeDtypeStruct((D, T), f32),
        compiler_params=_cparams(2, 58), name="peer_main")(xb, ub, vtb, r2, g2, cnt, g1)


def _rearrange_w_in(w):
    o = np.cumsum([0, W_A, W_A, W_A, H_A, H_B * DK_B, H_B * DK_B, W_B, W_B, W_C, CONV_DIM, H_C])
    fq, fk, fv, ff, rq, rk, rv, rg, z, xbc, dtr = [slice(int(o[i]), int(o[i + 1])) for i in range(11)]
    parts = [w[:, s] for s in (xbc, fq, z, fk, fv, rq, rk, rv, rg, ff, dtr)]
    pad = NH - TAIL0 - H_A - H_C
    return jnp.concatenate(parts + [jnp.zeros((w.shape[0], pad), w.dtype)], axis=1).astype(bf16)


def kernel(x_prompt, x_sample, cache_fox_k, cache_fox_v, cache_fox_logf, state_ret, state_ssm, state_conv, cache_mem_k, cache_mem_v, mem_prompt, ln_in_g, ln_in_b, w_in, fox_fb, ret_gn_w, conv_w, conv_b, dt_bias, a_log, d_skip, ssm_norm_w, w_out, ln1_g, ln1_b, wq_mem, wkv_mem, wo_mem, ln2_g, ln2_b, peer_wq, peer_k1, peer_k2, peer_u, peer_v, ln3_g, ln3_b):
    B, S, D = x_prompt.shape
    Bs, Ls, _ = x_sample.shape
    P = cache_fox_k.shape[2]
    NM = mem_prompt.shape[1]
    Tp, Ts = B * S, Bs * Ls
    T = Tp + Ts
    assert Tp % Ls == 0 and T % LANES == 0 and Ls >= CONV_W - 1

    tm = _pick(T, (640, 768, 384, 256, 128))
    tm_peer = _pick(T, (1280, 768, 256, 128))
    tq = _pick(S, (1024, 512, 256, 128))
    tk = min(tq, 512)
    lc = _pick(S, (256, 128, 64))
    tr_mem = _pick(S, (512, 256, 128))
    cs_bl = 512
    tn_in = _pick(NH, (1280, 640))

    x = jnp.concatenate([x_prompt.reshape(Tp, D), x_sample.reshape(Ts, D)], axis=0)
    x = layer_norm_rows(x, ln_in_g, ln_in_b, tm)

    pos_p = jnp.arange(S)
    pos_s = P + jnp.arange(Ls)
    zeros_ret = jnp.zeros((B, H_B // 2, LANES, LANES), f32)
    zeros_ssm = jnp.zeros((B, H_C // 2, LANES, LANES), f32)
    zeros_hist = jnp.zeros((B, SUBLANES, CONV_DIM), f32)
    lp_s = -(-(P + Ls) // cs_bl) * cs_bl

    w_in_b = jax.vmap(_rearrange_w_in)(w_in)
    peer_u_b = peer_u.astype(bf16)
    peer_vt_b = peer_v.transpose(0, 2, 1).astype(bf16)

    st_p = [[] for _ in range(8)]
    st_s = [[] for _ in range(6)]
    for l in range(DEPTH):
        h = matmul(x, w_in_b, tm, tn_in, "in_proj", layer=l)
        fb_row = jnp.zeros((1, LANES), f32).at[0, FF_LANE0:FF_LANE0 + H_A].set(fox_fb[l])
        logf_pad = forget_gate(h, fb_row, tm)
        logf = logf_pad[:, FF_LANE0:FF_LANE0 + H_A]
        logf_p = logf[:Tp].reshape(B, S, H_A)
        logf_s = logf[Tp:].reshape(Bs, Ls, H_A)

        c_p = cumsum_rows(logf_pad, B, S, min(cs_bl, S))[:, FF_LANE0:FF_LANE0 + H_A]
        c_p = c_p.reshape(B, S, H_A // 2, 2).transpose(0, 2, 1, 3)
        ao_p = fox_prompt(h, c_p, B, S, tq, tk)
        lf_all = jnp.concatenate([cache_fox_logf[l].astype(f32), logf_s], axis=1).transpose(0, 2, 1)
        lf_all = jnp.pad(lf_all, ((0, 0), (0, 0), (0, lp_s - (P + Ls))))
        c_s = cumsum_lanes(lf_all, cs_bl)[:, :, :P + Ls].reshape(Bs, H_A // 2, 2, P + Ls).transpose(0, 1, 3, 2)
        ao_s = fox_sample(h, cache_fox_k[l].reshape(Bs, P, W_A), cache_fox_v[l].reshape(Bs, P, W_A), c_s,
                          Tp // Ls, Bs, Ls, P)

        bo_p, ret_p = retention(h, ret_gn_w[l], zeros_ret, pos_p, 0, B, S, lc)
        bo_s, ret_s = retention(h, ret_gn_w[l], _pack_ret_state(state_ret[l].astype(f32)), pos_s,
                                Tp // Ls, Bs, Ls, Ls)

        ssd_prm = (conv_w[l], conv_b[l], dt_bias[l], a_log[l], d_skip[l], ssm_norm_w[l])
        co_p, ssm_p = ssd(h, *ssd_prm, zeros_hist, zeros_ssm, 0, B, S, lc)
        hist_s = jnp.pad(state_conv[l].astype(f32), ((0, 0), (SUBLANES - (CONV_W - 1), 0), (0, 0)))
        co_s, ssm_s = ssd(h, *ssd_prm, hist_s, state_ssm[l].astype(f32).reshape(Bs, H_C // 2, LANES, LANES),
                          Tp // Ls, Bs, Ls, Ls)

        cat = lambda a, b: jnp.concatenate([a, b], axis=0)
        x = outproj_ln(x, cat(ao_p, ao_s), cat(bo_p, bo_s), cat(co_p, co_s), w_out[l].astype(bf16),
                       ln1_g[l], ln1_b[l], tm)

        mkv = matmul(mem_prompt.reshape(B * NM, D), wkv_mem[l].astype(bf16), _pick(B * NM, (512, 256, 128)),
                     _pick(2 * W_M, (512, 256)), "mem_kv")
        mk_p = mkv[:, :W_M].reshape(B, NM, W_M)
        mv_p = mkv[:, W_M:].reshape(B, NM, W_M)
        wq_b, wo_b = wq_mem[l].astype(bf16), wo_mem[l].astype(bf16)
        xx = mem_attn_ln(x, wq_b, mk_p, mv_p, wo_b, ln2_g[l], ln2_b[l], 0, B, S, tr_mem)
        x, xb = mem_attn_ln(x, wq_b, cache_mem_k[l].reshape(Bs, NM, W_M), cache_mem_v[l].reshape(Bs, NM, W_M),
                            wo_b, ln2_g[l], ln2_b[l], Tp // Ls, Bs, Ls, Ls, into=xx)

        sT = peer_scores(x, peer_wq[l].astype(bf16), peer_k1[l], peer_k2[l], tm)
        r2, g2, cnt, g1 = peer_topk(sT, tm_peer)
        pe_t = peer_main(xb, peer_u_b, peer_vt_b, l, r2, g2, cnt, g1, tm_peer)
        x = ln_residual_t(x, pe_t, ln3_g[l], ln3_b[l], tm)

        hs = h[Tp:].reshape(Bs, Ls, NH)
        conv_p = jnp.stack([h[(b + 1) * S - (CONV_W - 1):(b + 1) * S, XBC0:XBC0 + CONV_DIM] for b in range(B)])
        new_p = (h[:Tp, FK0:FK0 + W_A].reshape(B, S, H_A, DH_A), h[:Tp, FV0:FV0 + W_A].reshape(B, S, H_A, DH_A),
                 logf_p, _unpack_ret_state(ret_p), ssm_p.reshape(B, H_C, P_C, N_C), conv_p,
                 mk_p.reshape(B, NM, H_M, DH_M), mv_p.reshape(B, NM, H_M, DH_M))
        new_s = (hs[..., FK0:FK0 + W_A].reshape(Bs, Ls, H_A, DH_A), hs[..., FV0:FV0 + W_A].reshape(Bs, Ls, H_A, DH_A),
                 logf_s, _unpack_ret_state(ret_s), ssm_s.reshape(Bs, H_C, P_C, N_C),
                 hs[:, Ls - (CONV_W - 1):, XBC0:XBC0 + CONV_DIM])
        for j, a in enumerate(new_p):
            st_p[j].append(a)
        for j, a in enumerate(new_s):
            st_s[j].append(a)

    outs_p = [jnp.stack(a) for a in st_p]
    outs_s = [jnp.stack(a) for a in st_s]
    return (x[:Tp].reshape(B, S, D), x[Tp:].reshape(Bs, Ls, D), *outs_p, *outs_s)
```

```python
import functools
import math

import numpy as np
import jax
import jax.numpy as jnp
from jax import lax
from jax.experimental import pallas as pl
from jax.experimental.pallas import tpu as pltpu

f32 = jnp.float32
bf16 = jnp.bfloat16
NEG_INF = float("-inf")

LN_EPS = 1e-5
DEPTH = 4
ALPHA = (2 * DEPTH) ** 0.25
H_A, DH_A = 8, 64
H_B, DK_B = 8, 64
H_C, P_C, N_C, G_C = 16, 64, 128, 2
W_A, W_B, W_C = 512, 512, 1024
CONV_W = 4
CONV_DIM = W_C + 2 * G_C * N_C
ROPE_BASE = 10000.0
H_M, DH_M = 4, 128
W_M = H_M * DH_M
PEER_NK, PEER_HEADS, PEER_DK, PEER_TOPK = 128, 8, 256, 16

LANES = 128
SUBLANES = 8
BF16_ROWS = 16
HALF = 64

XBC0, FQ0, Z0, FK0, FV0, RQ0, RK0, RV0, RG0, TAIL0, NH = 0, 1536, 2048, 3072, 3584, 4096, 4608, 5120, 5632, 6144, 6400
FF_LANE0, DT_LANE0 = 0, 8


def _cparams(n_axes, vmem_mb=None):
    kw = dict(dimension_semantics=("arbitrary",) * n_axes)
    if vmem_mb is not None:
        kw["vmem_limit_bytes"] = vmem_mb << 20
    return pltpu.CompilerParams(**kw)


def _pick(n, prefs):
    for p in prefs:
        if n % p == 0:
            return p
    return n


def _dot(a, b):
    return jnp.dot(a, b, preferred_element_type=f32)


def _dot_nt(a, b):
    return lax.dot_general(a, b, (((1,), (1,)), ((), ())), preferred_element_type=f32)


def _dot_tn(a, b):
    return lax.dot_general(a, b, (((0,), (0,)), ((), ())), preferred_element_type=f32)


def _ln(x, g, b):
    mu = jnp.mean(x, -1, keepdims=True)
    xc = x - mu
    var = jnp.mean(xc * xc, -1, keepdims=True)
    return xc * lax.rsqrt(var + LN_EPS) * g + b


def _silu(x):
    return x * jax.nn.sigmoid(x)


def _softplus(x):
    return jnp.maximum(x, 0.0) + jnp.log1p(jnp.exp(-jnp.abs(x)))


def _ln_kernel(x_ref, g_ref, b_ref, o_ref):
    o_ref[...] = _ln(x_ref[...], g_ref[...], b_ref[...])


def layer_norm_rows(x, g, b, tm):
    T, D = x.shape
    return pl.pallas_call(
        _ln_kernel, grid=(T // tm,),
        in_specs=[pl.BlockSpec((tm, D), lambda i: (i, 0)),
                  pl.BlockSpec((1, D), lambda i: (0, 0)),
                  pl.BlockSpec((1, D), lambda i: (0, 0))],
        out_specs=pl.BlockSpec((tm, D), lambda i: (i, 0)),
        out_shape=jax.ShapeDtypeStruct((T, D), f32),
        compiler_params=_cparams(1, 48), name="ln_in")(x, g.reshape(1, D), b.reshape(1, D))


def _ln_res_kernel(x_ref, rt_ref, g_ref, b_ref, o_ref):
    o_ref[...] = _ln(ALPHA * x_ref[...] + rt_ref[...].T, g_ref[...], b_ref[...])


def ln_residual_t(x, rt, g, b, tm):
    T, D = x.shape
    return pl.pallas_call(
        _ln_res_kernel, grid=(T // tm,),
        in_specs=[pl.BlockSpec((tm, D), lambda i: (i, 0)),
                  pl.BlockSpec((D, tm), lambda i: (0, i)),
                  pl.BlockSpec((1, D), lambda i: (0, 0)),
                  pl.BlockSpec((1, D), lambda i: (0, 0))],
        out_specs=pl.BlockSpec((tm, D), lambda i: (i, 0)),
        out_shape=jax.ShapeDtypeStruct((T, D), f32),
        compiler_params=_cparams(1, 48), name="ln_res")(x, rt, g.reshape(1, D), b.reshape(1, D))


def _mm_kernel(x_ref, w_ref, o_ref, xb_ref):
    @pl.when(pl.program_id(1) == 0)
    def _():
        xb_ref[...] = x_ref[...].astype(bf16)

    o_ref[...] = _dot(xb_ref[...], w_ref[...]).astype(o_ref.dtype)


def matmul(x, w, tm, tn, name, layer=None):
    T, K = x.shape
    N = w.shape[-1]
    if layer is None:
        w_spec = pl.BlockSpec((K, tn), lambda i, j: (0, j))
    else:
        w_spec = pl.BlockSpec((None, K, tn), lambda i, j: (layer, 0, j))
    return pl.pallas_call(
        _mm_kernel, grid=(T // tm, N // tn),
        in_specs=[pl.BlockSpec((tm, K), lambda i, j: (i, 0)), w_spec],
        out_specs=pl.BlockSpec((tm, tn), lambda i, j: (i, j)),
        out_shape=jax.ShapeDtypeStruct((T, N), f32),
        scratch_shapes=[pltpu.VMEM((tm, K), bf16)],
        compiler_params=_cparams(2, 52), name=name)(x, w)


def _gate_kernel(t_ref, fb_ref, o_ref):
    x = t_ref[...] + fb_ref[...]
    o_ref[...] = jnp.minimum(x, 0.0) - jnp.log1p(jnp.exp(-jnp.abs(x)))


def forget_gate(h, fb_row, tm):
    T = h.shape[0]
    return pl.pallas_call(
        _gate_kernel, grid=(T // tm,),
        in_specs=[pl.BlockSpec((tm, LANES), lambda i: (i, TAIL0 // LANES)),
                  pl.BlockSpec((1, LANES), lambda i: (0, 0))],
        out_specs=pl.BlockSpec((tm, LANES), lambda i: (i, 0)),
        out_shape=jax.ShapeDtypeStruct((T, LANES), f32),
        compiler_params=_cparams(1), name="forget_gate")(h, fb_row)


def _cumsum_kernel(x_ref, o_ref, carry_ref):
    @pl.when(pl.program_id(1) == 0)
    def _():
        carry_ref[...] = jnp.zeros_like(carry_ref)

    bl = x_ref.shape[-1]
    r = lax.broadcasted_iota(jnp.int32, (bl, bl), 0)
    c = lax.broadcasted_iota(jnp.int32, (bl, bl), 1)
    upper = (r <= c).astype(f32)
    y = jnp.dot(x_ref[0], upper, precision=lax.Precision.HIGHEST,
                preferred_element_type=f32) + carry_ref[:, 0:1]
    o_ref[0] = y
    carry_ref[...] = jnp.broadcast_to(y[:, bl - 1:bl], carry_ref.shape)


def cumsum_lanes(x, bl):
    n, r, L = x.shape
    return pl.pallas_call(
        _cumsum_kernel, grid=(n, L // bl),
        in_specs=[pl.BlockSpec((1, r, bl), lambda s, j: (s, 0, j))],
        out_specs=pl.BlockSpec((1, r, bl), lambda s, j: (s, 0, j)),
        out_shape=jax.ShapeDtypeStruct((n, r, L), f32),
        scratch_shapes=[pltpu.VMEM((r, LANES), f32)],
        compiler_params=_cparams(2), name="cumsum")(x)


def _cumsum_rows_kernel(x_ref, o_ref, carry_ref):
    @pl.when(pl.program_id(1) == 0)
    def _():
        carry_ref[...] = jnp.zeros_like(carry_ref)

    bl = x_ref.shape[0]
    r = lax.broadcasted_iota(jnp.int32, (bl, bl), 0)
    c = lax.broadcasted_iota(jnp.int32, (bl, bl), 1)
    lower = (c <= r).astype(f32)
    y = jnp.dot(lower, x_ref[...], precision=lax.Precision.HIGHEST,
                preferred_element_type=f32) + carry_ref[0:1, :]
    o_ref[...] = y
    carry_ref[...] = jnp.broadcast_to(y[bl - 1:bl, :], carry_ref.shape)


def cumsum_rows(x, nseq, L, bl):
    nb = L // bl
    return pl.pallas_call(
        _cumsum_rows_kernel, grid=(nseq, nb),
        in_specs=[pl.BlockSpec((bl, LANES), lambda s, j: (s * nb + j, 0))],
        out_specs=pl.BlockSpec((bl, LANES), lambda s, j: (s * nb + j, 0)),
        out_shape=jax.ShapeDtypeStruct((nseq * L, LANES), f32),
        scratch_shapes=[pltpu.VMEM((SUBLANES, LANES), f32)],
        compiler_params=_cparams(2), name="cumsum_rows")(x)


_BIAS_TERMS = 3


def _split_heads(q, lane):
    zero = jnp.zeros_like(q)
    return jnp.where(lane, q, zero), jnp.where(lane, zero, q)


def _pack_queries(q, lane_i):
    qs = q * (DH_A ** -0.5)
    q0 = jnp.where(lane_i < HALF, qs, jnp.where(lane_i < HALF + _BIAS_TERMS, 1.0, 0.0))
    q1 = jnp.where(lane_i >= HALF, qs, jnp.where(lane_i < _BIAS_TERMS, 1.0, 0.0))
    return q0.astype(bf16), q1.astype(bf16)


def _pack_keys(k, c0, c1, lane_i):
    def one(own, c, base):
        out = jnp.where(own, k, 0.0)
        rest = -c
        for t in range(_BIAS_TERMS):
            term = rest.astype(bf16).astype(f32)
            out = jnp.where(lane_i == base + t, term, out)
            rest = rest - term
        return out.astype(bf16)
    return one(lane_i < HALF, c0, HALF), one(lane_i >= HALF, c1, 0)


def _pack_values(v, lane):
    return jnp.where(lane, v, 1.0).astype(bf16), jnp.where(lane, 1.0, v).astype(bf16)


def _attn_step(qs, ks, vs, carry, mask):
    out = []
    for q, k, v, (m, acc) in zip(qs, ks, vs, carry):
        s = _dot_nt(q, k)
        if mask is not None:
            s = jnp.where(mask, s, NEG_INF)
        mn = jnp.maximum(m, jnp.max(s, -1, keepdims=True))
        p = jnp.exp(s - mn).astype(bf16)
        out.append((mn, acc * jnp.exp(m - mn) + _dot(p, v)))
    return tuple(out)


def _attn_init(tq):
    return tuple((jnp.full((tq, 1), NEG_INF, f32), jnp.zeros((tq, LANES), f32)) for _ in range(2))


def _attn_finish(carry, lane):
    (_, acc0), (_, acc1) = carry
    o0 = acc0 / pltpu.roll(acc0, HALF, 1)
    o1 = acc1 / pltpu.roll(acc1, HALF, 1)
    return jnp.where(lane, o0, o1)


def _fox_prompt_kernel(q_ref, k_ref, v_ref, c_ref, o_ref, k0_ref, k1_ref, v0_ref, v1_ref, *, tq, tk):
    i = pl.program_id(2)
    per = tq // tk
    lane_k = lax.broadcasted_iota(jnp.int32, (tk, LANES), 1)
    lane_q = lax.broadcasted_iota(jnp.int32, (tq, LANES), 1)

    @pl.when(i == 0)
    def _():
        def pack(j, _):
            rows = pl.ds(pl.multiple_of(j * tk, tk), tk)
            c = c_ref[0, 0, rows, :]
            k0, k1 = _pack_keys(k_ref[rows, :], c[:, 0:1], c[:, 1:2], lane_k)
            v0, v1 = _pack_values(v_ref[rows, :], lane_k < HALF)
            k0_ref[rows, :] = k0
            k1_ref[rows, :] = k1
            v0_ref[rows, :] = v0
            v1_ref[rows, :] = v1
            return 0

        lax.fori_loop(0, k_ref.shape[0] // tk, pack, 0)

    qs = _pack_queries(q_ref[...], lane_q)

    def step(j, carry, mask):
        rows = pl.ds(pl.multiple_of(j * tk, tk), tk)
        return _attn_step(qs, (k0_ref[rows, :], k1_ref[rows, :]), (v0_ref[rows, :], v1_ref[rows, :]), carry, mask)

    group = 2 if per % 2 == 0 else 1

    def body(g, carry):
        for u in range(group):
            carry = step(g * group + u, carry, None)
        return carry

    carry = lax.fori_loop(0, i * (per // group), body, _attn_init(tq))
    r = lax.broadcasted_iota(jnp.int32, (tq, tk), 0)
    c = lax.broadcasted_iota(jnp.int32, (tq, tk), 1)
    for d in range(per):
        carry = step(i * per + d, carry, c + d * tk <= r)
    o_ref[...] = _attn_finish(carry, lane_q < HALF).astype(o_ref.dtype)


def fox_prompt(h, cp, B, S, tq, tk):
    nq = S // tq
    qb, kb, vb = FQ0 // LANES, FK0 // LANES, FV0 // LANES
    return pl.pallas_call(
        functools.partial(_fox_prompt_kernel, tq=tq, tk=tk), grid=(B, H_A // 2, nq),
        in_specs=[pl.BlockSpec((tq, LANES), lambda b, p, i: (b * nq + i, qb + p)),
                  pl.BlockSpec((S, LANES), lambda b, p, i: (b, kb + p)),
                  pl.BlockSpec((S, LANES), lambda b, p, i: (b, vb + p)),
                  pl.BlockSpec((1, 1, S, 2), lambda b, p, i: (b, p, 0, 0))],
        out_specs=pl.BlockSpec((tq, LANES), lambda b, p, i: (b * nq + i, p)),
        out_shape=jax.ShapeDtypeStruct((B * S, W_A), bf16),
        scratch_shapes=[pltpu.VMEM((S, LANES), bf16) for _ in range(4)],
        compiler_params=_cparams(3, 52), name="fox_prompt")(h, h, h, cp)


def _fox_sample_kernel(q_ref, kn_ref, vn_ref, kc_ref, vc_ref, c_ref, o_ref, m_ref, l_ref, acc_ref,
                       *, Ls, P, tk):
    j = pl.program_id(1)

    @pl.when(j == 0)
    def _():
        m_ref[...] = jnp.full(m_ref.shape, NEG_INF, f32)
        l_ref[...] = jnp.zeros_like(l_ref)
        acc_ref[...] = jnp.zeros_like(acc_ref)

    q = q_ref[...] * (DH_A ** -0.5)

    def head_step(hd, k, v, ck, mask, seq_on_lanes):
        cols = slice(hd * DH_A, (hd + 1) * DH_A)
        qh, kb, vb = q[:, cols].astype(bf16), k.astype(bf16), v.astype(bf16)
        s = (_dot(qh, kb) if seq_on_lanes else _dot_nt(qh, kb)) - ck
        if mask is not None:
            s = jnp.where(mask, s, NEG_INF)
        m = m_ref[hd]
        mn = jnp.maximum(m, jnp.max(s, -1, keepdims=True))
        a = jnp.exp(m - mn)
        p = jnp.exp(s - mn)
        pb = p.astype(bf16)
        l_ref[hd] = a * l_ref[hd] + jnp.sum(p, -1, keepdims=True)
        acc_ref[hd] = a * acc_ref[hd] + (_dot_nt(pb, vb) if seq_on_lanes else _dot(pb, vb))
        m_ref[hd] = mn

    off = pl.multiple_of(j * tk, tk)
    for hd in range(H_A):
        head_step(hd, kc_ref[hd], vc_ref[hd], c_ref[0, hd:hd + 1, pl.ds(off, tk)], None, True)

    @pl.when(j == pl.num_programs(1) - 1)
    def _():
        r = lax.broadcasted_iota(jnp.int32, (Ls, Ls), 0)
        c = lax.broadcasted_iota(jnp.int32, (Ls, Ls), 1)
        outs = []
        for hd in range(H_A):
            cols = slice(hd * DH_A, (hd + 1) * DH_A)
            head_step(hd, kn_ref[:, cols], vn_ref[:, cols], c_ref[0, hd:hd + 1, P:P + Ls], c <= r, False)
            outs.append(acc_ref[hd] / l_ref[hd])
        o_ref[...] = jnp.concatenate(outs, axis=1).astype(o_ref.dtype)


def fox_sample(h, cache_k, cache_v, layer, cT, row0, Bs, Ls, P):
    tk = _pick(P, (1024, 512, 256, 128))
    blk = lambda col0: pl.BlockSpec((Ls, W_A), lambda b, j: (row0 + b, col0 // W_A))
    cache = lambda: pl.BlockSpec((None, None, H_A, DH_A, tk), lambda b, j: (layer, b, 0, 0, j))
    return pl.pallas_call(
        functools.partial(_fox_sample_kernel, Ls=Ls, P=P, tk=tk), grid=(Bs, P // tk),
        in_specs=[blk(FQ0), blk(FK0), blk(FV0), cache(), cache(),
                  pl.BlockSpec((1, H_A, cT.shape[-1]), lambda b, j: (b, 0, 0))],
        out_specs=pl.BlockSpec((Ls, W_A), lambda b, j: (b, 0)),
        out_shape=jax.ShapeDtypeStruct((Bs * Ls, W_A), bf16),
        scratch_shapes=[pltpu.VMEM((H_A, Ls, 1), f32), pltpu.VMEM((H_A, Ls, 1), f32),
                        pltpu.VMEM((H_A, Ls, DH_A), f32)],
        compiler_params=_cparams(2, 48), name="fox_sample")(h, h, h, cache_k, cache_v, cT)


def _pair_mean(x, lane):
    s0 = jnp.sum(jnp.where(lane, x, 0.0), -1, keepdims=True)
    s1 = jnp.sum(jnp.where(lane, 0.0, x), -1, keepdims=True)
    return jnp.where(lane, s0, s1) * (1.0 / HALF)


def _rotary(x, cos, sin_signed, first_half):
    xr = jnp.where(first_half, pltpu.roll(x, LANES - HALF // 2, 1), pltpu.roll(x, HALF // 2, 1))
    return x * cos + xr * sin_signed


def _retention_kernel(q_ref, k_ref, v_ref, g_ref, cos_ref, sin_ref, dec_ref, gq_ref, gk_ref, gl_ref,
                      s0_ref, gn_ref, o_ref, s1_ref, st_ref, *, Lc):
    c = pl.program_id(1)

    @pl.when(c == 0)
    def _():
        st_ref[...] = s0_ref[0]

    lane_i = lax.broadcasted_iota(jnp.int32, (Lc, LANES), 1)
    lane = lane_i < HALF
    first_half = (lane_i % HALF) < (HALF // 2)
    cos, sin = cos_ref[...], sin_ref[...]
    sr = lax.broadcasted_iota(jnp.int32, (LANES, LANES), 0) < HALF
    sc = lax.broadcasted_iota(jnp.int32, (LANES, LANES), 1) < HALF
    for p in range(H_B // 2):
        cols = slice(p * LANES, (p + 1) * LANES)
        q = _rotary(q_ref[:, cols], cos, sin, first_half)
        k = _rotary(k_ref[:, cols], cos, sin, first_half) * (DK_B ** -0.5)
        qb, kb, vb = q.astype(bf16), k.astype(bf16), v_ref[:, cols].astype(bf16)
        q0, q1 = _split_heads(qb, lane)
        a0 = (_dot_nt(q0, kb) * dec_ref[2 * p]).astype(bf16)
        a1 = (_dot_nt(q1, kb) * dec_ref[2 * p + 1]).astype(bf16)
        intra = jnp.where(lane, _dot(a0, vb), _dot(a1, vb))
        st = st_ref[p]
        o = intra + _dot(qb, st.astype(bf16)) * gq_ref[p]
        kd = (k * gk_ref[p]).astype(bf16)
        st_ref[p] = gl_ref[p, 0:1, :] * st + jnp.where(sr == sc, _dot_tn(kd, vb), 0.0)
        mu = _pair_mean(o, lane)
        d = o - mu
        var = _pair_mean(d * d, lane)
        on = d * lax.rsqrt(var + LN_EPS) * gn_ref[:, cols]
        o_ref[:, cols] = (_silu(g_ref[:, cols]) * on).astype(o_ref.dtype)

    @pl.when(c == pl.num_programs(1) - 1)
    def _():
        s1_ref[0] = st_ref[...]


def _retention_tables(pos, Lc):
    half = DK_B // 2
    inv = ROPE_BASE ** (-jnp.arange(half, dtype=f32) / half)
    ang = pos.astype(f32)[:, None] * inv[None, :]
    cos, sin = jnp.cos(ang), jnp.sin(ang)
    cos_t = jnp.tile(cos, (1, 4))
    sin_t = jnp.tile(jnp.concatenate([-sin, sin], -1), (1, 2))
    lg = jnp.log1p(-jnp.exp2(-5.0 - jnp.arange(H_B, dtype=f32)))
    i = jnp.arange(Lc, dtype=f32)
    diff = i[:, None] - i[None, :]
    dec = jnp.exp(jnp.where((diff >= 0)[None], diff[None] * lg[:, None, None], NEG_INF))
    pair = lambda t: jnp.repeat(t.reshape(t.shape[0], H_B // 2, 2), HALF, axis=-1)
    gq = pair(jnp.exp((i[:, None] + 1.0) * lg[None, :])).transpose(1, 0, 2)
    gk = pair(jnp.exp((Lc - 1.0 - i)[:, None] * lg[None, :])).transpose(1, 0, 2)
    gl = jnp.broadcast_to(pair(jnp.exp(Lc * lg)[None, :]).transpose(1, 0, 2), (H_B // 2, SUBLANES, LANES))
    return cos_t, sin_t, dec, gq, gk, gl


def retention(h, gn_w, s0, pos, row0, nseq, L, Lc):
    nch = L // Lc
    npair = H_B // 2
    cos_t, sin_t, dec, gq, gk, gl = _retention_tables(pos, Lc)
    blk = lambda col0: pl.BlockSpec((Lc, W_B), lambda s, c: (row0 + s * nch + c, col0 // W_B))
    full = lambda shape: pl.BlockSpec(shape, lambda s, c: (0,) * len(shape))
    return pl.pallas_call(
        functools.partial(_retention_kernel, Lc=Lc), grid=(nseq, nch),
        in_specs=[blk(RQ0), blk(RK0), blk(RV0), blk(RG0),
                  pl.BlockSpec((Lc, LANES), lambda s, c: (c, 0)),
                  pl.BlockSpec((Lc, LANES), lambda s, c: (c, 0)),
                  full((H_B, Lc, Lc)), full((npair, Lc, LANES)), full((npair, Lc, LANES)),
                  full((npair, SUBLANES, LANES)),
                  pl.BlockSpec((1, npair, LANES, LANES), lambda s, c: (s, 0, 0, 0)),
                  full((1, W_B))],
        out_specs=[pl.BlockSpec((Lc, W_B), lambda s, c: (s * nch + c, 0)),
                   pl.BlockSpec((1, npair, LANES, LANES), lambda s, c: (s, 0, 0, 0))],
        out_shape=[jax.ShapeDtypeStruct((nseq * L, W_B), bf16),
                   jax.ShapeDtypeStruct((nseq, npair, LANES, LANES), f32)],
        scratch_shapes=[pltpu.VMEM((npair, LANES, LANES), f32)],
        compiler_params=_cparams(2, 32), name="retention")(
            h, h, h, h, cos_t, sin_t, dec, gq, gk, gl, s0, gn_w.reshape(1, W_B))


def _pack_ret_state(s):
    n = s.shape[0]
    s = s.reshape(n, H_B // 2, 2, DK_B, DK_B)
    z = jnp.zeros_like(s[:, :, 0])
    top = jnp.concatenate([s[:, :, 0], z], -1)
    bot = jnp.concatenate([z, s[:, :, 1]], -1)
    return jnp.concatenate([top, bot], -2)


def _unpack_ret_state(s):
    n = s.shape[0]
    return jnp.stack([s[:, :, :HALF, :HALF], s[:, :, HALF:, HALF:]], 2).reshape(n, H_B, DK_B, DK_B)


def _ssd_kernel(xbc_ref, z_ref, t_ref, cw_ref, cb_ref, dtb_ref, alog_ref, dsk_ref, nw_ref, hist_ref, h0_ref,
                o_ref, h1_ref, xpad_ref, hs_ref, *, Lc):
    c = pl.program_id(1)

    @pl.when(c == 0)
    def _():
        xpad_ref[0:SUBLANES, :] = hist_ref[0]
        hs_ref[...] = h0_ref[0]

    xpad_ref[SUBLANES:SUBLANES + Lc, :] = xbc_ref[...]
    conv = cb_ref[...]
    for t in range(CONV_W):
        r0 = SUBLANES - (CONV_W - 1) + t
        conv = conv + xpad_ref[r0:r0 + Lc, :] * cw_ref[t:t + 1, :]
    xpad_ref[0:SUBLANES, :] = xpad_ref[Lc:Lc + SUBLANES, :]
    xc = _silu(conv)
    xs = xc[:, :W_C]
    bm = [xc[:, W_C + g * N_C:W_C + (g + 1) * N_C].astype(bf16) for g in range(G_C)]
    cm = [xc[:, W_C + (G_C + g) * N_C:W_C + (G_C + g + 1) * N_C].astype(bf16) for g in range(G_C)]

    lane_i = lax.broadcasted_iota(jnp.int32, (Lc, LANES), 1)
    lane = lane_i < HALF
    dt_valid = (lane_i >= DT_LANE0) & (lane_i < DT_LANE0 + H_C)
    dt = _softplus(t_ref[...] + dtb_ref[...])
    dta = jnp.where(dt_valid, dt * (-jnp.exp(alog_ref[...])), 0.0)
    r = lax.broadcasted_iota(jnp.int32, (Lc, Lc), 0)
    cidx = lax.broadcasted_iota(jnp.int32, (Lc, Lc), 1)
    tri = cidx <= r
    a_cs = jnp.dot(tri.astype(f32), dta, precision=lax.Precision.HIGHEST, preferred_element_type=f32)
    a_cs_t = a_cs.T
    dt_t = dt.T
    row_first = lax.broadcasted_iota(jnp.int32, (LANES, LANES), 0) < HALF

    ys = []
    for p in range(H_C // 2):
        g = (2 * p) // (H_C // G_C)
        xpair = xs[:, p * LANES:(p + 1) * LANES]
        xpair_b = xpair.astype(bf16)
        if p % (H_C // G_C // 2) == 0:
            cb = _dot_nt(cm[g], bm[g])
        yh, acol, dcol = [], [], []
        for hd in (2 * p, 2 * p + 1):
            li = DT_LANE0 + hd
            ac = a_cs[:, li:li + 1]
            seg = ac - a_cs_t[li:li + 1, :]
            w = cb * jnp.exp(jnp.where(tri, seg, NEG_INF)) * dt_t[li:li + 1, :]
            yh.append(_dot(w.astype(bf16), xpair_b))
            acol.append(ac)
            dcol.append(dt[:, li:li + 1])
        acs_pair = jnp.where(lane, acol[0], acol[1])
        dt_pair = jnp.where(lane, dcol[0], dcol[1])
        hs = hs_ref[p]
        y = jnp.where(lane, yh[0], yh[1]) + _dot_nt(cm[g], hs.astype(bf16)) * jnp.exp(acs_pair)
        ys.append(y)
        a_last = acs_pair[Lc - 1:Lc, :]
        to_end = jnp.exp(a_last - acs_pair) * dt_pair
        upd = _dot_tn((xpair * to_end).astype(bf16), bm[g])
        sdec = jnp.exp(jnp.where(row_first, acol[0][Lc - 1:Lc, :], acol[1][Lc - 1:Lc, :]))
        hs_ref[p] = sdec * hs + upd

    y = jnp.concatenate(ys, axis=1) + dsk_ref[...] * xs
    y = y * _silu(z_ref[...])
    y = y * lax.rsqrt(jnp.mean(y * y, -1, keepdims=True) + LN_EPS) * nw_ref[...]
    o_ref[...] = y.astype(o_ref.dtype)

    @pl.when(c == pl.num_programs(1) - 1)
    def _():
        h1_ref[0] = hs_ref[...]


def ssd(h, conv_w, conv_b, dt_bias, a_log, d_skip, norm_w, hist, h0, row0, nseq, L, Lc):
    nch = L // Lc
    lane_row = lambda v: jnp.zeros((1, LANES), f32).at[0, DT_LANE0:DT_LANE0 + H_C].set(v)
    cw = jnp.zeros((SUBLANES, CONV_DIM), f32).at[:CONV_W].set(conv_w)
    row = lambda w: pl.BlockSpec((1, w), lambda s, c: (0, 0))
    return pl.pallas_call(
        functools.partial(_ssd_kernel, Lc=Lc), grid=(nseq, nch),
        in_specs=[pl.BlockSpec((Lc, CONV_DIM), lambda s, c: (row0 + s * nch + c, XBC0 // CONV_DIM)),
                  pl.BlockSpec((Lc, W_C), lambda s, c: (row0 + s * nch + c, Z0 // W_C)),
                  pl.BlockSpec((Lc, LANES), lambda s, c: (row0 + s * nch + c, TAIL0 // LANES)),
                  pl.BlockSpec((SUBLANES, CONV_DIM), lambda s, c: (0, 0)),
                  row(CONV_DIM), row(LANES), row(LANES), row(W_C), row(W_C),
                  pl.BlockSpec((1, SUBLANES, CONV_DIM), lambda s, c: (s, 0, 0)),
                  pl.BlockSpec((1, H_C // 2, LANES, LANES), lambda s, c: (s, 0, 0, 0))],
        out_specs=[pl.BlockSpec((Lc, W_C), lambda s, c: (s * nch + c, 0)),
                   pl.BlockSpec((1, H_C // 2, LANES, LANES), lambda s, c: (s, 0, 0, 0))],
        out_shape=[jax.ShapeDtypeStruct((nseq * L, W_C), bf16),
                   jax.ShapeDtypeStruct((nseq, H_C // 2, LANES, LANES), f32)],
        scratch_shapes=[pltpu.VMEM((Lc + SUBLANES, CONV_DIM), f32),
                        pltpu.VMEM((H_C // 2, LANES, LANES), f32)],
        compiler_params=_cparams(2, 48), name="ssd")(
            h, h, h, cw, conv_b.reshape(1, CONV_DIM), lane_row(dt_bias), lane_row(a_log),
            jnp.repeat(d_skip, P_C).reshape(1, W_C), norm_w.reshape(1, W_C), hist, h0)


def _outproj_kernel(x_ref, a_ref, b_ref, c_ref, w_ref, g_ref, beta_ref, o_ref):
    mix = (_dot(a_ref[...], w_ref[0:W_A, :]) + _dot(b_ref[...], w_ref[W_A:W_A + W_B, :])
           + _dot(c_ref[...], w_ref[W_A + W_B:, :]))
    o_ref[...] = _ln(ALPHA * x_ref[...] + mix, g_ref[...], beta_ref[...])


def outproj_ln(x, ao, bo, co, w, g, b, tm):
    T, D = x.shape
    rows = lambda wd: pl.BlockSpec((tm, wd), lambda i: (i, 0))
    const = lambda s: pl.BlockSpec(s, lambda i: (0, 0))
    return pl.pallas_call(
        _outproj_kernel, grid=(T // tm,),
        in_specs=[rows(D), rows(W_A), rows(W_B), rows(W_C), const(w.shape), const((1, D)), const((1, D))],
        out_specs=rows(D),
        out_shape=jax.ShapeDtypeStruct((T, D), f32),
        compiler_params=_cparams(1, 52), name="outproj_ln")(x, ao, bo, co, w, g.reshape(1, D), b.reshape(1, D))


def _mem_body(x_ref, wq_ref, mk_ref, mv_ref, wo_ref, g_ref, b_ref, o_ref, ob_ref):
    x = x_ref[...]
    q = _dot(x.astype(bf16), wq_ref[...])
    outs = []
    for hd in range(H_M):
        sl = slice(hd * DH_M, (hd + 1) * DH_M)
        s = _dot_nt(q[:, sl].astype(bf16), mk_ref[0, :, sl].astype(bf16)) * (DH_M ** -0.5)
        p = jnp.exp(s - jnp.max(s, -1, keepdims=True))
        p = p / jnp.sum(p, -1, keepdims=True)
        outs.append(_dot(p.astype(bf16), mv_ref[0, :, sl].astype(bf16)))
    o = jnp.concatenate(outs, axis=1).astype(bf16)
    y = _ln(ALPHA * x + _dot(o, wo_ref[...]), g_ref[...], b_ref[...])
    o_ref[...] = y
    ob_ref[...] = y.astype(bf16)


def _mem_kernel_first(x_ref, wq_ref, mk_ref, mv_ref, wo_ref, g_ref, b_ref, o_ref, ob_ref):
    _mem_body(x_ref, wq_ref, mk_ref, mv_ref, wo_ref, g_ref, b_ref, o_ref, ob_ref)


def _mem_kernel_into(x_ref, wq_ref, mk_ref, mv_ref, wo_ref, g_ref, b_ref, prev_ref, prevb_ref, o_ref, ob_ref):
    del prev_ref, prevb_ref
    _mem_body(x_ref, wq_ref, mk_ref, mv_ref, wo_ref, g_ref, b_ref, o_ref, ob_ref)


def mem_attn_ln(x, wq, mk, mv, wo, g, b, row0, nseq, L, tr, into=None):
    T, D = x.shape
    per = L // tr
    nm = mk.shape[1]
    const = lambda s: pl.BlockSpec(s, lambda i: (0,) * len(s))
    in_specs = [pl.BlockSpec((tr, D), lambda i: (row0 + i, 0)), const(wq.shape),
                pl.BlockSpec((1, nm, W_M), lambda i: (i // per, 0, 0)),
                pl.BlockSpec((1, nm, W_M), lambda i: (i // per, 0, 0)),
                const(wo.shape), const((1, D)), const((1, D))]
    args = [x, wq, mk, mv, wo, g.reshape(1, D), b.reshape(1, D)]
    aliases = {}
    if into is not None:
        in_specs += [pl.BlockSpec(memory_space=pl.ANY), pl.BlockSpec(memory_space=pl.ANY)]
        aliases = {len(args): 0, len(args) + 1: 1}
        args += list(into)
    return pl.pallas_call(
        _mem_kernel_first if into is None else _mem_kernel_into, grid=(nseq * per,),
        in_specs=in_specs,
        out_specs=[pl.BlockSpec((tr, D), lambda i: (row0 + i, 0)), pl.BlockSpec((tr, D), lambda i: (row0 + i, 0))],
        out_shape=[jax.ShapeDtypeStruct((T, D), f32), jax.ShapeDtypeStruct((T, D), bf16)],
        input_output_aliases=aliases,
        compiler_params=_cparams(1, 48), name="mem_attn_ln")(*args)


def _peer_score_kernel(x_ref, wq_ref, k1_ref, k2_ref, o_ref):
    q = _dot(x_ref[...].astype(bf16), wq_ref[...])
    k1 = k1_ref[...].astype(bf16)
    k2 = k2_ref[...].astype(bf16)
    hk = PEER_DK // 2
    for hd in range(PEER_HEADS):
        q1 = q[:, hd * PEER_DK:hd * PEER_DK + hk].astype(bf16)
        q2 = q[:, hd * PEER_DK + hk:(hd + 1) * PEER_DK].astype(bf16)
        o_ref[hd, 0:PEER_NK, :] = _dot_nt(k1, q1)
        o_ref[hd, PEER_NK:2 * PEER_NK, :] = _dot_nt(k2, q2)


def peer_scores(x, wq, k1, k2, tm):
    T, D = x.shape
    const = lambda s: pl.BlockSpec(s, lambda i: (0, 0))
    return pl.pallas_call(
        _peer_score_kernel, grid=(T // tm,),
        in_specs=[pl.BlockSpec((tm, D), lambda i: (i, 0)), const(wq.shape), const(k1.shape), const(k2.shape)],
        out_specs=pl.BlockSpec((PEER_HEADS, 2 * PEER_NK, tm), lambda i: (0, 0, i)),
        out_shape=jax.ShapeDtypeStruct((PEER_HEADS, 2 * PEER_NK, T), f32),
        compiler_params=_cparams(1, 52), name="peer_scores")(x, wq, k1, k2)


_CAND_ROWS = PEER_TOPK + 7 * SUBLANES + SUBLANES


def _cand_flat_index(cw):
    idx = [0 * PEER_TOPK + b for b in range(PEER_TOPK)]
    for a in range(1, 8):
        idx += [a * PEER_TOPK + b for b in range(SUBLANES)]
    idx += [a * PEER_TOPK for a in range(8, PEER_TOPK)]
    return np.broadcast_to(np.asarray(idx, np.float32)[:, None], (_CAND_ROWS, cw)).copy()


def _extract_top(s, key_idx, n):
    rank = jnp.full(s.shape, float(n), f32)
    vals = []
    for a in range(n):
        mx = jnp.max(s, axis=0, keepdims=True)
        first = jnp.min(jnp.where(s == mx, key_idx, float(1 << 20)), axis=0, keepdims=True)
        sel = key_idx == first
        rank = jnp.where(sel, float(a), rank)
        s = jnp.where(sel, NEG_INF, s)
        vals.append(mx)
    return vals, rank


def _top_values(s, n, with_rank):
    rank = jnp.full(s.shape, float(n), f32) if with_rank else None
    vals = []
    for a in range(n):
        mx = jnp.max(s, axis=0, keepdims=True)
        hit = s == mx
        if with_rank:
            rank = jnp.where(hit, float(a), rank)
        s = jnp.where(hit, NEG_INF, s)
        vals.append(mx)
    return vals, rank


def _candidates(v1, v2):
    v1a = jnp.concatenate(v1, axis=0)
    v2a = jnp.concatenate(v2, axis=0)
    return jnp.concatenate([v1[0] + v2a] + [v1[a] + v2a[0:SUBLANES] for a in range(1, 8)]
                           + [v1a[SUBLANES:] + v2[0]], axis=0)


def _counts_per_rank(picked):
    cnt_a = [jnp.sum(picked[0:PEER_TOPK], axis=0, keepdims=True)]
    for a in range(1, 8):
        lo = PEER_TOPK + (a - 1) * SUBLANES
        cnt_a.append(jnp.sum(picked[lo:lo + SUBLANES], axis=0, keepdims=True))
    tail = picked[PEER_TOPK + 7 * SUBLANES:]
    return cnt_a + [tail[a:a + 1] for a in range(SUBLANES)]


def _select_fast(s1, s2):
    v1, _ = _top_values(s1, PEER_TOPK, False)
    v2, rank2 = _top_values(s2, PEER_TOPK, True)
    cand = _candidates(v1, v2)
    vc, _ = _top_values(cand, PEER_TOPK, False)
    top = v1[0] + v2[0]
    zsum = jnp.exp(vc[0] - top)
    for k in range(1, PEER_TOPK):
        zsum = zsum + jnp.exp(vc[k] - top)
    picked = jnp.where(cand >= vc[PEER_TOPK - 1], 1.0, 0.0)
    cnt_a = _counts_per_rank(picked)
    cnt = jnp.zeros(s1.shape, f32)
    for a in range(PEER_TOPK):
        cnt = jnp.where(s1 == v1[a], cnt_a[a], cnt)
    n1 = jnp.sum(jnp.where(s1 >= v1[PEER_TOPK - 1], 1.0, 0.0), axis=0, keepdims=True)
    n2 = jnp.sum(jnp.where(rank2 < float(PEER_TOPK), 1.0, 0.0), axis=0, keepdims=True)
    nc = jnp.sum(picked, axis=0, keepdims=True)
    k = float(PEER_TOPK)
    distinct = (n1 == k) & (n2 == k) & (nc == k)
    tie = jnp.max(jnp.where(distinct, 0.0, 1.0)) > 0.0
    return rank2, cnt, zsum, v1[0], v2[0], tie


def _select_exact(s1, s2, key_idx, cidx):
    v1, rank1 = _extract_top(s1, key_idx, PEER_TOPK)
    v2, rank2 = _extract_top(s2, key_idx, PEER_TOPK)
    cand = _candidates(v1, v2)
    top = v1[0] + v2[0]
    picked = jnp.zeros(cand.shape, f32)
    zsum = jnp.zeros((1, s1.shape[1]), f32)
    for _k in range(PEER_TOPK):
        mx = jnp.max(cand, axis=0, keepdims=True)
        first = jnp.min(jnp.where(cand == mx, cidx, float(1 << 20)), axis=0, keepdims=True)
        sel = cidx == first
        picked = jnp.where(sel, 1.0, picked)
        cand = jnp.where(sel, NEG_INF, cand)
        zsum = zsum + jnp.exp(mx - top)
    cnt_a = _counts_per_rank(picked)
    cnt = jnp.zeros(s1.shape, f32)
    for a in range(PEER_TOPK):
        cnt = jnp.where(rank1 == float(a), cnt_a[a], cnt)
    return rank2, cnt, zsum


def _peer_topk_kernel(s_ref, cidx_ref, r2_ref, g2_ref, cnt_ref, g1_ref, *, tb, cw):
    nj = PEER_NK // SUBLANES

    def chunk(ci, _):
        off = pl.multiple_of(ci * cw, cw)
        s1 = s_ref[0, 0:PEER_NK, pl.ds(off, cw)]
        s2 = s_ref[0, PEER_NK:2 * PEER_NK, pl.ds(off, cw)]
        rank2, cnt, zsum, m1, m2, tie = _select_fast(s1, s2)
        e1 = jnp.exp(s1 - m1)
        r2_ref[0, :, pl.ds(off, cw)] = rank2.astype(bf16)
        g2_ref[0, :, pl.ds(off, cw)] = jnp.exp(s2 - m2).astype(bf16)
        cnt_ref[:, 0, :, pl.ds(off, cw)] = cnt.reshape(nj, SUBLANES, cw)
        g1_ref[:, 0, :, pl.ds(off, cw)] = (e1 / zsum).reshape(nj, SUBLANES, cw)

        @pl.when(tie)
        def _():
            key_idx = lax.broadcasted_iota(jnp.int32, (PEER_NK, cw), 0).astype(f32)
            rank2x, cntx, zsumx = _select_exact(s1, s2, key_idx, cidx_ref[...])
            r2_ref[0, :, pl.ds(off, cw)] = rank2x.astype(bf16)
            cnt_ref[:, 0, :, pl.ds(off, cw)] = cntx.reshape(nj, SUBLANES, cw)
            g1_ref[:, 0, :, pl.ds(off, cw)] = (e1 / zsumx).reshape(nj, SUBLANES, cw)

        return 0

    lax.fori_loop(0, tb // cw, chunk, 0)


def peer_topk(sT, tb):
    T = sT.shape[-1]
    nj = PEER_NK // SUBLANES
    cw = _pick(tb, (2 * LANES, LANES))
    per_key = lambda: pl.BlockSpec((1, PEER_NK, tb), lambda hd, i: (hd, 0, i))
    per_row = lambda: pl.BlockSpec((nj, 1, SUBLANES, tb), lambda hd, i: (0, hd, 0, i))
    r2, g2, cnt, g1 = pl.pallas_call(
        functools.partial(_peer_topk_kernel, tb=tb, cw=cw), grid=(PEER_HEADS, T // tb),
        in_specs=[pl.BlockSpec((1, 2 * PEER_NK, tb), lambda hd, i: (hd, 0, i)),
                  pl.BlockSpec((_CAND_ROWS, cw), lambda hd, i: (0, 0))],
        out_specs=[per_key(), per_key(), per_row(), per_row()],
        out_shape=[jax.ShapeDtypeStruct((PEER_HEADS, PEER_NK, T), bf16),
                   jax.ShapeDtypeStruct((PEER_HEADS, PEER_NK, T), bf16),
                   jax.ShapeDtypeStruct((nj, PEER_HEADS, SUBLANES, T), f32),
                   jax.ShapeDtypeStruct((nj, PEER_HEADS, SUBLANES, T), f32)],
        compiler_params=_cparams(2, 32), name="peer_topk")(sT, jnp.asarray(_cand_flat_index(cw)))
    packed = (PEER_HEADS, PEER_NK // BF16_ROWS, BF16_ROWS, T)
    return (r2.reshape(packed), g2.reshape(packed),
            cnt.reshape(nj, PEER_HEADS * SUBLANES, T), g1.reshape(nj, PEER_HEADS * SUBLANES, T))


def _peer_main_kernel(x_ref, u_ref, vt_ref, r2_ref, g2_ref, cnt_ref, g1_ref, o_ref):
    j = pl.program_id(1)
    tm = x_ref.shape[0]

    @pl.when(j == 0)
    def _():
        o_ref[...] = jnp.zeros_like(o_ref)

    a = _dot_nt(u_ref[...], x_ref[...])
    act = (0.5 * a * (1.0 + lax.erf(a * (2.0 ** -0.5)))).astype(bf16)
    zero = jnp.zeros((), bf16)
    parts = []
    for r in range(SUBLANES):
        w = None
        for hd in range(PEER_HEADS):
            row = hd * SUBLANES + r
            cnt = jnp.broadcast_to(cnt_ref[0, row:row + 1, :], (BF16_ROWS, tm)).astype(bf16)[None]
            g1 = jnp.broadcast_to(g1_ref[0, row:row + 1, :], (BF16_ROWS, tm)).astype(bf16)[None]
            sel = jnp.where(r2_ref[hd] < cnt, g2_ref[hd], zero) * g1
            w = sel if w is None else w + sel
        parts.append(w.reshape(PEER_NK, tm) * act[r * PEER_NK:(r + 1) * PEER_NK])
    o_ref[...] += _dot(vt_ref[...], jnp.concatenate(parts, axis=0))


def peer_main(xb, ub, vtb, layer, r2, g2, cnt, g1, tm):
    T, D = xb.shape
    te = SUBLANES * PEER_NK
    nj = PEER_NK // SUBLANES
    once = pl.Buffered(1)
    per_tok = lambda: pl.BlockSpec((PEER_HEADS, PEER_NK // BF16_ROWS, BF16_ROWS, tm),
                                   lambda i, j: (0, 0, 0, i), pipeline_mode=once)
    per_row = lambda: pl.BlockSpec((1, PEER_HEADS * SUBLANES, tm), lambda i, j: (j, 0, i))
    return pl.pallas_call(
        _peer_main_kernel, grid=(T // tm, nj),
        in_specs=[pl.BlockSpec((tm, D), lambda i, j: (i, 0), pipeline_mode=once),
                  pl.BlockSpec((None, te, D), lambda i, j: (layer, j, 0)),
                  pl.BlockSpec((None, D, te), lambda i, j: (layer, 0, j)),
                  per_tok(), per_tok(), per_row(), per_row()],
        out_specs=pl.BlockSpec((D, tm), lambda i, j: (0, i), pipeline_mode=once),
        out_shape=jax.ShapeDtypeStruct((D, T), f32),
        compiler_params=_cparams(2, 58), name="peer_main")(xb, ub, vtb, r2, g2, cnt, g1)


def _rearrange_w_in(w):
    o = np.cumsum([0, W_A, W_A, W_A, H_A, H_B * DK_B, H_B * DK_B, W_B, W_B, W_C, CONV_DIM, H_C])
    fq, fk, fv, ff, rq, rk, rv, rg, z, xbc, dtr = [slice(int(o[i]), int(o[i + 1])) for i in range(11)]
    parts = [w[:, s] for s in (xbc, fq, z, fk, fv, rq, rk, rv, rg, ff, dtr)]
    pad = NH - TAIL0 - H_A - H_C
    return jnp.concatenate(parts + [jnp.zeros((w.shape[0], pad), w.dtype)], axis=1).astype(bf16)


def kernel(x_prompt, x_sample, cache_fox_k, cache_fox_v, cache_fox_logf, state_ret, state_ssm, state_conv, cache_mem_k, cache_mem_v, mem_prompt, ln_in_g, ln_in_b, w_in, fox_fb, ret_gn_w, conv_w, conv_b, dt_bias, a_log, d_skip, ssm_norm_w, w_out, ln1_g, ln1_b, wq_mem, wkv_mem, wo_mem, ln2_g, ln2_b, peer_wq, peer_k1, peer_k2, peer_u, peer_v, ln3_g, ln3_b):
    B, S, D = x_prompt.shape
    Bs, Ls, _ = x_sample.shape
    P = cache_fox_k.shape[2]
    NM = mem_prompt.shape[1]
    Tp, Ts = B * S, Bs * Ls
    T = Tp + Ts
    assert Tp % Ls == 0 and T % LANES == 0 and Ls >= CONV_W - 1

    tm = _pick(T, (640, 768, 384, 256, 128))
    tm_peer = _pick(T, (1280, 768, 256, 128))
    tq = _pick(S, (1024, 512, 256, 128))
    tk = min(tq, 512)
    lc = _pick(S, (256, 128, 64))
    tr_mem = _pick(S, (512, 256, 128))
    cs_bl = 512
    tn_in = _pick(NH, (1280, 640))

    x = jnp.concatenate([x_prompt.reshape(Tp, D), x_sample.reshape(Ts, D)], axis=0)
    x = layer_norm_rows(x, ln_in_g, ln_in_b, tm)

    pos_p = jnp.arange(S)
    pos_s = P + jnp.arange(Ls)
    zeros_ret = jnp.zeros((B, H_B // 2, LANES, LANES), f32)
    zeros_ssm = jnp.zeros((B, H_C // 2, LANES, LANES), f32)
    zeros_hist = jnp.zeros((B, SUBLANES, CONV_DIM), f32)
    lp_s = -(-(P + Ls) // cs_bl) * cs_bl

    w_in_b = jax.vmap(_rearrange_w_in)(w_in)
    peer_u_b = peer_u.astype(bf16)
    cache_k_rows = cache_fox_k.transpose(0, 1, 3, 4, 2)
    cache_v_rows = cache_fox_v.transpose(0, 1, 3, 4, 2)
    peer_vt_b = peer_v.transpose(0, 2, 1).astype(bf16)

    st_p = [[] for _ in range(8)]
    st_s = [[] for _ in range(6)]
    for l in range(DEPTH):
        h = matmul(x, w_in_b, tm, tn_in, "in_proj", layer=l)
        fb_row = jnp.zeros((1, LANES), f32).at[0, FF_LANE0:FF_LANE0 + H_A].set(fox_fb[l])
        logf_pad = forget_gate(h, fb_row, tm)
        logf = logf_pad[:, FF_LANE0:FF_LANE0 + H_A]
        logf_p = logf[:Tp].reshape(B, S, H_A)
        logf_s = logf[Tp:].reshape(Bs, Ls, H_A)

        c_p = cumsum_rows(logf_pad, B, S, min(cs_bl, S))[:, FF_LANE0:FF_LANE0 + H_A]
        c_p = c_p.reshape(B, S, H_A // 2, 2).transpose(0, 2, 1, 3)
        ao_p = fox_prompt(h, c_p, B, S, tq, tk)
        lf_all = jnp.concatenate([cache_fox_logf[l].astype(f32), logf_s], axis=1).transpose(0, 2, 1)
        lf_all = jnp.pad(lf_all, ((0, 0), (0, 0), (0, lp_s - (P + Ls))))
        ao_s = fox_sample(h, cache_k_rows, cache_v_rows, l, cumsum_lanes(lf_all, cs_bl), Tp // Ls, Bs, Ls, P)

        bo_p, ret_p = retention(h, ret_gn_w[l], zeros_ret, pos_p, 0, B, S, lc)
        bo_s, ret_s = retention(h, ret_gn_w[l], _pack_ret_state(state_ret[l].astype(f32)), pos_s,
                                Tp // Ls, Bs, Ls, Ls)

        ssd_prm = (conv_w[l], conv_b[l], dt_bias[l], a_log[l], d_skip[l], ssm_norm_w[l])
        co_p, ssm_p = ssd(h, *ssd_prm, zeros_hist, zeros_ssm, 0, B, S, lc)
        hist_s = jnp.pad(state_conv[l].astype(f32), ((0, 0), (SUBLANES - (CONV_W - 1), 0), (0, 0)))
        co_s, ssm_s = ssd(h, *ssd_prm, hist_s, state_ssm[l].astype(f32).reshape(Bs, H_C // 2, LANES, LANES),
                          Tp // Ls, Bs, Ls, Ls)

        cat = lambda a, b: jnp.concatenate([a, b], axis=0)
        x = outproj_ln(x, cat(ao_p, ao_s), cat(bo_p, bo_s), cat(co_p, co_s), w_out[l].astype(bf16),
                       ln1_g[l], ln1_b[l], tm)

        mkv = matmul(mem_prompt.reshape(B * NM, D), wkv_mem[l].astype(bf16), _pick(B * NM, (512, 256, 128)),
                     _pick(2 * W_M, (512, 256)), "mem_kv")
        mk_p = mkv[:, :W_M].reshape(B, NM, W_M)
        mv_p = mkv[:, W_M:].reshape(B, NM, W_M)
        wq_b, wo_b = wq_mem[l].astype(bf16), wo_mem[l].astype(bf16)
        xx = mem_attn_ln(x, wq_b, mk_p, mv_p, wo_b, ln2_g[l], ln2_b[l], 0, B, S, tr_mem)
        x, xb = mem_attn_ln(x, wq_b, cache_mem_k[l].reshape(Bs, NM, W_M), cache_mem_v[l].reshape(Bs, NM, W_M),
                            wo_b, ln2_g[l], ln2_b[l], Tp // Ls, Bs, Ls, Ls, into=xx)

        sT = peer_scores(x, peer_wq[l].astype(bf16), peer_k1[l], peer_k2[l], tm)
        r2, g2, cnt, g1 = peer_topk(sT, tm_peer)
        pe_t = peer_main(xb, peer_u_b, peer_vt_b, l, r2, g2, cnt, g1, tm_peer)
        x = ln_residual_t(x, pe_t, ln3_g[l], ln3_b[l], tm)

        hs = h[Tp:].reshape(Bs, Ls, NH)
        conv_p = jnp.stack([h[(b + 1) * S - (CONV_W - 1):(b + 1) * S, XBC0:XBC0 + CONV_DIM] for b in range(B)])
        new_p = (h[:Tp, FK0:FK0 + W_A].reshape(B, S, H_A, DH_A), h[:Tp, FV0:FV0 + W_A].reshape(B, S, H_A, DH_A),
                 logf_p, _unpack_ret_state(ret_p), ssm_p.reshape(B, H_C, P_C, N_C), conv_p,
                 mk_p.reshape(B, NM, H_M, DH_M), mv_p.reshape(B, NM, H_M, DH_M))
        new_s = (hs[..., FK0:FK0 + W_A].reshape(Bs, Ls, H_A, DH_A), hs[..., FV0:FV0 + W_A].reshape(Bs, Ls, H_A, DH_A),
                 logf_s, _unpack_ret_state(ret_s), ssm_s.reshape(Bs, H_C, P_C, N_C),
                 hs[:, Ls - (CONV_W - 1):, XBC0:XBC0 + CONV_DIM])
        for j, a in enumerate(new_p):
            st_p[j].append(a)
        for j, a in enumerate(new_s):
            st_s[j].append(a)

    outs_p = [jnp.stack(a) for a in st_p]
    outs_s = [jnp.stack(a) for a in st_s]
    return (x[:Tp].reshape(B, S, D), x[Tp:].reshape(Bs, Ls, D), *outs_p, *outs_s)
```

```python
import functools
import math

import numpy as np
import jax
import jax.numpy as jnp
from jax import lax
from jax.experimental import pallas as pl
from jax.experimental.pallas import tpu as pltpu

f32 = jnp.float32
bf16 = jnp.bfloat16
NEG_INF = float("-inf")

LN_EPS = 1e-5
DEPTH = 4
ALPHA = (2 * DEPTH) ** 0.25
H_A, DH_A = 8, 64
H_B, DK_B = 8, 64
H_C, P_C, N_C, G_C = 16, 64, 128, 2
W_A, W_B, W_C = 512, 512, 1024
CONV_W = 4
CONV_DIM = W_C + 2 * G_C * N_C
ROPE_BASE = 10000.0
H_M, DH_M = 4, 128
W_M = H_M * DH_M
PEER_NK, PEER_HEADS, PEER_DK, PEER_TOPK = 128, 8, 256, 16

LANES = 128
SUBLANES = 8
BF16_ROWS = 16
HALF = 64

XBC0, FQ0, Z0, FK0, FV0, RQ0, RK0, RV0, RG0, TAIL0, NH = 0, 1536, 2048, 3072, 3584, 4096, 4608, 5120, 5632, 6144, 6400
FF_LANE0, DT_LANE0 = 0, 8


def _cparams(n_axes, vmem_mb=None):
    kw = dict(dimension_semantics=("arbitrary",) * n_axes)
    if vmem_mb is not None:
        kw["vmem_limit_bytes"] = vmem_mb << 20
    return pltpu.CompilerParams(**kw)


def _pick(n, prefs):
    for p in prefs:
        if n % p == 0:
            return p
    return n


def _dot(a, b):
    return jnp.dot(a, b, preferred_element_type=f32)


def _dot_nt(a, b):
    return lax.dot_general(a, b, (((1,), (1,)), ((), ())), preferred_element_type=f32)


def _dot_tn(a, b):
    return lax.dot_general(a, b, (((0,), (0,)), ((), ())), preferred_element_type=f32)


def _ln(x, g, b):
    mu = jnp.mean(x, -1, keepdims=True)
    xc = x - mu
    var = jnp.mean(xc * xc, -1, keepdims=True)
    return xc * lax.rsqrt(var + LN_EPS) * g + b


def _silu(x):
    return x * jax.nn.sigmoid(x)


def _softplus(x):
    return jnp.maximum(x, 0.0) + jnp.log1p(jnp.exp(-jnp.abs(x)))


def _ln_kernel(x_ref, g_ref, b_ref, o_ref):
    o_ref[...] = _ln(x_ref[...], g_ref[...], b_ref[...])


def layer_norm_rows(x, g, b, tm):
    T, D = x.shape
    return pl.pallas_call(
        _ln_kernel, grid=(T // tm,),
        in_specs=[pl.BlockSpec((tm, D), lambda i: (i, 0)),
                  pl.BlockSpec((1, D), lambda i: (0, 0)),
                  pl.BlockSpec((1, D), lambda i: (0, 0))],
        out_specs=pl.BlockSpec((tm, D), lambda i: (i, 0)),
        out_shape=jax.ShapeDtypeStruct((T, D), f32),
        compiler_params=_cparams(1, 48), name="ln_in")(x, g.reshape(1, D), b.reshape(1, D))


def _ln_res_kernel(x_ref, rt_ref, g_ref, b_ref, o_ref):
    o_ref[...] = _ln(ALPHA * x_ref[...] + rt_ref[...].T, g_ref[...], b_ref[...])


def ln_residual_t(x, rt, g, b, tm):
    T, D = x.shape
    return pl.pallas_call(
        _ln_res_kernel, grid=(T // tm,),
        in_specs=[pl.BlockSpec((tm, D), lambda i: (i, 0)),
                  pl.BlockSpec((D, tm), lambda i: (0, i)),
                  pl.BlockSpec((1, D), lambda i: (0, 0)),
                  pl.BlockSpec((1, D), lambda i: (0, 0))],
        out_specs=pl.BlockSpec((tm, D), lambda i: (i, 0)),
        out_shape=jax.ShapeDtypeStruct((T, D), f32),
        compiler_params=_cparams(1, 48), name="ln_res")(x, rt, g.reshape(1, D), b.reshape(1, D))


def _mm_kernel(x_ref, w_ref, o_ref, xb_ref):
    @pl.when(pl.program_id(1) == 0)
    def _():
        xb_ref[...] = x_ref[...].astype(bf16)

    o_ref[...] = _dot(xb_ref[...], w_ref[...]).astype(o_ref.dtype)


def matmul(x, w, tm, tn, name, layer=None):
    T, K = x.shape
    N = w.shape[-1]
    if layer is None:
        w_spec = pl.BlockSpec((K, tn), lambda i, j: (0, j))
    else:
        w_spec = pl.BlockSpec((None, K, tn), lambda i, j: (layer, 0, j))
    return pl.pallas_call(
        _mm_kernel, grid=(T // tm, N // tn),
        in_specs=[pl.BlockSpec((tm, K), lambda i, j: (i, 0)), w_spec],
        out_specs=pl.BlockSpec((tm, tn), lambda i, j: (i, j)),
        out_shape=jax.ShapeDtypeStruct((T, N), f32),
        scratch_shapes=[pltpu.VMEM((tm, K), bf16)],
        compiler_params=_cparams(2, 52), name=name)(x, w)


def _gate_kernel(t_ref, fb_ref, o_ref):
    x = t_ref[...] + fb_ref[...]
    o_ref[...] = jnp.minimum(x, 0.0) - jnp.log1p(jnp.exp(-jnp.abs(x)))


def forget_gate(h, fb_row, tm):
    T = h.shape[0]
    return pl.pallas_call(
        _gate_kernel, grid=(T // tm,),
        in_specs=[pl.BlockSpec((tm, LANES), lambda i: (i, TAIL0 // LANES)),
                  pl.BlockSpec((1, LANES), lambda i: (0, 0))],
        out_specs=pl.BlockSpec((tm, LANES), lambda i: (i, 0)),
        out_shape=jax.ShapeDtypeStruct((T, LANES), f32),
        compiler_params=_cparams(1), name="forget_gate")(h, fb_row)


def _cumsum_kernel(x_ref, o_ref, carry_ref):
    @pl.when(pl.program_id(1) == 0)
    def _():
        carry_ref[...] = jnp.zeros_like(carry_ref)

    bl = x_ref.shape[-1]
    r = lax.broadcasted_iota(jnp.int32, (bl, bl), 0)
    c = lax.broadcasted_iota(jnp.int32, (bl, bl), 1)
    upper = (r <= c).astype(f32)
    y = jnp.dot(x_ref[0], upper, precision=lax.Precision.HIGHEST,
                preferred_element_type=f32) + carry_ref[:, 0:1]
    o_ref[0] = y
    carry_ref[...] = jnp.broadcast_to(y[:, bl - 1:bl], carry_ref.shape)


def cumsum_lanes(x, bl):
    n, r, L = x.shape
    return pl.pallas_call(
        _cumsum_kernel, grid=(n, L // bl),
        in_specs=[pl.BlockSpec((1, r, bl), lambda s, j: (s, 0, j))],
        out_specs=pl.BlockSpec((1, r, bl), lambda s, j: (s, 0, j)),
        out_shape=jax.ShapeDtypeStruct((n, r, L), f32),
        scratch_shapes=[pltpu.VMEM((r, LANES), f32)],
        compiler_params=_cparams(2), name="cumsum")(x)


def _cumsum_rows_kernel(x_ref, o_ref, carry_ref):
    @pl.when(pl.program_id(1) == 0)
    def _():
        carry_ref[...] = jnp.zeros_like(carry_ref)

    bl = x_ref.shape[0]
    r = lax.broadcasted_iota(jnp.int32, (bl, bl), 0)
    c = lax.broadcasted_iota(jnp.int32, (bl, bl), 1)
    lower = (c <= r).astype(f32)
    y = jnp.dot(lower, x_ref[...], precision=lax.Precision.HIGHEST,
                preferred_element_type=f32) + carry_ref[0:1, :]
    o_ref[...] = y
    carry_ref[...] = jnp.broadcast_to(y[bl - 1:bl, :], carry_ref.shape)


def cumsum_rows(x, nseq, L, bl):
    nb = L // bl
    return pl.pallas_call(
        _cumsum_rows_kernel, grid=(nseq, nb),
        in_specs=[pl.BlockSpec((bl, LANES), lambda s, j: (s * nb + j, 0))],
        out_specs=pl.BlockSpec((bl, LANES), lambda s, j: (s * nb + j, 0)),
        out_shape=jax.ShapeDtypeStruct((nseq * L, LANES), f32),
        scratch_shapes=[pltpu.VMEM((SUBLANES, LANES), f32)],
        compiler_params=_cparams(2), name="cumsum_rows")(x)


_BIAS_TERMS = 3


def _split_heads(q, lane):
    zero = jnp.zeros_like(q)
    return jnp.where(lane, q, zero), jnp.where(lane, zero, q)


def _pack_queries(q, lane_i):
    qs = q * (DH_A ** -0.5)
    q0 = jnp.where(lane_i < HALF, qs, jnp.where(lane_i < HALF + _BIAS_TERMS, 1.0, 0.0))
    q1 = jnp.where(lane_i >= HALF, qs, jnp.where(lane_i < _BIAS_TERMS, 1.0, 0.0))
    return q0.astype(bf16), q1.astype(bf16)


def _pack_keys(k, c0, c1, lane_i):
    def one(own, c, base):
        out = jnp.where(own, k, 0.0)
        rest = -c
        for t in range(_BIAS_TERMS):
            term = rest.astype(bf16).astype(f32)
            out = jnp.where(lane_i == base + t, term, out)
            rest = rest - term
        return out.astype(bf16)
    return one(lane_i < HALF, c0, HALF), one(lane_i >= HALF, c1, 0)


def _pack_values(v, lane):
    return jnp.where(lane, v, 1.0).astype(bf16), jnp.where(lane, 1.0, v).astype(bf16)


def _attn_step(qs, ks, vs, carry, mask):
    out = []
    for q, k, v, (m, acc) in zip(qs, ks, vs, carry):
        s = _dot_nt(q, k)
        if mask is not None:
            s = jnp.where(mask, s, NEG_INF)
        mn = jnp.maximum(m, jnp.max(s, -1, keepdims=True))
        p = jnp.exp(s - mn).astype(bf16)
        out.append((mn, acc * jnp.exp(m - mn) + _dot(p, v)))
    return tuple(out)


def _attn_init(tq):
    return tuple((jnp.full((tq, 1), NEG_INF, f32), jnp.zeros((tq, LANES), f32)) for _ in range(2))


def _attn_finish(carry, lane):
    (_, acc0), (_, acc1) = carry
    o0 = acc0 / pltpu.roll(acc0, HALF, 1)
    o1 = acc1 / pltpu.roll(acc1, HALF, 1)
    return jnp.where(lane, o0, o1)


def _fox_prompt_kernel(q_ref, k_ref, v_ref, c_ref, o_ref, k0_ref, k1_ref, v0_ref, v1_ref, *, tq, tk):
    i = pl.program_id(2)
    per = tq // tk
    lane_k = lax.broadcasted_iota(jnp.int32, (tk, LANES), 1)
    lane_q = lax.broadcasted_iota(jnp.int32, (tq, LANES), 1)

    @pl.when(i == 0)
    def _():
        def pack(j, _):
            rows = pl.ds(pl.multiple_of(j * tk, tk), tk)
            c = c_ref[0, 0, rows, :]
            k0, k1 = _pack_keys(k_ref[rows, :], c[:, 0:1], c[:, 1:2], lane_k)
            v0, v1 = _pack_values(v_ref[rows, :], lane_k < HALF)
            k0_ref[rows, :] = k0
            k1_ref[rows, :] = k1
            v0_ref[rows, :] = v0
            v1_ref[rows, :] = v1
            return 0

        lax.fori_loop(0, k_ref.shape[0] // tk, pack, 0)

    qs = _pack_queries(q_ref[...], lane_q)

    def step(j, carry, mask):
        rows = pl.ds(pl.multiple_of(j * tk, tk), tk)
        return _attn_step(qs, (k0_ref[rows, :], k1_ref[rows, :]), (v0_ref[rows, :], v1_ref[rows, :]), carry, mask)

    group = 2 if per % 2 == 0 else 1

    def body(g, carry):
        for u in range(group):
            carry = step(g * group + u, carry, None)
        return carry

    carry = lax.fori_loop(0, i * (per // group), body, _attn_init(tq))
    r = lax.broadcasted_iota(jnp.int32, (tq, tk), 0)
    c = lax.broadcasted_iota(jnp.int32, (tq, tk), 1)
    for d in range(per):
        carry = step(i * per + d, carry, c + d * tk <= r)
    o_ref[...] = _attn_finish(carry, lane_q < HALF).astype(o_ref.dtype)


def fox_prompt(h, cp, B, S, tq, tk):
    nq = S // tq
    qb, kb, vb = FQ0 // LANES, FK0 // LANES, FV0 // LANES
    return pl.pallas_call(
        functools.partial(_fox_prompt_kernel, tq=tq, tk=tk), grid=(B, H_A // 2, nq),
        in_specs=[pl.BlockSpec((tq, LANES), lambda b, p, i: (b * nq + i, qb + p)),
                  pl.BlockSpec((S, LANES), lambda b, p, i: (b, kb + p)),
                  pl.BlockSpec((S, LANES), lambda b, p, i: (b, vb + p)),
                  pl.BlockSpec((1, 1, S, 2), lambda b, p, i: (b, p, 0, 0))],
        out_specs=pl.BlockSpec((tq, LANES), lambda b, p, i: (b * nq + i, p)),
        out_shape=jax.ShapeDtypeStruct((B * S, W_A), bf16),
        scratch_shapes=[pltpu.VMEM((S, LANES), bf16) for _ in range(4)],
        compiler_params=_cparams(3, 52), name="fox_prompt")(h, h, h, cp)


def _fox_sample_kernel(q_ref, kn_ref, vn_ref, kc_ref, vc_ref, c_ref, o_ref, m_ref, l_ref, acc_ref,
                       *, Ls, P, tk):
    j = pl.program_id(1)

    @pl.when(j == 0)
    def _():
        m_ref[...] = jnp.full(m_ref.shape, NEG_INF, f32)
        l_ref[...] = jnp.zeros_like(l_ref)
        acc_ref[...] = jnp.zeros_like(acc_ref)

    q = q_ref[...] * (DH_A ** -0.5)

    def head_step(hd, k, v, ck, mask, seq_on_lanes):
        cols = slice(hd * DH_A, (hd + 1) * DH_A)
        qh, kb, vb = q[:, cols].astype(bf16), k.astype(bf16), v.astype(bf16)
        s = (_dot(qh, kb) if seq_on_lanes else _dot_nt(qh, kb)) - ck
        if mask is not None:
            s = jnp.where(mask, s, NEG_INF)
        m = m_ref[hd]
        mn = jnp.maximum(m, jnp.max(s, -1, keepdims=True))
        a = jnp.exp(m - mn)
        p = jnp.exp(s - mn)
        pb = p.astype(bf16)
        l_ref[hd] = a * l_ref[hd] + jnp.sum(p, -1, keepdims=True)
        acc_ref[hd] = a * acc_ref[hd] + (_dot_nt(pb, vb) if seq_on_lanes else _dot(pb, vb))
        m_ref[hd] = mn

    off = pl.multiple_of(j * tk, tk)
    for hd in range(H_A):
        head_step(hd, kc_ref[hd], vc_ref[hd], c_ref[0, hd:hd + 1, pl.ds(off, tk)], None, True)

    @pl.when(j == pl.num_programs(1) - 1)
    def _():
        r = lax.broadcasted_iota(jnp.int32, (Ls, Ls), 0)
        c = lax.broadcasted_iota(jnp.int32, (Ls, Ls), 1)
        outs = []
        for hd in range(H_A):
            cols = slice(hd * DH_A, (hd + 1) * DH_A)
            head_step(hd, kn_ref[:, cols], vn_ref[:, cols], c_ref[0, hd:hd + 1, P:P + Ls], c <= r, False)
            outs.append(acc_ref[hd] / l_ref[hd])
        o_ref[...] = jnp.concatenate(outs, axis=1).astype(o_ref.dtype)


def fox_sample(h, cache_k, cache_v, layer, cT, row0, Bs, Ls, P):
    tk = _pick(P, (1024, 512, 256, 128))
    blk = lambda col0: pl.BlockSpec((Ls, W_A), lambda b, j: (row0 + b, col0 // W_A))
    cache = lambda: pl.BlockSpec((None, None, H_A, DH_A, tk), lambda b, j: (layer, b, 0, 0, j))
    return pl.pallas_call(
        functools.partial(_fox_sample_kernel, Ls=Ls, P=P, tk=tk), grid=(Bs, P // tk),
        in_specs=[blk(FQ0), blk(FK0), blk(FV0), cache(), cache(),
                  pl.BlockSpec((1, H_A, cT.shape[-1]), lambda b, j: (b, 0, 0))],
        out_specs=pl.BlockSpec((Ls, W_A), lambda b, j: (b, 0)),
        out_shape=jax.ShapeDtypeStruct((Bs * Ls, W_A), bf16),
        scratch_shapes=[pltpu.VMEM((H_A, Ls, 1), f32), pltpu.VMEM((H_A, Ls, 1), f32),
                        pltpu.VMEM((H_A, Ls, DH_A), f32)],
        compiler_params=_cparams(2, 48), name="fox_sample")(h, h, h, cache_k, cache_v, cT)


def _kv_out_kernel(*refs):
    x_ref, o_ref = refs[0], refs[-1]
    o_ref[...] = x_ref[...].T.reshape(2, DH_A, x_ref.shape[0])


def kv_sequence_minor(h, col0, layer, B, S, ts, into=None):
    nq = S // ts
    in_specs = [pl.BlockSpec((ts, LANES), lambda b, p, i: (b * nq + i, col0 // LANES + p))]
    args, aliases = [h], {}
    if into is not None:
        in_specs.append(pl.BlockSpec(memory_space=pl.ANY))
        args.append(into)
        aliases = {1: 0}
    return pl.pallas_call(
        _kv_out_kernel, grid=(B, H_A // 2, nq), in_specs=in_specs,
        out_specs=pl.BlockSpec((None, None, 2, DH_A, ts), lambda b, p, i: (layer, b, p, 0, i)),
        out_shape=jax.ShapeDtypeStruct((DEPTH, B, H_A, DH_A, S), f32),
        input_output_aliases=aliases,
        compiler_params=_cparams(3), name="kv_out")(*args)


def _pair_mean(x, lane):
    s0 = jnp.sum(jnp.where(lane, x, 0.0), -1, keepdims=True)
    s1 = jnp.sum(jnp.where(lane, 0.0, x), -1, keepdims=True)
    return jnp.where(lane, s0, s1) * (1.0 / HALF)


def _rotary(x, cos, sin_signed, first_half):
    xr = jnp.where(first_half, pltpu.roll(x, LANES - HALF // 2, 1), pltpu.roll(x, HALF // 2, 1))
    return x * cos + xr * sin_signed


def _retention_kernel(q_ref, k_ref, v_ref, g_ref, cos_ref, sin_ref, dec_ref, gq_ref, gk_ref, gl_ref,
                      s0_ref, gn_ref, o_ref, s1_ref, st_ref, *, Lc):
    c = pl.program_id(1)

    @pl.when(c == 0)
    def _():
        st_ref[...] = s0_ref[0]

    lane_i = lax.broadcasted_iota(jnp.int32, (Lc, LANES), 1)
    lane = lane_i < HALF
    first_half = (lane_i % HALF) < (HALF // 2)
    cos, sin = cos_ref[...], sin_ref[...]
    sr = lax.broadcasted_iota(jnp.int32, (LANES, LANES), 0) < HALF
    sc = lax.broadcasted_iota(jnp.int32, (LANES, LANES), 1) < HALF
    for p in range(H_B // 2):
        cols = slice(p * LANES, (p + 1) * LANES)
        q = _rotary(q_ref[:, cols], cos, sin, first_half)
        k = _rotary(k_ref[:, cols], cos, sin, first_half) * (DK_B ** -0.5)
        qb, kb, vb = q.astype(bf16), k.astype(bf16), v_ref[:, cols].astype(bf16)
        q0, q1 = _split_heads(qb, lane)
        a0 = (_dot_nt(q0, kb) * dec_ref[2 * p]).astype(bf16)
        a1 = (_dot_nt(q1, kb) * dec_ref[2 * p + 1]).astype(bf16)
        intra = jnp.where(lane, _dot(a0, vb), _dot(a1, vb))
        st = st_ref[p]
        o = intra + _dot(qb, st.astype(bf16)) * gq_ref[p]
        kd = (k * gk_ref[p]).astype(bf16)
        st_ref[p] = gl_ref[p, 0:1, :] * st + jnp.where(sr == sc, _dot_tn(kd, vb), 0.0)
        mu = _pair_mean(o, lane)
        d = o - mu
        var = _pair_mean(d * d, lane)
        on = d * lax.rsqrt(var + LN_EPS) * gn_ref[:, cols]
        o_ref[:, cols] = (_silu(g_ref[:, cols]) * on).astype(o_ref.dtype)

    @pl.when(c == pl.num_programs(1) - 1)
    def _():
        s1_ref[0] = st_ref[...]


def _retention_tables(pos, Lc):
    half = DK_B // 2
    inv = ROPE_BASE ** (-jnp.arange(half, dtype=f32) / half)
    ang = pos.astype(f32)[:, None] * inv[None, :]
    cos, sin = jnp.cos(ang), jnp.sin(ang)
    cos_t = jnp.tile(cos, (1, 4))
    sin_t = jnp.tile(jnp.concatenate([-sin, sin], -1), (1, 2))
    lg = jnp.log1p(-jnp.exp2(-5.0 - jnp.arange(H_B, dtype=f32)))
    i = jnp.arange(Lc, dtype=f32)
    diff = i[:, None] - i[None, :]
    dec = jnp.exp(jnp.where((diff >= 0)[None], diff[None] * lg[:, None, None], NEG_INF))
    pair = lambda t: jnp.repeat(t.reshape(t.shape[0], H_B // 2, 2), HALF, axis=-1)
    gq = pair(jnp.exp((i[:, None] + 1.0) * lg[None, :])).transpose(1, 0, 2)
    gk = pair(jnp.exp((Lc - 1.0 - i)[:, None] * lg[None, :])).transpose(1, 0, 2)
    gl = jnp.broadcast_to(pair(jnp.exp(Lc * lg)[None, :]).transpose(1, 0, 2), (H_B // 2, SUBLANES, LANES))
    return cos_t, sin_t, dec, gq, gk, gl


def retention(h, gn_w, s0, pos, row0, nseq, L, Lc):
    nch = L // Lc
    npair = H_B // 2
    cos_t, sin_t, dec, gq, gk, gl = _retention_tables(pos, Lc)
    blk = lambda col0: pl.BlockSpec((Lc, W_B), lambda s, c: (row0 + s * nch + c, col0 // W_B))
    full = lambda shape: pl.BlockSpec(shape, lambda s, c: (0,) * len(shape))
    return pl.pallas_call(
        functools.partial(_retention_kernel, Lc=Lc), grid=(nseq, nch),
        in_specs=[blk(RQ0), blk(RK0), blk(RV0), blk(RG0),
                  pl.BlockSpec((Lc, LANES), lambda s, c: (c, 0)),
                  pl.BlockSpec((Lc, LANES), lambda s, c: (c, 0)),
                  full((H_B, Lc, Lc)), full((npair, Lc, LANES)), full((npair, Lc, LANES)),
                  full((npair, SUBLANES, LANES)),
                  pl.BlockSpec((1, npair, LANES, LANES), lambda s, c: (s, 0, 0, 0)),
                  full((1, W_B))],
        out_specs=[pl.BlockSpec((Lc, W_B), lambda s, c: (s * nch + c, 0)),
                   pl.BlockSpec((1, npair, LANES, LANES), lambda s, c: (s, 0, 0, 0))],
        out_shape=[jax.ShapeDtypeStruct((nseq * L, W_B), bf16),
                   jax.ShapeDtypeStruct((nseq, npair, LANES, LANES), f32)],
        scratch_shapes=[pltpu.VMEM((npair, LANES, LANES), f32)],
        compiler_params=_cparams(2, 32), name="retention")(
            h, h, h, h, cos_t, sin_t, dec, gq, gk, gl, s0, gn_w.reshape(1, W_B))


def _pack_ret_state(s):
    n = s.shape[0]
    s = s.reshape(n, H_B // 2, 2, DK_B, DK_B)
    z = jnp.zeros_like(s[:, :, 0])
    top = jnp.concatenate([s[:, :, 0], z], -1)
    bot = jnp.concatenate([z, s[:, :, 1]], -1)
    return jnp.concatenate([top, bot], -2)


def _unpack_ret_state(s):
    n = s.shape[0]
    return jnp.stack([s[:, :, :HALF, :HALF], s[:, :, HALF:, HALF:]], 2).reshape(n, H_B, DK_B, DK_B)


def _ssd_kernel(xbc_ref, z_ref, t_ref, cw_ref, cb_ref, dtb_ref, alog_ref, dsk_ref, nw_ref, hist_ref, h0_ref,
                o_ref, h1_ref, xpad_ref, hs_ref, *, Lc):
    c = pl.program_id(1)

    @pl.when(c == 0)
    def _():
        xpad_ref[0:SUBLANES, :] = hist_ref[0]
        hs_ref[...] = h0_ref[0]

    xpad_ref[SUBLANES:SUBLANES + Lc, :] = xbc_ref[...]
    conv = cb_ref[...]
    for t in range(CONV_W):
        r0 = SUBLANES - (CONV_W - 1) + t
        conv = conv + xpad_ref[r0:r0 + Lc, :] * cw_ref[t:t + 1, :]
    xpad_ref[0:SUBLANES, :] = xpad_ref[Lc:Lc + SUBLANES, :]
    xc = _silu(conv)
    xs = xc[:, :W_C]
    bm = [xc[:, W_C + g * N_C:W_C + (g + 1) * N_C].astype(bf16) for g in range(G_C)]
    cm = [xc[:, W_C + (G_C + g) * N_C:W_C + (G_C + g + 1) * N_C].astype(bf16) for g in range(G_C)]

    lane_i = lax.broadcasted_iota(jnp.int32, (Lc, LANES), 1)
    lane = lane_i < HALF
    dt_valid = (lane_i >= DT_LANE0) & (lane_i < DT_LANE0 + H_C)
    dt = _softplus(t_ref[...] + dtb_ref[...])
    dta = jnp.where(dt_valid, dt * (-jnp.exp(alog_ref[...])), 0.0)
    r = lax.broadcasted_iota(jnp.int32, (Lc, Lc), 0)
    cidx = lax.broadcasted_iota(jnp.int32, (Lc, Lc), 1)
    tri = cidx <= r
    a_cs = jnp.dot(tri.astype(f32), dta, precision=lax.Precision.HIGHEST, preferred_element_type=f32)
    a_cs_t = a_cs.T
    dt_t = dt.T
    row_first = lax.broadcasted_iota(jnp.int32, (LANES, LANES), 0) < HALF

    ys = []
    for p in range(H_C // 2):
        g = (2 * p) // (H_C // G_C)
        xpair = xs[:, p * LANES:(p + 1) * LANES]
        xpair_b = xpair.astype(bf16)
        if p % (H_C // G_C // 2) == 0:
            cb = _dot_nt(cm[g], bm[g])
        yh, acol, dcol = [], [], []
        for hd in (2 * p, 2 * p + 1):
            li = DT_LANE0 + hd
            ac = a_cs[:, li:li + 1]
            seg = ac - a_cs_t[li:li + 1, :]
            w = cb * jnp.exp(jnp.where(tri, seg, NEG_INF)) * dt_t[li:li + 1, :]
            yh.append(_dot(w.astype(bf16), xpair_b))
            acol.append(ac)
            dcol.append(dt[:, li:li + 1])
        acs_pair = jnp.where(lane, acol[0], acol[1])
        dt_pair = jnp.where(lane, dcol[0], dcol[1])
        hs = hs_ref[p]
        y = jnp.where(lane, yh[0], yh[1]) + _dot_nt(cm[g], hs.astype(bf16)) * jnp.exp(acs_pair)
        ys.append(y)
        a_last = acs_pair[Lc - 1:Lc, :]
        to_end = jnp.exp(a_last - acs_pair) * dt_pair
        upd = _dot_tn((xpair * to_end).astype(bf16), bm[g])
        sdec = jnp.exp(jnp.where(row_first, acol[0][Lc - 1:Lc, :], acol[1][Lc - 1:Lc, :]))
        hs_ref[p] = sdec * hs + upd

    y = jnp.concatenate(ys, axis=1) + dsk_ref[...] * xs
    y = y * _silu(z_ref[...])
    y = y * lax.rsqrt(jnp.mean(y * y, -1, keepdims=True) + LN_EPS) * nw_ref[...]
    o_ref[...] = y.astype(o_ref.dtype)

    @pl.when(c == pl.num_programs(1) - 1)
    def _():
        h1_ref[0] = hs_ref[...]


def ssd(h, conv_w, conv_b, dt_bias, a_log, d_skip, norm_w, hist, h0, row0, nseq, L, Lc):
    nch = L // Lc
    lane_row = lambda v: jnp.zeros((1, LANES), f32).at[0, DT_LANE0:DT_LANE0 + H_C].set(v)
    cw = jnp.zeros((SUBLANES, CONV_DIM), f32).at[:CONV_W].set(conv_w)
    row = lambda w: pl.BlockSpec((1, w), lambda s, c: (0, 0))
    return pl.pallas_call(
        functools.partial(_ssd_kernel, Lc=Lc), grid=(nseq, nch),
        in_specs=[pl.BlockSpec((Lc, CONV_DIM), lambda s, c: (row0 + s * nch + c, XBC0 // CONV_DIM)),
                  pl.BlockSpec((Lc, W_C), lambda s, c: (row0 + s * nch + c, Z0 // W_C)),
                  pl.BlockSpec((Lc, LANES), lambda s, c: (row0 + s * nch + c, TAIL0 // LANES)),
                  pl.BlockSpec((SUBLANES, CONV_DIM), lambda s, c: (0, 0)),
                  row(CONV_DIM), row(LANES), row(LANES), row(W_C), row(W_C),
                  pl.BlockSpec((1, SUBLANES, CONV_DIM), lambda s, c: (s, 0, 0)),
                  pl.BlockSpec((1, H_C // 2, LANES, LANES), lambda s, c: (s, 0, 0, 0))],
        out_specs=[pl.BlockSpec((Lc, W_C), lambda s, c: (s * nch + c, 0)),
                   pl.BlockSpec((1, H_C // 2, LANES, LANES), lambda s, c: (s, 0, 0, 0))],
        out_shape=[jax.ShapeDtypeStruct((nseq * L, W_C), bf16),
                   jax.ShapeDtypeStruct((nseq, H_C // 2, LANES, LANES), f32)],
        scratch_shapes=[pltpu.VMEM((Lc + SUBLANES, CONV_DIM), f32),
                        pltpu.VMEM((H_C // 2, LANES, LANES), f32)],
        compiler_params=_cparams(2, 48), name="ssd")(
            h, h, h, cw, conv_b.reshape(1, CONV_DIM), lane_row(dt_bias), lane_row(a_log),
            jnp.repeat(d_skip, P_C).reshape(1, W_C), norm_w.reshape(1, W_C), hist, h0)


def _outproj_kernel(x_ref, a_ref, b_ref, c_ref, w_ref, g_ref, beta_ref, o_ref):
    mix = (_dot(a_ref[...], w_ref[0:W_A, :]) + _dot(b_ref[...], w_ref[W_A:W_A + W_B, :])
           + _dot(c_ref[...], w_ref[W_A + W_B:, :]))
    o_ref[...] = _ln(ALPHA * x_ref[...] + mix, g_ref[...], beta_ref[...])


def outproj_ln(x, ao, bo, co, w, g, b, tm):
    T, D = x.shape
    rows = lambda wd: pl.BlockSpec((tm, wd), lambda i: (i, 0))
    const = lambda s: pl.BlockSpec(s, lambda i: (0, 0))
    return pl.pallas_call(
        _outproj_kernel, grid=(T // tm,),
        in_specs=[rows(D), rows(W_A), rows(W_B), rows(W_C), const(w.shape), const((1, D)), const((1, D))],
        out_specs=rows(D),
        out_shape=jax.ShapeDtypeStruct((T, D), f32),
        compiler_params=_cparams(1, 52), name="outproj_ln")(x, ao, bo, co, w, g.reshape(1, D), b.reshape(1, D))


def _mem_body(x_ref, wq_ref, mk_ref, mv_ref, wo_ref, g_ref, b_ref, o_ref, ob_ref):
    x = x_ref[...]
    q = _dot(x.astype(bf16), wq_ref[...])
    outs = []
    for hd in range(H_M):
        sl = slice(hd * DH_M, (hd + 1) * DH_M)
        s = _dot_nt(q[:, sl].astype(bf16), mk_ref[0, :, sl].astype(bf16)) * (DH_M ** -0.5)
        p = jnp.exp(s - jnp.max(s, -1, keepdims=True))
        p = p / jnp.sum(p, -1, keepdims=True)
        outs.append(_dot(p.astype(bf16), mv_ref[0, :, sl].astype(bf16)))
    o = jnp.concatenate(outs, axis=1).astype(bf16)
    y = _ln(ALPHA * x + _dot(o, wo_ref[...]), g_ref[...], b_ref[...])
    o_ref[...] = y
    ob_ref[...] = y.astype(bf16)


def _mem_kernel_first(x_ref, wq_ref, mk_ref, mv_ref, wo_ref, g_ref, b_ref, o_ref, ob_ref):
    _mem_body(x_ref, wq_ref, mk_ref, mv_ref, wo_ref, g_ref, b_ref, o_ref, ob_ref)


def _mem_kernel_into(x_ref, wq_ref, mk_ref, mv_ref, wo_ref, g_ref, b_ref, prev_ref, prevb_ref, o_ref, ob_ref):
    del prev_ref, prevb_ref
    _mem_body(x_ref, wq_ref, mk_ref, mv_ref, wo_ref, g_ref, b_ref, o_ref, ob_ref)


def mem_attn_ln(x, wq, mk, mv, wo, g, b, row0, nseq, L, tr, into=None):
    T, D = x.shape
    per = L // tr
    nm = mk.shape[1]
    const = lambda s: pl.BlockSpec(s, lambda i: (0,) * len(s))
    in_specs = [pl.BlockSpec((tr, D), lambda i: (row0 + i, 0)), const(wq.shape),
                pl.BlockSpec((1, nm, W_M), lambda i: (i // per, 0, 0)),
                pl.BlockSpec((1, nm, W_M), lambda i: (i // per, 0, 0)),
                const(wo.shape), const((1, D)), const((1, D))]
    args = [x, wq, mk, mv, wo, g.reshape(1, D), b.reshape(1, D)]
    aliases = {}
    if into is not None:
        in_specs += [pl.BlockSpec(memory_space=pl.ANY), pl.BlockSpec(memory_space=pl.ANY)]
        aliases = {len(args): 0, len(args) + 1: 1}
        args += list(into)
    return pl.pallas_call(
        _mem_kernel_first if into is None else _mem_kernel_into, grid=(nseq * per,),
        in_specs=in_specs,
        out_specs=[pl.BlockSpec((tr, D), lambda i: (row0 + i, 0)), pl.BlockSpec((tr, D), lambda i: (row0 + i, 0))],
        out_shape=[jax.ShapeDtypeStruct((T, D), f32), jax.ShapeDtypeStruct((T, D), bf16)],
        input_output_aliases=aliases,
        compiler_params=_cparams(1, 48), name="mem_attn_ln")(*args)


def _peer_score_kernel(x_ref, wq_ref, k1_ref, k2_ref, o_ref):
    q = _dot(x_ref[...].astype(bf16), wq_ref[...])
    k1 = k1_ref[...].astype(bf16)
    k2 = k2_ref[...].astype(bf16)
    hk = PEER_DK // 2
    for hd in range(PEER_HEADS):
        q1 = q[:, hd * PEER_DK:hd * PEER_DK + hk].astype(bf16)
        q2 = q[:, hd * PEER_DK + hk:(hd + 1) * PEER_DK].astype(bf16)
        o_ref[hd, 0:PEER_NK, :] = _dot_nt(k1, q1)
        o_ref[hd, PEER_NK:2 * PEER_NK, :] = _dot_nt(k2, q2)


def peer_scores(x, wq, k1, k2, tm):
    T, D = x.shape
    const = lambda s: pl.BlockSpec(s, lambda i: (0, 0))
    return pl.pallas_call(
        _peer_score_kernel, grid=(T // tm,),
        in_specs=[pl.BlockSpec((tm, D), lambda i: (i, 0)), const(wq.shape), const(k1.shape), const(k2.shape)],
        out_specs=pl.BlockSpec((PEER_HEADS, 2 * PEER_NK, tm), lambda i: (0, 0, i)),
        out_shape=jax.ShapeDtypeStruct((PEER_HEADS, 2 * PEER_NK, T), f32),
        compiler_params=_cparams(1, 52), name="peer_scores")(x, wq, k1, k2)


_CAND_ROWS = PEER_TOPK + 7 * SUBLANES + SUBLANES


def _cand_flat_index(cw):
    idx = [0 * PEER_TOPK + b for b in range(PEER_TOPK)]
    for a in range(1, 8):
        idx += [a * PEER_TOPK + b for b in range(SUBLANES)]
    idx += [a * PEER_TOPK for a in range(8, PEER_TOPK)]
    return np.broadcast_to(np.asarray(idx, np.float32)[:, None], (_CAND_ROWS, cw)).copy()


def _extract_top(s, key_idx, n):
    rank = jnp.full(s.shape, float(n), f32)
    vals = []
    for a in range(n):
        mx = jnp.max(s, axis=0, keepdims=True)
        first = jnp.min(jnp.where(s == mx, key_idx, float(1 << 20)), axis=0, keepdims=True)
        sel = key_idx == first
        rank = jnp.where(sel, float(a), rank)
        s = jnp.where(sel, NEG_INF, s)
        vals.append(mx)
    return vals, rank


def _top_values(s, n, with_rank):
    rank = jnp.full(s.shape, float(n), f32) if with_rank else None
    vals = []
    for a in range(n):
        mx = jnp.max(s, axis=0, keepdims=True)
        hit = s == mx
        if with_rank:
            rank = jnp.where(hit, float(a), rank)
        s = jnp.where(hit, NEG_INF, s)
        vals.append(mx)
    return vals, rank


def _candidates(v1, v2):
    v1a = jnp.concatenate(v1, axis=0)
    v2a = jnp.concatenate(v2, axis=0)
    return jnp.concatenate([v1[0] + v2a] + [v1[a] + v2a[0:SUBLANES] for a in range(1, 8)]
                           + [v1a[SUBLANES:] + v2[0]], axis=0)


def _counts_per_rank(picked):
    cnt_a = [jnp.sum(picked[0:PEER_TOPK], axis=0, keepdims=True)]
    for a in range(1, 8):
        lo = PEER_TOPK + (a - 1) * SUBLANES
        cnt_a.append(jnp.sum(picked[lo:lo + SUBLANES], axis=0, keepdims=True))
    tail = picked[PEER_TOPK + 7 * SUBLANES:]
    return cnt_a + [tail[a:a + 1] for a in range(SUBLANES)]


def _select_fast(s1, s2):
    v1, _ = _top_values(s1, PEER_TOPK, False)
    v2, rank2 = _top_values(s2, PEER_TOPK, True)
    cand = _candidates(v1, v2)
    vc, _ = _top_values(cand, PEER_TOPK, False)
    top = v1[0] + v2[0]
    zsum = jnp.exp(vc[0] - top)
    for k in range(1, PEER_TOPK):
        zsum = zsum + jnp.exp(vc[k] - top)
    picked = jnp.where(cand >= vc[PEER_TOPK - 1], 1.0, 0.0)
    cnt_a = _counts_per_rank(picked)
    cnt = jnp.zeros(s1.shape, f32)
    for a in range(PEER_TOPK):
        cnt = jnp.where(s1 == v1[a], cnt_a[a], cnt)
    n1 = jnp.sum(jnp.where(s1 >= v1[PEER_TOPK - 1], 1.0, 0.0), axis=0, keepdims=True)
    n2 = jnp.sum(jnp.where(rank2 < float(PEER_TOPK), 1.0, 0.0), axis=0, keepdims=True)
    nc = jnp.sum(picked, axis=0, keepdims=True)
    k = float(PEER_TOPK)
    distinct = (n1 == k) & (n2 == k) & (nc == k)
    tie = jnp.max(jnp.where(distinct, 0.0, 1.0)) > 0.0
    return rank2, cnt, zsum, v1[0], v2[0], tie


def _select_exact(s1, s2, key_idx, cidx):
    v1, rank1 = _extract_top(s1, key_idx, PEER_TOPK)
    v2, rank2 = _extract_top(s2, key_idx, PEER_TOPK)
    cand = _candidates(v1, v2)
    top = v1[0] + v2[0]
    picked = jnp.zeros(cand.shape, f32)
    zsum = jnp.zeros((1, s1.shape[1]), f32)
    for _k in range(PEER_TOPK):
        mx = jnp.max(cand, axis=0, keepdims=True)
        first = jnp.min(jnp.where(cand == mx, cidx, float(1 << 20)), axis=0, keepdims=True)
        sel = cidx == first
        picked = jnp.where(sel, 1.0, picked)
        cand = jnp.where(sel, NEG_INF, cand)
        zsum = zsum + jnp.exp(mx - top)
    cnt_a = _counts_per_rank(picked)
    cnt = jnp.zeros(s1.shape, f32)
    for a in range(PEER_TOPK):
        cnt = jnp.where(rank1 == float(a), cnt_a[a], cnt)
    return rank2, cnt, zsum


def _peer_topk_kernel(s_ref, cidx_ref, r2_ref, g2_ref, cnt_ref, g1_ref, *, tb, cw):
    nj = PEER_NK // SUBLANES

    def chunk(ci, _):
        off = pl.multiple_of(ci * cw, cw)
        s1 = s_ref[0, 0:PEER_NK, pl.ds(off, cw)]
        s2 = s_ref[0, PEER_NK:2 * PEER_NK, pl.ds(off, cw)]
        rank2, cnt, zsum, m1, m2, tie = _select_fast(s1, s2)
        e1 = jnp.exp(s1 - m1)
        r2_ref[0, :, pl.ds(off, cw)] = rank2.astype(bf16)
        g2_ref[0, :, pl.ds(off, cw)] = jnp.exp(s2 - m2).astype(bf16)
        cnt_ref[:, 0, :, pl.ds(off, cw)] = cnt.reshape(nj, SUBLANES, cw)
        g1_ref[:, 0, :, pl.ds(off, cw)] = (e1 / zsum).reshape(nj, SUBLANES, cw)

        @pl.when(tie)
        def _():
            key_idx = lax.broadcasted_iota(jnp.int32, (PEER_NK, cw), 0).astype(f32)
            rank2x, cntx, zsumx = _select_exact(s1, s2, key_idx, cidx_ref[...])
            r2_ref[0, :, pl.ds(off, cw)] = rank2x.astype(bf16)
            cnt_ref[:, 0, :, pl.ds(off, cw)] = cntx.reshape(nj, SUBLANES, cw)
            g1_ref[:, 0, :, pl.ds(off, cw)] = (e1 / zsumx).reshape(nj, SUBLANES, cw)

        return 0

    lax.fori_loop(0, tb // cw, chunk, 0)


def peer_topk(sT, tb):
    T = sT.shape[-1]
    nj = PEER_NK // SUBLANES
    cw = _pick(tb, (2 * LANES, LANES))
    per_key = lambda: pl.BlockSpec((1, PEER_NK, tb), lambda hd, i: (hd, 0, i))
    per_row = lambda: pl.BlockSpec((nj, 1, SUBLANES, tb), lambda hd, i: (0, hd, 0, i))
    r2, g2, cnt, g1 = pl.pallas_call(
        functools.partial(_peer_topk_kernel, tb=tb, cw=cw), grid=(PEER_HEADS, T // tb),
        in_specs=[pl.BlockSpec((1, 2 * PEER_NK, tb), lambda hd, i: (hd, 0, i)),
                  pl.BlockSpec((_CAND_ROWS, cw), lambda hd, i: (0, 0))],
        out_specs=[per_key(), per_key(), per_row(), per_row()],
        out_shape=[jax.ShapeDtypeStruct((PEER_HEADS, PEER_NK, T), bf16),
                   jax.ShapeDtypeStruct((PEER_HEADS, PEER_NK, T), bf16),
                   jax.ShapeDtypeStruct((nj, PEER_HEADS, SUBLANES, T), f32),
                   jax.ShapeDtypeStruct((nj, PEER_HEADS, SUBLANES, T), f32)],
        compiler_params=_cparams(2, 32), name="peer_topk")(sT, jnp.asarray(_cand_flat_index(cw)))
    packed = (PEER_HEADS, PEER_NK // BF16_ROWS, BF16_ROWS, T)
    return (r2.reshape(packed), g2.reshape(packed),
            cnt.reshape(nj, PEER_HEADS * SUBLANES, T), g1.reshape(nj, PEER_HEADS * SUBLANES, T))


def _peer_main_kernel(x_ref, u_ref, vt_ref, r2_ref, g2_ref, cnt_ref, g1_ref, o_ref):
    j = pl.program_id(1)
    tm = x_ref.shape[0]

    @pl.when(j == 0)
    def _():
        o_ref[...] = jnp.zeros_like(o_ref)

    a = _dot_nt(u_ref[...], x_ref[...])
    act = (0.5 * a * (1.0 + lax.erf(a * (2.0 ** -0.5)))).astype(bf16)
    zero = jnp.zeros((), bf16)
    parts = []
    for r in range(SUBLANES):
        w = None
        for hd in range(PEER_HEADS):
            row = hd * SUBLANES + r
            cnt = jnp.broadcast_to(cnt_ref[0, row:row + 1, :], (BF16_ROWS, tm)).astype(bf16)[None]
            g1 = jnp.broadcast_to(g1_ref[0, row:row + 1, :], (BF16_ROWS, tm)).astype(bf16)[None]
            sel = jnp.where(r2_ref[hd] < cnt, g2_ref[hd], zero) * g1
            w = sel if w is None else w + sel
        parts.append(w.reshape(PEER_NK, tm) * act[r * PEER_NK:(r + 1) * PEER_NK])
    o_ref[...] += _dot(vt_ref[...], jnp.concatenate(parts, axis=0))


def peer_main(xb, ub, vtb, layer, r2, g2, cnt, g1, tm):
    T, D = xb.shape
    te = SUBLANES * PEER_NK
    nj = PEER_NK // SUBLANES
    once = pl.Buffered(1)
    per_tok = lambda: pl.BlockSpec((PEER_HEADS, PEER_NK // BF16_ROWS, BF16_ROWS, tm),
                                   lambda i, j: (0, 0, 0, i), pipeline_mode=once)
    per_row = lambda: pl.BlockSpec((1, PEER_HEADS * SUBLANES, tm), lambda i, j: (j, 0, i))
    return pl.pallas_call(
        _peer_main_kernel, grid=(T // tm, nj),
        in_specs=[pl.BlockSpec((tm, D), lambda i, j: (i, 0), pipeline_mode=once),
                  pl.BlockSpec((None, te, D), lambda i, j: (layer, j, 0)),
                  pl.BlockSpec((None, D, te), lambda i, j: (layer, 0, j)),
                  per_tok(), per_tok(), per_row(), per_row()],
        out_specs=pl.BlockSpec((D, tm), lambda i, j: (0, i), pipeline_mode=once),
        out_shape=jax.ShapeDtypeStruct((D, T), f32),
        compiler_params=_cparams(2, 58), name="peer_main")(xb, ub, vtb, r2, g2, cnt, g1)


def _rearrange_w_in(w):
    o = np.cumsum([0, W_A, W_A, W_A, H_A, H_B * DK_B, H_B * DK_B, W_B, W_B, W_C, CONV_DIM, H_C])
    fq, fk, fv, ff, rq, rk, rv, rg, z, xbc, dtr = [slice(int(o[i]), int(o[i + 1])) for i in range(11)]
    parts = [w[:, s] for s in (xbc, fq, z, fk, fv, rq, rk, rv, rg, ff, dtr)]
    pad = NH - TAIL0 - H_A - H_C
    return jnp.concatenate(parts + [jnp.zeros((w.shape[0], pad), w.dtype)], axis=1).astype(bf16)


def kernel(x_prompt, x_sample, cache_fox_k, cache_fox_v, cache_fox_logf, state_ret, state_ssm, state_conv, cache_mem_k, cache_mem_v, mem_prompt, ln_in_g, ln_in_b, w_in, fox_fb, ret_gn_w, conv_w, conv_b, dt_bias, a_log, d_skip, ssm_norm_w, w_out, ln1_g, ln1_b, wq_mem, wkv_mem, wo_mem, ln2_g, ln2_b, peer_wq, peer_k1, peer_k2, peer_u, peer_v, ln3_g, ln3_b):
    B, S, D = x_prompt.shape
    Bs, Ls, _ = x_sample.shape
    P = cache_fox_k.shape[2]
    NM = mem_prompt.shape[1]
    Tp, Ts = B * S, Bs * Ls
    T = Tp + Ts
    assert Tp % Ls == 0 and T % LANES == 0 and Ls >= CONV_W - 1

    tm = _pick(T, (640, 768, 384, 256, 128))
    tm_peer = _pick(T, (1280, 768, 256, 128))
    tq = _pick(S, (1024, 512, 256, 128))
    tk = min(tq, 512)
    lc = _pick(S, (256, 128, 64))
    tr_mem = _pick(S, (512, 256, 128))
    cs_bl = 512
    tn_in = _pick(NH, (1280, 640))

    x = jnp.concatenate([x_prompt.reshape(Tp, D), x_sample.reshape(Ts, D)], axis=0)
    x = layer_norm_rows(x, ln_in_g, ln_in_b, tm)

    pos_p = jnp.arange(S)
    pos_s = P + jnp.arange(Ls)
    zeros_ret = jnp.zeros((B, H_B // 2, LANES, LANES), f32)
    zeros_ssm = jnp.zeros((B, H_C // 2, LANES, LANES), f32)
    zeros_hist = jnp.zeros((B, SUBLANES, CONV_DIM), f32)
    lp_s = -(-(P + Ls) // cs_bl) * cs_bl

    w_in_b = jax.vmap(_rearrange_w_in)(w_in)
    peer_u_b = peer_u.astype(bf16)
    cache_k_rows = cache_fox_k.transpose(0, 1, 3, 4, 2)
    cache_v_rows = cache_fox_v.transpose(0, 1, 3, 4, 2)
    peer_vt_b = peer_v.transpose(0, 2, 1).astype(bf16)

    k_t = v_t = None
    st_p = [[] for _ in range(8)]
    st_s = [[] for _ in range(6)]
    for l in range(DEPTH):
        h = matmul(x, w_in_b, tm, tn_in, "in_proj", layer=l)
        fb_row = jnp.zeros((1, LANES), f32).at[0, FF_LANE0:FF_LANE0 + H_A].set(fox_fb[l])
        logf_pad = forget_gate(h, fb_row, tm)
        logf = logf_pad[:, FF_LANE0:FF_LANE0 + H_A]
        logf_p = logf[:Tp].reshape(B, S, H_A)
        logf_s = logf[Tp:].reshape(Bs, Ls, H_A)

        c_p = cumsum_rows(logf_pad, B, S, min(cs_bl, S))[:, FF_LANE0:FF_LANE0 + H_A]
        c_p = c_p.reshape(B, S, H_A // 2, 2).transpose(0, 2, 1, 3)
        ao_p = fox_prompt(h, c_p, B, S, tq, tk)
        lf_all = jnp.concatenate([cache_fox_logf[l].astype(f32), logf_s], axis=1).transpose(0, 2, 1)
        lf_all = jnp.pad(lf_all, ((0, 0), (0, 0), (0, lp_s - (P + Ls))))
        ao_s = fox_sample(h, cache_k_rows, cache_v_rows, l, cumsum_lanes(lf_all, cs_bl), Tp // Ls, Bs, Ls, P)

        bo_p, ret_p = retention(h, ret_gn_w[l], zeros_ret, pos_p, 0, B, S, lc)
        bo_s, ret_s = retention(h, ret_gn_w[l], _pack_ret_state(state_ret[l].astype(f32)), pos_s,
                                Tp // Ls, Bs, Ls, Ls)

        ssd_prm = (conv_w[l], conv_b[l], dt_bias[l], a_log[l], d_skip[l], ssm_norm_w[l])
        co_p, ssm_p = ssd(h, *ssd_prm, zeros_hist, zeros_ssm, 0, B, S, lc)
        hist_s = jnp.pad(state_conv[l].astype(f32), ((0, 0), (SUBLANES - (CONV_W - 1), 0), (0, 0)))
        co_s, ssm_s = ssd(h, *ssd_prm, hist_s, state_ssm[l].astype(f32).reshape(Bs, H_C // 2, LANES, LANES),
                          Tp // Ls, Bs, Ls, Ls)

        cat = lambda a, b: jnp.concatenate([a, b], axis=0)
        x = outproj_ln(x, cat(ao_p, ao_s), cat(bo_p, bo_s), cat(co_p, co_s), w_out[l].astype(bf16),
                       ln1_g[l], ln1_b[l], tm)

        mkv = matmul(mem_prompt.reshape(B * NM, D), wkv_mem[l].astype(bf16), _pick(B * NM, (512, 256, 128)),
                     _pick(2 * W_M, (512, 256)), "mem_kv")
        mk_p = mkv[:, :W_M].reshape(B, NM, W_M)
        mv_p = mkv[:, W_M:].reshape(B, NM, W_M)
        wq_b, wo_b = wq_mem[l].astype(bf16), wo_mem[l].astype(bf16)
        xx = mem_attn_ln(x, wq_b, mk_p, mv_p, wo_b, ln2_g[l], ln2_b[l], 0, B, S, tr_mem)
        x, xb = mem_attn_ln(x, wq_b, cache_mem_k[l].reshape(Bs, NM, W_M), cache_mem_v[l].reshape(Bs, NM, W_M),
                            wo_b, ln2_g[l], ln2_b[l], Tp // Ls, Bs, Ls, Ls, into=xx)

        sT = peer_scores(x, peer_wq[l].astype(bf16), peer_k1[l], peer_k2[l], tm)
        r2, g2, cnt, g1 = peer_topk(sT, tm_peer)
        pe_t = peer_main(xb, peer_u_b, peer_vt_b, l, r2, g2, cnt, g1, tm_peer)
        x = ln_residual_t(x, pe_t, ln3_g[l], ln3_b[l], tm)

        hs = h[Tp:].reshape(Bs, Ls, NH)
        conv_p = jnp.stack([h[(b + 1) * S - (CONV_W - 1):(b + 1) * S, XBC0:XBC0 + CONV_DIM] for b in range(B)])
        k_t = kv_sequence_minor(h, FK0, l, B, S, tq, into=k_t)
        v_t = kv_sequence_minor(h, FV0, l, B, S, tq, into=v_t)
        new_p = (None, None, logf_p, _unpack_ret_state(ret_p), ssm_p.reshape(B, H_C, P_C, N_C), conv_p,
                 mk_p.reshape(B, NM, H_M, DH_M), mv_p.reshape(B, NM, H_M, DH_M))
        new_s = (hs[..., FK0:FK0 + W_A].reshape(Bs, Ls, H_A, DH_A), hs[..., FV0:FV0 + W_A].reshape(Bs, Ls, H_A, DH_A),
                 logf_s, _unpack_ret_state(ret_s), ssm_s.reshape(Bs, H_C, P_C, N_C),
                 hs[:, Ls - (CONV_W - 1):, XBC0:XBC0 + CONV_DIM])
        for j, a in enumerate(new_p):
            st_p[j].append(a)
        for j, a in enumerate(new_s):
            st_s[j].append(a)

    outs_p = [k_t.transpose(0, 1, 4, 2, 3), v_t.transpose(0, 1, 4, 2, 3)] + [jnp.stack(a) for a in st_p[2:]]
    outs_s = [jnp.stack(a) for a in st_s]
    return (x[:Tp].reshape(B, S, D), x[Tp:].reshape(Bs, Ls, D), *outs_p, *outs_s)
```

```python
import functools
import math

import numpy as np
import jax
import jax.numpy as jnp
from jax import lax
from jax.experimental import pallas as pl
from jax.experimental.pallas import tpu as pltpu

f32 = jnp.float32
bf16 = jnp.bfloat16
NEG_INF = float("-inf")

LN_EPS = 1e-5
DEPTH = 4
ALPHA = (2 * DEPTH) ** 0.25
H_A, DH_A = 8, 64
H_B, DK_B = 8, 64
H_C, P_C, N_C, G_C = 16, 64, 128, 2
W_A, W_B, W_C = 512, 512, 1024
CONV_W = 4
CONV_DIM = W_C + 2 * G_C * N_C
ROPE_BASE = 10000.0
H_M, DH_M = 4, 128
W_M = H_M * DH_M
PEER_NK, PEER_HEADS, PEER_DK, PEER_TOPK = 128, 8, 256, 16

LANES = 128
SUBLANES = 8
BF16_ROWS = 16
HALF = 64

XBC0, FQ0, Z0, FK0, FV0, RQ0, RK0, RV0, RG0, TAIL0, NH = 0, 1536, 2048, 3072, 3584, 4096, 4608, 5120, 5632, 6144, 6400
FF_LANE0, DT_LANE0 = 0, 8


def _cparams(n_axes, vmem_mb=None):
    kw = dict(dimension_semantics=("arbitrary",) * n_axes)
    if vmem_mb is not None:
        kw["vmem_limit_bytes"] = vmem_mb << 20
    return pltpu.CompilerParams(**kw)


def _pick(n, prefs):
    for p in prefs:
        if n % p == 0:
            return p
    return n


def _dot(a, b):
    return jnp.dot(a, b, preferred_element_type=f32)


def _dot_nt(a, b):
    return lax.dot_general(a, b, (((1,), (1,)), ((), ())), preferred_element_type=f32)


def _dot_tn(a, b):
    return lax.dot_general(a, b, (((0,), (0,)), ((), ())), preferred_element_type=f32)


def _ln(x, g, b):
    mu = jnp.mean(x, -1, keepdims=True)
    xc = x - mu
    var = jnp.mean(xc * xc, -1, keepdims=True)
    return xc * lax.rsqrt(var + LN_EPS) * g + b


def _silu(x):
    return x * jax.nn.sigmoid(x)


def _softplus(x):
    return jnp.maximum(x, 0.0) + jnp.log1p(jnp.exp(-jnp.abs(x)))


def _ln_kernel(x_ref, g_ref, b_ref, o_ref):
    o_ref[...] = _ln(x_ref[...], g_ref[...], b_ref[...])


def _ln_into_kernel(x_ref, g_ref, b_ref, prev_ref, o_ref):
    del prev_ref
    o_ref[...] = _ln(x_ref[...], g_ref[...], b_ref[...])


def layer_norm_rows(x, g, b, tm, total_rows, row0=0, into=None):
    n, D = x.shape
    in_specs = [pl.BlockSpec((tm, D), lambda i: (i, 0)),
                pl.BlockSpec((1, D), lambda i: (0, 0)),
                pl.BlockSpec((1, D), lambda i: (0, 0))]
    args, aliases = [x, g.reshape(1, D), b.reshape(1, D)], {}
    if into is not None:
        in_specs.append(pl.BlockSpec(memory_space=pl.ANY))
        args.append(into)
        aliases = {3: 0}
    return pl.pallas_call(
        _ln_kernel if into is None else _ln_into_kernel, grid=(n // tm,),
        in_specs=in_specs,
        out_specs=pl.BlockSpec((tm, D), lambda i: (row0 + i, 0)),
        out_shape=jax.ShapeDtypeStruct((total_rows, D), f32),
        input_output_aliases=aliases,
        compiler_params=_cparams(1, 48), name="ln_in")(*args)


def _ln_res_kernel(x_ref, rt_ref, g_ref, b_ref, o_ref):
    o_ref[...] = _ln(ALPHA * x_ref[...] + rt_ref[...].T, g_ref[...], b_ref[...])


def ln_residual_t(x, rt, g, b, tm, row0=0, rows=None):
    T, D = x.shape
    rows = T if rows is None else rows
    return pl.pallas_call(
        _ln_res_kernel, grid=(rows // tm,),
        in_specs=[pl.BlockSpec((tm, D), lambda i: (row0 + i, 0)),
                  pl.BlockSpec((D, tm), lambda i: (0, row0 + i)),
                  pl.BlockSpec((1, D), lambda i: (0, 0)),
                  pl.BlockSpec((1, D), lambda i: (0, 0))],
        out_specs=pl.BlockSpec((tm, D), lambda i: (i, 0)),
        out_shape=jax.ShapeDtypeStruct((rows, D), f32),
        compiler_params=_cparams(1, 48), name="ln_res")(x, rt, g.reshape(1, D), b.reshape(1, D))


def _mm_kernel(x_ref, w_ref, o_ref, xb_ref):
    @pl.when(pl.program_id(1) == 0)
    def _():
        xb_ref[...] = x_ref[...].astype(bf16)

    o_ref[...] = _dot(xb_ref[...], w_ref[...]).astype(o_ref.dtype)


def matmul(x, w, tm, tn, name, layer=None):
    T, K = x.shape
    N = w.shape[-1]
    if layer is None:
        w_spec = pl.BlockSpec((K, tn), lambda i, j: (0, j))
    else:
        w_spec = pl.BlockSpec((None, K, tn), lambda i, j: (layer, 0, j))
    return pl.pallas_call(
        _mm_kernel, grid=(T // tm, N // tn),
        in_specs=[pl.BlockSpec((tm, K), lambda i, j: (i, 0)), w_spec],
        out_specs=pl.BlockSpec((tm, tn), lambda i, j: (i, j)),
        out_shape=jax.ShapeDtypeStruct((T, N), f32),
        scratch_shapes=[pltpu.VMEM((tm, K), bf16)],
        compiler_params=_cparams(2, 56), name=name)(x, w)


def _gate_kernel(t_ref, fb_ref, o_ref):
    x = t_ref[...] + fb_ref[...]
    o_ref[...] = jnp.minimum(x, 0.0) - jnp.log1p(jnp.exp(-jnp.abs(x)))


def forget_gate(h, fb_row, tm):
    T = h.shape[0]
    return pl.pallas_call(
        _gate_kernel, grid=(T // tm,),
        in_specs=[pl.BlockSpec((tm, LANES), lambda i: (i, TAIL0 // LANES)),
                  pl.BlockSpec((1, LANES), lambda i: (0, 0))],
        out_specs=pl.BlockSpec((tm, LANES), lambda i: (i, 0)),
        out_shape=jax.ShapeDtypeStruct((T, LANES), f32),
        compiler_params=_cparams(1), name="forget_gate")(h, fb_row)


def _cumsum_kernel(x_ref, o_ref, carry_ref):
    @pl.when(pl.program_id(1) == 0)
    def _():
        carry_ref[...] = jnp.zeros_like(carry_ref)

    bl = x_ref.shape[-1]
    r = lax.broadcasted_iota(jnp.int32, (bl, bl), 0)
    c = lax.broadcasted_iota(jnp.int32, (bl, bl), 1)
    upper = (r <= c).astype(f32)
    y = jnp.dot(x_ref[0], upper, precision=lax.Precision.HIGHEST,
                preferred_element_type=f32) + carry_ref[:, 0:1]
    o_ref[0] = y
    carry_ref[...] = jnp.broadcast_to(y[:, bl - 1:bl], carry_ref.shape)


def cumsum_lanes(x, bl):
    n, r, L = x.shape
    return pl.pallas_call(
        _cumsum_kernel, grid=(n, L // bl),
        in_specs=[pl.BlockSpec((1, r, bl), lambda s, j: (s, 0, j))],
        out_specs=pl.BlockSpec((1, r, bl), lambda s, j: (s, 0, j)),
        out_shape=jax.ShapeDtypeStruct((n, r, L), f32),
        scratch_shapes=[pltpu.VMEM((r, LANES), f32)],
        compiler_params=_cparams(2), name="cumsum")(x)


def _cumsum_rows_kernel(x_ref, o_ref, carry_ref):
    @pl.when(pl.program_id(1) == 0)
    def _():
        carry_ref[...] = jnp.zeros_like(carry_ref)

    bl = x_ref.shape[0]
    r = lax.broadcasted_iota(jnp.int32, (bl, bl), 0)
    c = lax.broadcasted_iota(jnp.int32, (bl, bl), 1)
    lower = (c <= r).astype(f32)
    y = jnp.dot(lower, x_ref[...], precision=lax.Precision.HIGHEST,
                preferred_element_type=f32) + carry_ref[0:1, :]
    o_ref[...] = y
    carry_ref[...] = jnp.broadcast_to(y[bl - 1:bl, :], carry_ref.shape)


def cumsum_rows(x, nseq, L, bl):
    nb = L // bl
    return pl.pallas_call(
        _cumsum_rows_kernel, grid=(nseq, nb),
        in_specs=[pl.BlockSpec((bl, LANES), lambda s, j: (s * nb + j, 0))],
        out_specs=pl.BlockSpec((bl, LANES), lambda s, j: (s * nb + j, 0)),
        out_shape=jax.ShapeDtypeStruct((nseq * L, LANES), f32),
        scratch_shapes=[pltpu.VMEM((SUBLANES, LANES), f32)],
        compiler_params=_cparams(2), name="cumsum_rows")(x)


_BIAS_TERMS = 3


def _split_heads(q, lane):
    zero = jnp.zeros_like(q)
    return jnp.where(lane, q, zero), jnp.where(lane, zero, q)


def _pack_queries(q, lane_i):
    qs = q * (DH_A ** -0.5)
    q0 = jnp.where(lane_i < HALF, qs, jnp.where(lane_i < HALF + _BIAS_TERMS, 1.0, 0.0))
    q1 = jnp.where(lane_i >= HALF, qs, jnp.where(lane_i < _BIAS_TERMS, 1.0, 0.0))
    return q0.astype(bf16), q1.astype(bf16)


def _pack_keys(k, c0, c1, lane_i):
    def one(own, c, base):
        out = jnp.where(own, k, 0.0)
        rest = -c
        for t in range(_BIAS_TERMS):
            term = rest.astype(bf16).astype(f32)
            out = jnp.where(lane_i == base + t, term, out)
            rest = rest - term
        return out.astype(bf16)
    return one(lane_i < HALF, c0, HALF), one(lane_i >= HALF, c1, 0)


def _pack_values(v, lane):
    return jnp.where(lane, v, 1.0).astype(bf16), jnp.where(lane, 1.0, v).astype(bf16)


def _attn_step(qs, ks, vs, carry, mask):
    out = []
    for q, k, v, (m, acc) in zip(qs, ks, vs, carry):
        s = _dot_nt(q, k)
        if mask is not None:
            s = jnp.where(mask, s, NEG_INF)
        mn = jnp.maximum(m, jnp.max(s, -1, keepdims=True))
        p = jnp.exp(s - mn).astype(bf16)
        out.append((mn, acc * jnp.exp(m - mn) + _dot(p, v)))
    return tuple(out)


def _attn_init(tq):
    return tuple((jnp.full((tq, 1), NEG_INF, f32), jnp.zeros((tq, LANES), f32)) for _ in range(2))


def _attn_finish(carry, lane):
    (_, acc0), (_, acc1) = carry
    o0 = acc0 / pltpu.roll(acc0, HALF, 1)
    o1 = acc1 / pltpu.roll(acc1, HALF, 1)
    return jnp.where(lane, o0, o1)


def _fox_prompt_kernel(q_ref, k_ref, v_ref, c_ref, o_ref, k0_ref, k1_ref, v0_ref, v1_ref, *, tq, tk):
    i = pl.program_id(2)
    per = tq // tk
    lane_k = lax.broadcasted_iota(jnp.int32, (tk, LANES), 1)
    lane_q = lax.broadcasted_iota(jnp.int32, (tq, LANES), 1)

    @pl.when(i == 0)
    def _():
        def pack(j, _):
            rows = pl.ds(pl.multiple_of(j * tk, tk), tk)
            c = c_ref[0, 0, rows, :]
            k0, k1 = _pack_keys(k_ref[rows, :], c[:, 0:1], c[:, 1:2], lane_k)
            v0, v1 = _pack_values(v_ref[rows, :], lane_k < HALF)
            k0_ref[rows, :] = k0
            k1_ref[rows, :] = k1
            v0_ref[rows, :] = v0
            v1_ref[rows, :] = v1
            return 0

        lax.fori_loop(0, k_ref.shape[0] // tk, pack, 0)

    qs = _pack_queries(q_ref[...], lane_q)

    def step(j, carry, mask):
        rows = pl.ds(pl.multiple_of(j * tk, tk), tk)
        return _attn_step(qs, (k0_ref[rows, :], k1_ref[rows, :]), (v0_ref[rows, :], v1_ref[rows, :]), carry, mask)

    group = 2 if per % 2 == 0 else 1

    def body(g, carry):
        for u in range(group):
            carry = step(g * group + u, carry, None)
        return carry

    carry = lax.fori_loop(0, i * (per // group), body, _attn_init(tq))
    r = lax.broadcasted_iota(jnp.int32, (tq, tk), 0)
    c = lax.broadcasted_iota(jnp.int32, (tq, tk), 1)
    for d in range(per):
        carry = step(i * per + d, carry, c + d * tk <= r)
    o_ref[...] = _attn_finish(carry, lane_q < HALF).astype(o_ref.dtype)


def fox_prompt(h, cp, B, S, tq, tk):
    nq = S // tq
    qb, kb, vb = FQ0 // LANES, FK0 // LANES, FV0 // LANES
    return pl.pallas_call(
        functools.partial(_fox_prompt_kernel, tq=tq, tk=tk), grid=(B, H_A // 2, nq),
        in_specs=[pl.BlockSpec((tq, LANES), lambda b, p, i: (b * nq + i, qb + p)),
                  pl.BlockSpec((S, LANES), lambda b, p, i: (b, kb + p)),
                  pl.BlockSpec((S, LANES), lambda b, p, i: (b, vb + p)),
                  pl.BlockSpec((1, 1, S, 2), lambda b, p, i: (b, p, 0, 0))],
        out_specs=pl.BlockSpec((tq, LANES), lambda b, p, i: (b * nq + i, p)),
        out_shape=jax.ShapeDtypeStruct((B * S, W_A), bf16),
        scratch_shapes=[pltpu.VMEM((S, LANES), bf16) for _ in range(4)],
        compiler_params=_cparams(3, 52), name="fox_prompt")(h, h, h, cp)


def _fox_sample_kernel(q_ref, kn_ref, vn_ref, kc_ref, vc_ref, c_ref, o_ref, m_ref, l_ref, acc_ref,
                       *, Ls, P, tk):
    j = pl.program_id(1)

    @pl.when(j == 0)
    def _():
        m_ref[...] = jnp.full(m_ref.shape, NEG_INF, f32)
        l_ref[...] = jnp.zeros_like(l_ref)
        acc_ref[...] = jnp.zeros_like(acc_ref)

    q = q_ref[...] * (DH_A ** -0.5)

    def head_step(hd, k, v, ck, mask, seq_on_lanes):
        cols = slice(hd * DH_A, (hd + 1) * DH_A)
        qh, kb, vb = q[:, cols].astype(bf16), k.astype(bf16), v.astype(bf16)
        s = (_dot(qh, kb) if seq_on_lanes else _dot_nt(qh, kb)) - ck
        if mask is not None:
            s = jnp.where(mask, s, NEG_INF)
        m = m_ref[hd]
        mn = jnp.maximum(m, jnp.max(s, -1, keepdims=True))
        a = jnp.exp(m - mn)
        p = jnp.exp(s - mn)
        pb = p.astype(bf16)
        l_ref[hd] = a * l_ref[hd] + jnp.sum(p, -1, keepdims=True)
        acc_ref[hd] = a * acc_ref[hd] + (_dot_nt(pb, vb) if seq_on_lanes else _dot(pb, vb))
        m_ref[hd] = mn

    off = pl.multiple_of(j * tk, tk)
    for hd in range(H_A):
        head_step(hd, kc_ref[hd], vc_ref[hd], c_ref[0, hd:hd + 1, pl.ds(off, tk)], None, True)

    @pl.when(j == pl.num_programs(1) - 1)
    def _():
        r = lax.broadcasted_iota(jnp.int32, (Ls, Ls), 0)
        c = lax.broadcasted_iota(jnp.int32, (Ls, Ls), 1)
        outs = []
        for hd in range(H_A):
            cols = slice(hd * DH_A, (hd + 1) * DH_A)
            head_step(hd, kn_ref[:, cols], vn_ref[:, cols], c_ref[0, hd:hd + 1, P:P + Ls], c <= r, False)
            outs.append(acc_ref[hd] / l_ref[hd])
        o_ref[...] = jnp.concatenate(outs, axis=1).astype(o_ref.dtype)


def fox_sample(h, cache_k, cache_v, layer, cT, row0, Bs, Ls, P):
    tk = _pick(P, (1024, 512, 256, 128))
    blk = lambda col0: pl.BlockSpec((Ls, W_A), lambda b, j: (row0 + b, col0 // W_A))
    cache = lambda: pl.BlockSpec((None, None, H_A, DH_A, tk), lambda b, j: (layer, b, 0, 0, j))
    return pl.pallas_call(
        functools.partial(_fox_sample_kernel, Ls=Ls, P=P, tk=tk), grid=(Bs, P // tk),
        in_specs=[blk(FQ0), blk(FK0), blk(FV0), cache(), cache(),
                  pl.BlockSpec((1, H_A, cT.shape[-1]), lambda b, j: (b, 0, 0))],
        out_specs=pl.BlockSpec((Ls, W_A), lambda b, j: (b, 0)),
        out_shape=jax.ShapeDtypeStruct((Bs * Ls, W_A), bf16),
        scratch_shapes=[pltpu.VMEM((H_A, Ls, 1), f32), pltpu.VMEM((H_A, Ls, 1), f32),
                        pltpu.VMEM((H_A, Ls, DH_A), f32)],
        compiler_params=_cparams(2, 48), name="fox_sample")(h, h, h, cache_k, cache_v, cT)


def _kv_out_kernel(*refs):
    x_ref, o_ref = refs[0], refs[-1]
    o_ref[...] = x_ref[...].T.reshape(2, DH_A, x_ref.shape[0])


def kv_sequence_minor(h, col0, layer, B, S, ts, into=None):
    nq = S // ts
    in_specs = [pl.BlockSpec((ts, LANES), lambda b, p, i: (b * nq + i, col0 // LANES + p))]
    args, aliases = [h], {}
    if into is not None:
        in_specs.append(pl.BlockSpec(memory_space=pl.ANY))
        args.append(into)
        aliases = {1: 0}
    return pl.pallas_call(
        _kv_out_kernel, grid=(B, H_A // 2, nq), in_specs=in_specs,
        out_specs=pl.BlockSpec((None, None, 2, DH_A, ts), lambda b, p, i: (layer, b, p, 0, i)),
        out_shape=jax.ShapeDtypeStruct((DEPTH, B, H_A, DH_A, S), f32),
        input_output_aliases=aliases,
        compiler_params=_cparams(3), name="kv_out")(*args)


def _pair_mean(x, lane):
    s0 = jnp.sum(jnp.where(lane, x, 0.0), -1, keepdims=True)
    s1 = jnp.sum(jnp.where(lane, 0.0, x), -1, keepdims=True)
    return jnp.where(lane, s0, s1) * (1.0 / HALF)


def _rotary(x, cos, sin_signed, first_half):
    xr = jnp.where(first_half, pltpu.roll(x, LANES - HALF // 2, 1), pltpu.roll(x, HALF // 2, 1))
    return x * cos + xr * sin_signed


def _retention_kernel(q_ref, k_ref, v_ref, g_ref, cos_ref, sin_ref, dec_ref, gq_ref, gk_ref, gl_ref,
                      s0_ref, gn_ref, o_ref, s1_ref, st_ref, *, Lc):
    c = pl.program_id(1)

    @pl.when(c == 0)
    def _():
        st_ref[...] = s0_ref[0]

    lane_i = lax.broadcasted_iota(jnp.int32, (Lc, LANES), 1)
    lane = lane_i < HALF
    first_half = (lane_i % HALF) < (HALF // 2)
    cos, sin = cos_ref[...], sin_ref[...]
    sr = lax.broadcasted_iota(jnp.int32, (LANES, LANES), 0) < HALF
    sc = lax.broadcasted_iota(jnp.int32, (LANES, LANES), 1) < HALF
    for p in range(H_B // 2):
        cols = slice(p * LANES, (p + 1) * LANES)
        q = _rotary(q_ref[:, cols], cos, sin, first_half)
        k = _rotary(k_ref[:, cols], cos, sin, first_half) * (DK_B ** -0.5)
        qb, kb, vb = q.astype(bf16), k.astype(bf16), v_ref[:, cols].astype(bf16)
        q0, q1 = _split_heads(qb, lane)
        a0 = (_dot_nt(q0, kb) * dec_ref[2 * p]).astype(bf16)
        a1 = (_dot_nt(q1, kb) * dec_ref[2 * p + 1]).astype(bf16)
        intra = jnp.where(lane, _dot(a0, vb), _dot(a1, vb))
        st = st_ref[p]
        o = intra + _dot(qb, st.astype(bf16)) * gq_ref[p]
        kd = (k * gk_ref[p]).astype(bf16)
        st_ref[p] = gl_ref[p, 0:1, :] * st + jnp.where(sr == sc, _dot_tn(kd, vb), 0.0)
        mu = _pair_mean(o, lane)
        d = o - mu
        var = _pair_mean(d * d, lane)
        on = d * lax.rsqrt(var + LN_EPS) * gn_ref[:, cols]
        o_ref[:, cols] = (_silu(g_ref[:, cols]) * on).astype(o_ref.dtype)

    @pl.when(c == pl.num_programs(1) - 1)
    def _():
        s1_ref[0] = st_ref[...]


def _retention_tables(pos, Lc):
    half = DK_B // 2
    inv = ROPE_BASE ** (-jnp.arange(half, dtype=f32) / half)
    ang = pos.astype(f32)[:, None] * inv[None, :]
    cos, sin = jnp.cos(ang), jnp.sin(ang)
    cos_t = jnp.tile(cos, (1, 4))
    sin_t = jnp.tile(jnp.concatenate([-sin, sin], -1), (1, 2))
    lg = jnp.log1p(-jnp.exp2(-5.0 - jnp.arange(H_B, dtype=f32)))
    i = jnp.arange(Lc, dtype=f32)
    diff = i[:, None] - i[None, :]
    dec = jnp.exp(jnp.where((diff >= 0)[None], diff[None] * lg[:, None, None], NEG_INF))
    pair = lambda t: jnp.repeat(t.reshape(t.shape[0], H_B // 2, 2), HALF, axis=-1)
    gq = pair(jnp.exp((i[:, None] + 1.0) * lg[None, :])).transpose(1, 0, 2)
    gk = pair(jnp.exp((Lc - 1.0 - i)[:, None] * lg[None, :])).transpose(1, 0, 2)
    gl = jnp.broadcast_to(pair(jnp.exp(Lc * lg)[None, :]).transpose(1, 0, 2), (H_B // 2, SUBLANES, LANES))
    return cos_t, sin_t, dec, gq, gk, gl


def retention(h, gn_w, s0, pos, row0, nseq, L, Lc):
    nch = L // Lc
    npair = H_B // 2
    cos_t, sin_t, dec, gq, gk, gl = _retention_tables(pos, Lc)
    blk = lambda col0: pl.BlockSpec((Lc, W_B), lambda s, c: (row0 + s * nch + c, col0 // W_B))
    full = lambda shape: pl.BlockSpec(shape, lambda s, c: (0,) * len(shape))
    return pl.pallas_call(
        functools.partial(_retention_kernel, Lc=Lc), grid=(nseq, nch),
        in_specs=[blk(RQ0), blk(RK0), blk(RV0), blk(RG0),
                  pl.BlockSpec((Lc, LANES), lambda s, c: (c, 0)),
                  pl.BlockSpec((Lc, LANES), lambda s, c: (c, 0)),
                  full((H_B, Lc, Lc)), full((npair, Lc, LANES)), full((npair, Lc, LANES)),
                  full((npair, SUBLANES, LANES)),
                  pl.BlockSpec((1, npair, LANES, LANES), lambda s, c: (s, 0, 0, 0)),
                  full((1, W_B))],
        out_specs=[pl.BlockSpec((Lc, W_B), lambda s, c: (s * nch + c, 0)),
                   pl.BlockSpec((1, npair, LANES, LANES), lambda s, c: (s, 0, 0, 0))],
        out_shape=[jax.ShapeDtypeStruct((nseq * L, W_B), bf16),
                   jax.ShapeDtypeStruct((nseq, npair, LANES, LANES), f32)],
        scratch_shapes=[pltpu.VMEM((npair, LANES, LANES), f32)],
        compiler_params=_cparams(2, 32), name="retention")(
            h, h, h, h, cos_t, sin_t, dec, gq, gk, gl, s0, gn_w.reshape(1, W_B))


def _pack_ret_state(s):
    n = s.shape[0]
    s = s.reshape(n, H_B // 2, 2, DK_B, DK_B)
    z = jnp.zeros_like(s[:, :, 0])
    top = jnp.concatenate([s[:, :, 0], z], -1)
    bot = jnp.concatenate([z, s[:, :, 1]], -1)
    return jnp.concatenate([top, bot], -2)


def _unpack_ret_state(s):
    n = s.shape[0]
    return jnp.stack([s[:, :, :HALF, :HALF], s[:, :, HALF:, HALF:]], 2).reshape(n, H_B, DK_B, DK_B)


def _ssd_kernel(xbc_ref, z_ref, t_ref, cw_ref, cb_ref, dtb_ref, alog_ref, dsk_ref, nw_ref, hist_ref, h0_ref,
                o_ref, h1_ref, xpad_ref, hs_ref, *, Lc):
    c = pl.program_id(1)

    @pl.when(c == 0)
    def _():
        xpad_ref[0:SUBLANES, :] = hist_ref[0]
        hs_ref[...] = h0_ref[0]

    xpad_ref[SUBLANES:SUBLANES + Lc, :] = xbc_ref[...]
    conv = cb_ref[...]
    for t in range(CONV_W):
        r0 = SUBLANES - (CONV_W - 1) + t
        conv = conv + xpad_ref[r0:r0 + Lc, :] * cw_ref[t:t + 1, :]
    xpad_ref[0:SUBLANES, :] = xpad_ref[Lc:Lc + SUBLANES, :]
    xc = _silu(conv)
    xs = xc[:, :W_C]
    bm = [xc[:, W_C + g * N_C:W_C + (g + 1) * N_C].astype(bf16) for g in range(G_C)]
    cm = [xc[:, W_C + (G_C + g) * N_C:W_C + (G_C + g + 1) * N_C].astype(bf16) for g in range(G_C)]

    lane_i = lax.broadcasted_iota(jnp.int32, (Lc, LANES), 1)
    lane = lane_i < HALF
    dt_valid = (lane_i >= DT_LANE0) & (lane_i < DT_LANE0 + H_C)
    dt = _softplus(t_ref[...] + dtb_ref[...])
    dta = jnp.where(dt_valid, dt * (-jnp.exp(alog_ref[...])), 0.0)
    r = lax.broadcasted_iota(jnp.int32, (Lc, Lc), 0)
    cidx = lax.broadcasted_iota(jnp.int32, (Lc, Lc), 1)
    tri = cidx <= r
    a_cs = jnp.dot(tri.astype(f32), dta, precision=lax.Precision.HIGHEST, preferred_element_type=f32)
    a_cs_t = a_cs.T
    dt_t = dt.T
    row_first = lax.broadcasted_iota(jnp.int32, (LANES, LANES), 0) < HALF

    ys = []
    for p in range(H_C // 2):
        g = (2 * p) // (H_C // G_C)
        xpair = xs[:, p * LANES:(p + 1) * LANES]
        xpair_b = xpair.astype(bf16)
        if p % (H_C // G_C // 2) == 0:
            cb = _dot_nt(cm[g], bm[g])
        yh, acol, dcol = [], [], []
        for hd in (2 * p, 2 * p + 1):
            li = DT_LANE0 + hd
            ac = a_cs[:, li:li + 1]
            seg = ac - a_cs_t[li:li + 1, :]
            w = cb * jnp.exp(jnp.where(tri, seg, NEG_INF)) * dt_t[li:li + 1, :]
            yh.append(_dot(w.astype(bf16), xpair_b))
            acol.append(ac)
            dcol.append(dt[:, li:li + 1])
        acs_pair = jnp.where(lane, acol[0], acol[1])
        dt_pair = jnp.where(lane, dcol[0], dcol[1])
        hs = hs_ref[p]
        y = jnp.where(lane, yh[0], yh[1]) + _dot_nt(cm[g], hs.astype(bf16)) * jnp.exp(acs_pair)
        ys.append(y)
        a_last = acs_pair[Lc - 1:Lc, :]
        to_end = jnp.exp(a_last - acs_pair) * dt_pair
        upd = _dot_tn((xpair * to_end).astype(bf16), bm[g])
        sdec = jnp.exp(jnp.where(row_first, acol[0][Lc - 1:Lc, :], acol[1][Lc - 1:Lc, :]))
        hs_ref[p] = sdec * hs + upd

    y = jnp.concatenate(ys, axis=1) + dsk_ref[...] * xs
    y = y * _silu(z_ref[...])
    y = y * lax.rsqrt(jnp.mean(y * y, -1, keepdims=True) + LN_EPS) * nw_ref[...]
    o_ref[...] = y.astype(o_ref.dtype)

    @pl.when(c == pl.num_programs(1) - 1)
    def _():
        h1_ref[0] = hs_ref[...]


def ssd(h, conv_w, conv_b, dt_bias, a_log, d_skip, norm_w, hist, h0, row0, nseq, L, Lc):
    nch = L // Lc
    lane_row = lambda v: jnp.zeros((1, LANES), f32).at[0, DT_LANE0:DT_LANE0 + H_C].set(v)
    cw = jnp.zeros((SUBLANES, CONV_DIM), f32).at[:CONV_W].set(conv_w)
    row = lambda w: pl.BlockSpec((1, w), lambda s, c: (0, 0))
    return pl.pallas_call(
        functools.partial(_ssd_kernel, Lc=Lc), grid=(nseq, nch),
        in_specs=[pl.BlockSpec((Lc, CONV_DIM), lambda s, c: (row0 + s * nch + c, XBC0 // CONV_DIM)),
                  pl.BlockSpec((Lc, W_C), lambda s, c: (row0 + s * nch + c, Z0 // W_C)),
                  pl.BlockSpec((Lc, LANES), lambda s, c: (row0 + s * nch + c, TAIL0 // LANES)),
                  pl.BlockSpec((SUBLANES, CONV_DIM), lambda s, c: (0, 0)),
                  row(CONV_DIM), row(LANES), row(LANES), row(W_C), row(W_C),
                  pl.BlockSpec((1, SUBLANES, CONV_DIM), lambda s, c: (s, 0, 0)),
                  pl.BlockSpec((1, H_C // 2, LANES, LANES), lambda s, c: (s, 0, 0, 0))],
        out_specs=[pl.BlockSpec((Lc, W_C), lambda s, c: (s * nch + c, 0)),
                   pl.BlockSpec((1, H_C // 2, LANES, LANES), lambda s, c: (s, 0, 0, 0))],
        out_shape=[jax.ShapeDtypeStruct((nseq * L, W_C), bf16),
                   jax.ShapeDtypeStruct((nseq, H_C // 2, LANES, LANES), f32)],
        scratch_shapes=[pltpu.VMEM((Lc + SUBLANES, CONV_DIM), f32),
                        pltpu.VMEM((H_C // 2, LANES, LANES), f32)],
        compiler_params=_cparams(2, 48), name="ssd")(
            h, h, h, cw, conv_b.reshape(1, CONV_DIM), lane_row(dt_bias), lane_row(a_log),
            jnp.repeat(d_skip, P_C).reshape(1, W_C), norm_w.reshape(1, W_C), hist, h0)


def _outproj_kernel(x_ref, a_ref, b_ref, c_ref, w_ref, g_ref, beta_ref, o_ref):
    mix = (_dot(a_ref[...], w_ref[0:W_A, :]) + _dot(b_ref[...], w_ref[W_A:W_A + W_B, :])
           + _dot(c_ref[...], w_ref[W_A + W_B:, :]))
    o_ref[...] = _ln(ALPHA * x_ref[...] + mix, g_ref[...], beta_ref[...])


def outproj_ln(x, ao, bo, co, w, g, b, tm):
    T, D = x.shape
    rows = lambda wd: pl.BlockSpec((tm, wd), lambda i: (i, 0))
    const = lambda s: pl.BlockSpec(s, lambda i: (0, 0))
    return pl.pallas_call(
        _outproj_kernel, grid=(T // tm,),
        in_specs=[rows(D), rows(W_A), rows(W_B), rows(W_C), const(w.shape), const((1, D)), const((1, D))],
        out_specs=rows(D),
        out_shape=jax.ShapeDtypeStruct((T, D), f32),
        compiler_params=_cparams(1, 52), name="outproj_ln")(x, ao, bo, co, w, g.reshape(1, D), b.reshape(1, D))


def _mem_body(x_ref, wq_ref, mk_ref, mv_ref, wo_ref, g_ref, b_ref, o_ref, ob_ref):
    x = x_ref[...]
    q = _dot(x.astype(bf16), wq_ref[...])
    outs = []
    for hd in range(H_M):
        sl = slice(hd * DH_M, (hd + 1) * DH_M)
        s = _dot_nt(q[:, sl].astype(bf16), mk_ref[0, :, sl].astype(bf16)) * (DH_M ** -0.5)
        p = jnp.exp(s - jnp.max(s, -1, keepdims=True))
        p = p / jnp.sum(p, -1, keepdims=True)
        outs.append(_dot(p.astype(bf16), mv_ref[0, :, sl].astype(bf16)))
    o = jnp.concatenate(outs, axis=1).astype(bf16)
    y = _ln(ALPHA * x + _dot(o, wo_ref[...]), g_ref[...], b_ref[...])
    o_ref[...] = y
    ob_ref[...] = y.astype(bf16)


def _mem_kernel_first(x_ref, wq_ref, mk_ref, mv_ref, wo_ref, g_ref, b_ref, o_ref, ob_ref):
    _mem_body(x_ref, wq_ref, mk_ref, mv_ref, wo_ref, g_ref, b_ref, o_ref, ob_ref)


def _mem_kernel_into(x_ref, wq_ref, mk_ref, mv_ref, wo_ref, g_ref, b_ref, prev_ref, prevb_ref, o_ref, ob_ref):
    del prev_ref, prevb_ref
    _mem_body(x_ref, wq_ref, mk_ref, mv_ref, wo_ref, g_ref, b_ref, o_ref, ob_ref)


def mem_attn_ln(x, wq, mk, mv, wo, g, b, row0, nseq, L, tr, into=None):
    T, D = x.shape
    per = L // tr
    nm = mk.shape[1]
    const = lambda s: pl.BlockSpec(s, lambda i: (0,) * len(s))
    in_specs = [pl.BlockSpec((tr, D), lambda i: (row0 + i, 0)), const(wq.shape),
                pl.BlockSpec((1, nm, W_M), lambda i: (i // per, 0, 0)),
                pl.BlockSpec((1, nm, W_M), lambda i: (i // per, 0, 0)),
                const(wo.shape), const((1, D)), const((1, D))]
    args = [x, wq, mk, mv, wo, g.reshape(1, D), b.reshape(1, D)]
    aliases = {}
    if into is not None:
        in_specs += [pl.BlockSpec(memory_space=pl.ANY), pl.BlockSpec(memory_space=pl.ANY)]
        aliases = {len(args): 0, len(args) + 1: 1}
        args += list(into)
    return pl.pallas_call(
        _mem_kernel_first if into is None else _mem_kernel_into, grid=(nseq * per,),
        in_specs=in_specs,
        out_specs=[pl.BlockSpec((tr, D), lambda i: (row0 + i, 0)), pl.BlockSpec((tr, D), lambda i: (row0 + i, 0))],
        out_shape=[jax.ShapeDtypeStruct((T, D), f32), jax.ShapeDtypeStruct((T, D), bf16)],
        input_output_aliases=aliases,
        compiler_params=_cparams(1, 48), name="mem_attn_ln")(*args)


def _peer_score_kernel(x_ref, wq_ref, k1_ref, k2_ref, o_ref):
    q = _dot(x_ref[...].astype(bf16), wq_ref[...])
    k1 = k1_ref[...].astype(bf16)
    k2 = k2_ref[...].astype(bf16)
    hk = PEER_DK // 2
    for hd in range(PEER_HEADS):
        q1 = q[:, hd * PEER_DK:hd * PEER_DK + hk].astype(bf16)
        q2 = q[:, hd * PEER_DK + hk:(hd + 1) * PEER_DK].astype(bf16)
        o_ref[hd, 0:PEER_NK, :] = _dot_nt(k1, q1)
        o_ref[hd, PEER_NK:2 * PEER_NK, :] = _dot_nt(k2, q2)


def peer_scores(x, wq, k1, k2, tm):
    T, D = x.shape
    const = lambda s: pl.BlockSpec(s, lambda i: (0, 0))
    return pl.pallas_call(
        _peer_score_kernel, grid=(T // tm,),
        in_specs=[pl.BlockSpec((tm, D), lambda i: (i, 0)), const(wq.shape), const(k1.shape), const(k2.shape)],
        out_specs=pl.BlockSpec((PEER_HEADS, 2 * PEER_NK, tm), lambda i: (0, 0, i)),
        out_shape=jax.ShapeDtypeStruct((PEER_HEADS, 2 * PEER_NK, T), f32),
        compiler_params=_cparams(1, 52), name="peer_scores")(x, wq, k1, k2)


_CAND_ROWS = PEER_TOPK + 7 * SUBLANES + SUBLANES


def _cand_flat_index(cw):
    idx = [0 * PEER_TOPK + b for b in range(PEER_TOPK)]
    for a in range(1, 8):
        idx += [a * PEER_TOPK + b for b in range(SUBLANES)]
    idx += [a * PEER_TOPK for a in range(8, PEER_TOPK)]
    return np.broadcast_to(np.asarray(idx, np.float32)[:, None], (_CAND_ROWS, cw)).copy()


def _extract_top(s, key_idx, n):
    rank = jnp.full(s.shape, float(n), f32)
    vals = []
    for a in range(n):
        mx = jnp.max(s, axis=0, keepdims=True)
        first = jnp.min(jnp.where(s == mx, key_idx, float(1 << 20)), axis=0, keepdims=True)
        sel = key_idx == first
        rank = jnp.where(sel, float(a), rank)
        s = jnp.where(sel, NEG_INF, s)
        vals.append(mx)
    return vals, rank


def _top_values(s, n, with_rank):
    rank = jnp.full(s.shape, float(n), f32) if with_rank else None
    vals = []
    for a in range(n):
        mx = jnp.max(s, axis=0, keepdims=True)
        hit = s == mx
        if with_rank:
            rank = jnp.where(hit, float(a), rank)
        s = jnp.where(hit, NEG_INF, s)
        vals.append(mx)
    return vals, rank


def _candidates(v1, v2):
    v1a = jnp.concatenate(v1, axis=0)
    v2a = jnp.concatenate(v2, axis=0)
    return jnp.concatenate([v1[0] + v2a] + [v1[a] + v2a[0:SUBLANES] for a in range(1, 8)]
                           + [v1a[SUBLANES:] + v2[0]], axis=0)


def _counts_per_rank(picked):
    cnt_a = [jnp.sum(picked[0:PEER_TOPK], axis=0, keepdims=True)]
    for a in range(1, 8):
        lo = PEER_TOPK + (a - 1) * SUBLANES
        cnt_a.append(jnp.sum(picked[lo:lo + SUBLANES], axis=0, keepdims=True))
    tail = picked[PEER_TOPK + 7 * SUBLANES:]
    return cnt_a + [tail[a:a + 1] for a in range(SUBLANES)]


def _select_fast(s1, s2):
    v1, _ = _top_values(s1, PEER_TOPK, False)
    v2, rank2 = _top_values(s2, PEER_TOPK, True)
    cand = _candidates(v1, v2)
    vc, _ = _top_values(cand, PEER_TOPK, False)
    top = v1[0] + v2[0]
    zsum = jnp.exp(vc[0] - top)
    for k in range(1, PEER_TOPK):
        zsum = zsum + jnp.exp(vc[k] - top)
    picked = jnp.where(cand >= vc[PEER_TOPK - 1], 1.0, 0.0)
    cnt_a = _counts_per_rank(picked)
    cnt = jnp.zeros(s1.shape, f32)
    for a in range(PEER_TOPK):
        cnt = jnp.where(s1 == v1[a], cnt_a[a], cnt)
    n1 = jnp.sum(jnp.where(s1 >= v1[PEER_TOPK - 1], 1.0, 0.0), axis=0, keepdims=True)
    n2 = jnp.sum(jnp.where(rank2 < float(PEER_TOPK), 1.0, 0.0), axis=0, keepdims=True)
    nc = jnp.sum(picked, axis=0, keepdims=True)
    k = float(PEER_TOPK)
    distinct = (n1 == k) & (n2 == k) & (nc == k)
    tie = jnp.max(jnp.where(distinct, 0.0, 1.0)) > 0.0
    return rank2, cnt, zsum, v1[0], v2[0], tie


def _select_exact(s1, s2, key_idx, cidx):
    v1, rank1 = _extract_top(s1, key_idx, PEER_TOPK)
    v2, rank2 = _extract_top(s2, key_idx, PEER_TOPK)
    cand = _candidates(v1, v2)
    top = v1[0] + v2[0]
    picked = jnp.zeros(cand.shape, f32)
    zsum = jnp.zeros((1, s1.shape[1]), f32)
    for _k in range(PEER_TOPK):
        mx = jnp.max(cand, axis=0, keepdims=True)
        first = jnp.min(jnp.where(cand == mx, cidx, float(1 << 20)), axis=0, keepdims=True)
        sel = cidx == first
        picked = jnp.where(sel, 1.0, picked)
        cand = jnp.where(sel, NEG_INF, cand)
        zsum = zsum + jnp.exp(mx - top)
    cnt_a = _counts_per_rank(picked)
    cnt = jnp.zeros(s1.shape, f32)
    for a in range(PEER_TOPK):
        cnt = jnp.where(rank1 == float(a), cnt_a[a], cnt)
    return rank2, cnt, zsum


def _peer_topk_kernel(s_ref, cidx_ref, r2_ref, g2_ref, cnt_ref, g1_ref, *, tb, cw):
    nj = PEER_NK // SUBLANES

    def chunk(ci, _):
        off = pl.multiple_of(ci * cw, cw)
        s1 = s_ref[0, 0:PEER_NK, pl.ds(off, cw)]
        s2 = s_ref[0, PEER_NK:2 * PEER_NK, pl.ds(off, cw)]
        rank2, cnt, zsum, m1, m2, tie = _select_fast(s1, s2)
        e1 = jnp.exp(s1 - m1)
        r2_ref[0, :, pl.ds(off, cw)] = rank2.astype(bf16)
        g2_ref[0, :, pl.ds(off, cw)] = jnp.exp(s2 - m2).astype(bf16)
        cnt_ref[:, 0, :, pl.ds(off, cw)] = cnt.reshape(nj, SUBLANES, cw)
        g1_ref[:, 0, :, pl.ds(off, cw)] = (e1 / zsum).reshape(nj, SUBLANES, cw)

        @pl.when(tie)
        def _():
            key_idx = lax.broadcasted_iota(jnp.int32, (PEER_NK, cw), 0).astype(f32)
            rank2x, cntx, zsumx = _select_exact(s1, s2, key_idx, cidx_ref[...])
            r2_ref[0, :, pl.ds(off, cw)] = rank2x.astype(bf16)
            cnt_ref[:, 0, :, pl.ds(off, cw)] = cntx.reshape(nj, SUBLANES, cw)
            g1_ref[:, 0, :, pl.ds(off, cw)] = (e1 / zsumx).reshape(nj, SUBLANES, cw)

        return 0

    lax.fori_loop(0, tb // cw, chunk, 0)


def peer_topk(sT, tb):
    T = sT.shape[-1]
    nj = PEER_NK // SUBLANES
    cw = _pick(tb, (2 * LANES, LANES))
    per_key = lambda: pl.BlockSpec((1, PEER_NK, tb), lambda hd, i: (hd, 0, i))
    per_row = lambda: pl.BlockSpec((nj, 1, SUBLANES, tb), lambda hd, i: (0, hd, 0, i))
    r2, g2, cnt, g1 = pl.pallas_call(
        functools.partial(_peer_topk_kernel, tb=tb, cw=cw), grid=(PEER_HEADS, T // tb),
        in_specs=[pl.BlockSpec((1, 2 * PEER_NK, tb), lambda hd, i: (hd, 0, i)),
                  pl.BlockSpec((_CAND_ROWS, cw), lambda hd, i: (0, 0))],
        out_specs=[per_key(), per_key(), per_row(), per_row()],
        out_shape=[jax.ShapeDtypeStruct((PEER_HEADS, PEER_NK, T), bf16),
                   jax.ShapeDtypeStruct((PEER_HEADS, PEER_NK, T), bf16),
                   jax.ShapeDtypeStruct((nj, PEER_HEADS, SUBLANES, T), f32),
                   jax.ShapeDtypeStruct((nj, PEER_HEADS, SUBLANES, T), f32)],
        compiler_params=_cparams(2, 32), name="peer_topk")(sT, jnp.asarray(_cand_flat_index(cw)))
    packed = (PEER_HEADS, PEER_NK // BF16_ROWS, BF16_ROWS, T)
    return (r2.reshape(packed), g2.reshape(packed),
            cnt.reshape(nj, PEER_HEADS * SUBLANES, T), g1.reshape(nj, PEER_HEADS * SUBLANES, T))


def _peer_main_kernel(x_ref, u_ref, vt_ref, r2_ref, g2_ref, cnt_ref, g1_ref, o_ref):
    j = pl.program_id(1)
    tm = x_ref.shape[0]

    @pl.when(j == 0)
    def _():
        o_ref[...] = jnp.zeros_like(o_ref)

    a = _dot_nt(u_ref[...], x_ref[...])
    act = (0.5 * a * (1.0 + lax.erf(a * (2.0 ** -0.5)))).astype(bf16)
    zero = jnp.zeros((), bf16)
    parts = []
    for r in range(SUBLANES):
        w = None
        for hd in range(PEER_HEADS):
            row = hd * SUBLANES + r
            cnt = jnp.broadcast_to(cnt_ref[0, row:row + 1, :], (BF16_ROWS, tm)).astype(bf16)[None]
            g1 = jnp.broadcast_to(g1_ref[0, row:row + 1, :], (BF16_ROWS, tm)).astype(bf16)[None]
            sel = jnp.where(r2_ref[hd] < cnt, g2_ref[hd], zero) * g1
            w = sel if w is None else w + sel
        parts.append(w.reshape(PEER_NK, tm) * act[r * PEER_NK:(r + 1) * PEER_NK])
    o_ref[...] += _dot(vt_ref[...], jnp.concatenate(parts, axis=0))


def peer_main(xb, ub, vtb, layer, r2, g2, cnt, g1, tm):
    T, D = xb.shape
    te = SUBLANES * PEER_NK
    nj = PEER_NK // SUBLANES
    once = pl.Buffered(1)
    per_tok = lambda: pl.BlockSpec((PEER_HEADS, PEER_NK // BF16_ROWS, BF16_ROWS, tm),
                                   lambda i, j: (0, 0, 0, i), pipeline_mode=once)
    per_row = lambda: pl.BlockSpec((1, PEER_HEADS * SUBLANES, tm), lambda i, j: (j, 0, i))
    return pl.pallas_call(
        _peer_main_kernel, grid=(T // tm, nj),
        in_specs=[pl.BlockSpec((tm, D), lambda i, j: (i, 0), pipeline_mode=once),
                  pl.BlockSpec((None, te, D), lambda i, j: (layer, j, 0)),
                  pl.BlockSpec((None, D, te), lambda i, j: (layer, 0, j)),
                  per_tok(), per_tok(), per_row(), per_row()],
        out_specs=pl.BlockSpec((D, tm), lambda i, j: (0, i), pipeline_mode=once),
        out_shape=jax.ShapeDtypeStruct((D, T), f32),
        compiler_params=_cparams(2, 58), name="peer_main")(xb, ub, vtb, r2, g2, cnt, g1)


def _rearrange_w_in(w):
    o = np.cumsum([0, W_A, W_A, W_A, H_A, H_B * DK_B, H_B * DK_B, W_B, W_B, W_C, CONV_DIM, H_C])
    fq, fk, fv, ff, rq, rk, rv, rg, z, xbc, dtr = [slice(int(o[i]), int(o[i + 1])) for i in range(11)]
    parts = [w[:, s] for s in (xbc, fq, z, fk, fv, rq, rk, rv, rg, ff, dtr)]
    pad = NH - TAIL0 - H_A - H_C
    return jnp.concatenate(parts + [jnp.zeros((w.shape[0], pad), w.dtype)], axis=1).astype(bf16)


def kernel(x_prompt, x_sample, cache_fox_k, cache_fox_v, cache_fox_logf, state_ret, state_ssm, state_conv, cache_mem_k, cache_mem_v, mem_prompt, ln_in_g, ln_in_b, w_in, fox_fb, ret_gn_w, conv_w, conv_b, dt_bias, a_log, d_skip, ssm_norm_w, w_out, ln1_g, ln1_b, wq_mem, wkv_mem, wo_mem, ln2_g, ln2_b, peer_wq, peer_k1, peer_k2, peer_u, peer_v, ln3_g, ln3_b):
    B, S, D = x_prompt.shape
    Bs, Ls, _ = x_sample.shape
    P = cache_fox_k.shape[2]
    NM = mem_prompt.shape[1]
    Tp, Ts = B * S, Bs * Ls
    T = Tp + Ts
    assert Tp % Ls == 0 and T % LANES == 0 and Ls >= CONV_W - 1

    tm = _pick(T, (640, 768, 384, 256, 128))
    tm_peer = _pick(T, (1280, 768, 256, 128))
    tq = _pick(S, (1024, 512, 256, 128))
    tk = min(tq, 512)
    lc = _pick(S, (256, 128, 64))
    tr_mem = _pick(S, (512, 256, 128))
    cs_bl = 512
    tn_in = _pick(NH, (1280, 640))
    tm_in = _pick(T, (1280, 768, 384, 256, 128))

    tm_s = _pick(Ts, (256, 128, 64, 32))
    assert Tp % tm_s == 0
    x = layer_norm_rows(x_prompt.reshape(Tp, D), ln_in_g, ln_in_b, _pick(Tp, (512, 256, 128)), T,
                        into=jnp.zeros((T, D), f32))
    x = layer_norm_rows(x_sample.reshape(Ts, D), ln_in_g, ln_in_b, tm_s, T, row0=Tp // tm_s, into=x)

    pos_p = jnp.arange(S)
    pos_s = P + jnp.arange(Ls)
    zeros_ret = jnp.zeros((B, H_B // 2, LANES, LANES), f32)
    zeros_ssm = jnp.zeros((B, H_C // 2, LANES, LANES), f32)
    zeros_hist = jnp.zeros((B, SUBLANES, CONV_DIM), f32)
    lp_s = -(-(P + Ls) // cs_bl) * cs_bl

    w_in_b = jax.vmap(_rearrange_w_in)(w_in)
    peer_u_b = peer_u.astype(bf16)
    cache_k_rows = cache_fox_k.transpose(0, 1, 3, 4, 2)
    cache_v_rows = cache_fox_v.transpose(0, 1, 3, 4, 2)
    peer_vt_b = peer_v.transpose(0, 2, 1).astype(bf16)

    k_t = v_t = jnp.zeros((DEPTH, B, H_A, DH_A, S), f32)
    stream0 = (jnp.zeros((T, D), f32), jnp.zeros((T, D), bf16))
    st_p = [[] for _ in range(8)]
    st_s = [[] for _ in range(6)]
    for l in range(DEPTH):
        h = matmul(x, w_in_b, tm_in, tn_in, "in_proj", layer=l)
        fb_row = jnp.zeros((1, LANES), f32).at[0, FF_LANE0:FF_LANE0 + H_A].set(fox_fb[l])
        logf_pad = forget_gate(h, fb_row, tm)
        logf = logf_pad[:, FF_LANE0:FF_LANE0 + H_A]
        logf_p = logf[:Tp].reshape(B, S, H_A)
        logf_s = logf[Tp:].reshape(Bs, Ls, H_A)

        c_p = cumsum_rows(logf_pad, B, S, min(cs_bl, S))[:, FF_LANE0:FF_LANE0 + H_A]
        c_p = c_p.reshape(B, S, H_A // 2, 2).transpose(0, 2, 1, 3)
        ao_p = fox_prompt(h, c_p, B, S, tq, tk)
        lf_all = jnp.concatenate([cache_fox_logf[l].astype(f32), logf_s], axis=1).transpose(0, 2, 1)
        lf_all = jnp.pad(lf_all, ((0, 0), (0, 0), (0, lp_s - (P + Ls))))
        ao_s = fox_sample(h, cache_k_rows, cache_v_rows, l, cumsum_lanes(lf_all, cs_bl), Tp // Ls, Bs, Ls, P)

        bo_p, ret_p = retention(h, ret_gn_w[l], zeros_ret, pos_p, 0, B, S, lc)
        bo_s, ret_s = retention(h, ret_gn_w[l], _pack_ret_state(state_ret[l].astype(f32)), pos_s,
                                Tp // Ls, Bs, Ls, Ls)

        ssd_prm = (conv_w[l], conv_b[l], dt_bias[l], a_log[l], d_skip[l], ssm_norm_w[l])
        co_p, ssm_p = ssd(h, *ssd_prm, zeros_hist, zeros_ssm, 0, B, S, lc)
        hist_s = jnp.pad(state_conv[l].astype(f32), ((0, 0), (SUBLANES - (CONV_W - 1), 0), (0, 0)))
        co_s, ssm_s = ssd(h, *ssd_prm, hist_s, state_ssm[l].astype(f32).reshape(Bs, H_C // 2, LANES, LANES),
                          Tp // Ls, Bs, Ls, Ls)

        cat = lambda a, b: jnp.concatenate([a, b], axis=0)
        x = outproj_ln(x, cat(ao_p, ao_s), cat(bo_p, bo_s), cat(co_p, co_s), w_out[l].astype(bf16),
                       ln1_g[l], ln1_b[l], tm)

        mkv = matmul(mem_prompt.reshape(B * NM, D), wkv_mem[l].astype(bf16), _pick(B * NM, (512, 256, 128)),
                     _pick(2 * W_M, (512, 256)), "mem_kv")
        mk_p = mkv[:, :W_M].reshape(B, NM, W_M)
        mv_p = mkv[:, W_M:].reshape(B, NM, W_M)
        wq_b, wo_b = wq_mem[l].astype(bf16), wo_mem[l].astype(bf16)
        xx = mem_attn_ln(x, wq_b, mk_p, mv_p, wo_b, ln2_g[l], ln2_b[l], 0, B, S, tr_mem, into=stream0)
        x, xb = mem_attn_ln(x, wq_b, cache_mem_k[l].reshape(Bs, NM, W_M), cache_mem_v[l].reshape(Bs, NM, W_M),
                            wo_b, ln2_g[l], ln2_b[l], Tp // Ls, Bs, Ls, Ls, into=xx)

        sT = peer_scores(x, peer_wq[l].astype(bf16), peer_k1[l], peer_k2[l], tm)
        r2, g2, cnt, g1 = peer_topk(sT, tm_peer)
        pe_t = peer_main(xb, peer_u_b, peer_vt_b, l, r2, g2, cnt, g1, tm_peer)
        if l < DEPTH - 1:
            x = ln_residual_t(x, pe_t, ln3_g[l], ln3_b[l], tm)
        else:
            tm_p, tm_l = _pick(Tp, (512, 256, 128)), _pick(Ts, (256, 128))
            y_p = ln_residual_t(x, pe_t, ln3_g[l], ln3_b[l], tm_p, 0, Tp)
            y_s = ln_residual_t(x, pe_t, ln3_g[l], ln3_b[l], tm_l, Tp // tm_l, Ts)

        hs = h[Tp:].reshape(Bs, Ls, NH)
        conv_p = jnp.stack([h[(b + 1) * S - (CONV_W - 1):(b + 1) * S, XBC0:XBC0 + CONV_DIM] for b in range(B)])
        k_t = kv_sequence_minor(h, FK0, l, B, S, tq, into=k_t)
        v_t = kv_sequence_minor(h, FV0, l, B, S, tq, into=v_t)
        new_p = (None, None, logf_p, _unpack_ret_state(ret_p), ssm_p.reshape(B, H_C, P_C, N_C), conv_p,
                 mk_p.reshape(B, NM, H_M, DH_M), mv_p.reshape(B, NM, H_M, DH_M))
        new_s = (hs[..., FK0:FK0 + W_A].reshape(Bs, Ls, H_A, DH_A), hs[..., FV0:FV0 + W_A].reshape(Bs, Ls, H_A, DH_A),
                 logf_s, _unpack_ret_state(ret_s), ssm_s.reshape(Bs, H_C, P_C, N_C),
                 hs[:, Ls - (CONV_W - 1):, XBC0:XBC0 + CONV_DIM])
        for j, a in enumerate(new_p):
            st_p[j].append(a)
        for j, a in enumerate(new_s):
            st_s[j].append(a)

    outs_p = [k_t.transpose(0, 1, 4, 2, 3), v_t.transpose(0, 1, 4, 2, 3)] + [jnp.stack(a) for a in st_p[2:]]
    outs_s = [jnp.stack(a) for a in st_s]
    return (y_p.reshape(B, S, D), y_s.reshape(Bs, Ls, D), *outs_p, *outs_s)
```
